```python
import math
import jax, jax.numpy as jnp
from jax import lax
import numpy as np

D_MODEL = 2048
BATCH = 16
SEQ = 256
DEPTH = 1
DEC_BATCH = 8
DEC_SEQ = 2048
PAST_LEN = 256

GRID_W = 64
D_RG = (3 * D_MODEL) // 4
RG_HEAD_DIM = 64
RG_HEADS = D_RG // RG_HEAD_DIM
RG_C = 8.0
CONV_W = 4
CONV_PAD = (1, 2)
D_S5 = D_MODEL - D_RG
S5_GROUP_CH = 16
S5_GROUPS = D_S5 // S5_GROUP_CH
S5_STATE = 64
D_MIX = D_RG + D_S5
D_IN = 2 * D_RG + D_S5
N_EXPERTS = 16
EC_CAPACITY = 2
D_EXPERT = D_MODEL
N_MOD = 6
EPS = 1e-6

kernel_name = "hybrid_rglru_s5_ec_diffusion_step"


def rmsnorm(x, g):
    xf = x.astype(jnp.float32)
    y = xf * lax.rsqrt(jnp.mean(xf * xf, axis=-1, keepdims=True) + EPS)
    return (y * g.astype(jnp.float32)).astype(x.dtype)


def modulate(x, g, shift, scale):
    return rmsnorm(x, g) * (1 + scale) + shift


def _lin_combine(l, r):
    al, bl = l
    ar, br = r
    return (al * ar, ar * bl + br)


def linear_scan(a, b, h0, reverse):
    acum, bcum = lax.associative_scan(_lin_combine, (a, b), reverse=reverse, axis=1)
    h = acum * h0[:, None] + bcum
    final = h[:, 0] if reverse else h[:, -1]
    return h, final


def _cplx_combine(l, r):
    alr, ali, blr, bli = l
    arr, ari, brr, bri = r
    return (alr * arr - ali * ari, alr * ari + ali * arr,
            arr * blr - ari * bli + brr, arr * bli + ari * blr + bri)


def dwconv_centred(x, w, b):
    C = x.shape[-1]
    y = lax.conv_general_dilated(x, w[:, None, :].astype(x.dtype), window_strides=(1,),
                                 padding=[CONV_PAD], dimension_numbers=('NWC', 'WIO', 'NWC'),
                                 feature_group_count=C)
    return y + b


def rglru_mixer(xb, gate, h0, p):
    B, T, _ = xb.shape
    xc = dwconv_centred(xb, p['rg_conv_w'], p['rg_conv_b']).astype(jnp.float32)
    xh = xc.reshape(B, T, RG_HEADS, RG_HEAD_DIM)
    r = jax.nn.sigmoid(jnp.einsum('bthi,dhij->dbthj', xh, p['rg_wa'].astype(jnp.float32)).reshape(2, B, T, D_RG)
                       + p['rg_ba'].astype(jnp.float32)[:, None, None])
    i = jax.nn.sigmoid(jnp.einsum('bthi,dhij->dbthj', xh, p['rg_wx'].astype(jnp.float32)).reshape(2, B, T, D_RG)
                       + p['rg_bx'].astype(jnp.float32)[:, None, None])
    log_a = -RG_C * r * jax.nn.softplus(-p['rg_lam'].astype(jnp.float32))[:, None, None]
    a = jnp.exp(log_a)
    b = jnp.sqrt(-jnp.expm1(2.0 * log_a)) * (i * xc[None])
    h0f = h0.astype(jnp.float32)
    h_f, fin_f = linear_scan(a[0], b[0], h0f[:, 0], False)
    h_b, fin_b = linear_scan(a[1], b[1], h0f[:, 1], True)
    y = (h_f + h_b) * jax.nn.gelu(gate.astype(jnp.float32))
    return y.astype(xb.dtype), jnp.stack([fin_f, fin_b], axis=1)


def s5_mixer(u, h0_re, h0_im, grid, p):
    B, T, _ = u.shape
    if grid:
        rows = T // GRID_W
        u = u.reshape(B, rows, GRID_W, D_S5).transpose(0, 2, 1, 3).reshape(B, T, D_S5)
    uf = u.astype(jnp.float32)
    ug = uf.reshape(B, T, S5_GROUPS, S5_GROUP_CH)
    a_re = p['s5_a_re'].astype(jnp.float32)
    a_im = p['s5_a_im'].astype(jnp.float32)
    dt = jnp.exp(p['s5_log_dt'].astype(jnp.float32))[..., None]
    mag = jnp.exp(dt * a_re)
    ab_re = mag * jnp.cos(dt * a_im)
    ab_im = mag * jnp.sin(dt * a_im)
    den = a_re * a_re + a_im * a_im
    n_re = ab_re - 1.0
    q_re = (n_re * a_re + ab_im * a_im) / den
    q_im = (ab_im * a_re - n_re * a_im) / den
    b_re = p['s5_b_re'].astype(jnp.float32)
    b_im = p['s5_b_im'].astype(jnp.float32)
    bb_re = q_re[..., None] * b_re - q_im[..., None] * b_im
    bb_im = q_re[..., None] * b_im + q_im[..., None] * b_re
    bu_re = jnp.einsum('btgk,dgpk->dbtgp', ug, bb_re)
    bu_im = jnp.einsum('btgk,dgpk->dbtgp', ug, bb_im)
    shape = (B, T, S5_GROUPS, S5_STATE)
    y = jnp.zeros((B, T, S5_GROUPS, S5_GROUP_CH), jnp.float32)
    fins_re, fins_im = [], []
    for dirn, rev in ((0, False), (1, True)):
        ar = jnp.broadcast_to(ab_re[dirn], shape)
        ai = jnp.broadcast_to(ab_im[dirn], shape)
        acr, aci, bcr, bci = lax.associative_scan(_cplx_combine, (ar, ai, bu_re[dirn], bu_im[dirn]),
                                                  reverse=rev, axis=1)
        h0r = h0_re[:, dirn, None].astype(jnp.float32)
        h0i = h0_im[:, dirn, None].astype(jnp.float32)
        hr = acr * h0r - aci * h0i + bcr
        hi = acr * h0i + aci * h0r + bci
        y = y + jnp.einsum('btgp,gkp->btgk', hr, p['s5_c_re'][dirn].astype(jnp.float32)) \
              - jnp.einsum('btgp,gkp->btgk', hi, p['s5_c_im'][dirn].astype(jnp.float32))
        idx = 0 if rev else T - 1
        fins_re.append(hr[:, idx])
        fins_im.append(hi[:, idx])
    y = y.reshape(B, T, D_S5) + p['s5_d'].astype(jnp.float32) * uf
    if grid:
        y = y.reshape(B, GRID_W, T // GRID_W, D_S5).transpose(0, 2, 1, 3).reshape(B, T, D_S5)
    y = jax.nn.gelu(y)
    y = y * jax.nn.sigmoid(y @ p['s5_w_glu'].astype(jnp.float32) + p['s5_b_glu'].astype(jnp.float32))
    return y.astype(u.dtype), jnp.stack(fins_re, axis=1), jnp.stack(fins_im, axis=1)


def expert_choice_ffn(h, w_router, w1, w3, w2):
    B, T, _ = h.shape
    cap = (EC_CAPACITY * T) // N_EXPERTS
    aff = jax.nn.softmax(jnp.einsum('btd,de->bte', h, w_router).astype(jnp.float32), axis=-1)
    gates, idx = lax.top_k(jnp.swapaxes(aff, 1, 2), cap)
    bidx = jnp.arange(B)[:, None, None]
    xg = h[bidx, idx]
    hid = jax.nn.silu(jnp.einsum('becd,edf->becf', xg, w1)) * jnp.einsum('becd,edf->becf', xg, w3)
    ye = jnp.einsum('becf,efd->becd', hid, w2) * gates[..., None].astype(h.dtype)
    return jnp.zeros_like(h).at[bidx, idx].add(ye)


def trunk_layer(x, mod, rg_h0, s5_h0_re, s5_h0_im, grid, p):
    shift1, scale1, gate1, shift2, scale2, gate2 = jnp.split(mod, N_MOD, axis=-1)
    h = modulate(x, p['norm1_g'], shift1, scale1)
    proj = h @ p['w_in']
    xb, gate, u = proj[..., :D_RG], proj[..., D_RG:2 * D_RG], proj[..., 2 * D_RG:]
    y_rg, fin_rg = rglru_mixer(xb, gate, rg_h0, p)
    y_s5, fin_re, fin_im = s5_mixer(u, s5_h0_re, s5_h0_im, grid, p)
    y = jnp.concatenate([rmsnorm(y_rg, p['gnorm_rg']), rmsnorm(y_s5, p['gnorm_s5'])], axis=-1) @ p['w_out']
    x = x + gate1 * y
    h2 = modulate(x, p['norm2_g'], shift2, scale2)
    x = x + gate2 * expert_choice_ffn(h2, p['w_router'], p['w1'], p['w3'], p['w2'])
    return x, fin_rg, fin_re, fin_im


def setup_inputs(seed: int = 0) -> dict:
    key = jax.random.key(seed)
    ks = jax.random.split(key, 40)
    f32 = jnp.float32
    nrm = lambda k, s, sc: jax.random.normal(k, s, f32) * sc
    u_lam = jax.random.uniform(ks[10], (DEPTH, 2, D_RG), f32, 0.9, 0.999)
    a0 = u_lam ** (1.0 / RG_C)
    rg_lam = jnp.log(a0) - jnp.log1p(-a0)
    n_idx = jnp.arange(S5_STATE, dtype=f32)
    return {
        "x_prompt": nrm(ks[0], (BATCH, SEQ, D_MODEL), 1.0),
        "x_sample": nrm(ks[1], (DEC_BATCH, DEC_SEQ, D_MODEL), 1.0),
        "state_rglru": nrm(ks[2], (DEC_BATCH, DEPTH, 2, D_RG), 0.5),
        "state_s5_re": nrm(ks[3], (DEC_BATCH, DEPTH, 2, S5_GROUPS, S5_STATE), 0.1),
        "state_s5_im": nrm(ks[4], (DEC_BATCH, DEPTH, 2, S5_GROUPS, S5_STATE), 0.1),
        "c": nrm(ks[5], (DEC_BATCH, D_MODEL), 1.0),
        "c_ctx": nrm(ks[6], (D_MODEL,), 1.0),
        "norm1_g": 1.0 + nrm(ks[7], (DEPTH, D_MODEL), 0.05),
        "norm2_g": 1.0 + nrm(ks[8], (DEPTH, D_MODEL), 0.05),
        "w_ada": nrm(ks[9], (DEPTH, D_MODEL, N_MOD * D_MODEL), 0.5 * D_MODEL ** -0.5),
        "b_ada": nrm(ks[11], (DEPTH, N_MOD * D_MODEL), 0.02),
        "w_in": nrm(ks[12], (DEPTH, D_MODEL, D_IN), D_MODEL ** -0.5),
        "rg_conv_w": nrm(ks[13], (DEPTH, CONV_W, D_RG), 0.5),
        "rg_conv_b": nrm(ks[14], (DEPTH, D_RG), 0.02),
        "rg_wa": nrm(ks[15], (DEPTH, 2, RG_HEADS, RG_HEAD_DIM, RG_HEAD_DIM), RG_HEAD_DIM ** -0.5),
        "rg_ba": nrm(ks[16], (DEPTH, 2, D_RG), 0.02),
        "rg_wx": nrm(ks[17], (DEPTH, 2, RG_HEADS, RG_HEAD_DIM, RG_HEAD_DIM), RG_HEAD_DIM ** -0.5),
        "rg_bx": nrm(ks[18], (DEPTH, 2, D_RG), 0.02),
        "rg_lam": rg_lam,
        "s5_a_re": -0.5 + nrm(ks[19], (DEPTH, 2, S5_GROUPS, S5_STATE), 0.02),
        "s5_a_im": math.pi * n_idx + nrm(ks[20], (DEPTH, 2, S5_GROUPS, S5_STATE), 0.02),
        "s5_log_dt": jax.random.uniform(ks[21], (DEPTH, 2, S5_GROUPS), f32, math.log(0.001), math.log(0.1)),
        "s5_b_re": nrm(ks[22], (DEPTH, 2, S5_GROUPS, S5_STATE, S5_GROUP_CH), (2 * S5_GROUP_CH) ** -0.5),
        "s5_b_im": nrm(ks[23], (DEPTH, 2, S5_GROUPS, S5_STATE, S5_GROUP_CH), (2 * S5_GROUP_CH) ** -0.5),
        "s5_c_re": nrm(ks[24], (DEPTH, 2, S5_GROUPS, S5_GROUP_CH, S5_STATE), (2 * S5_STATE) ** -0.5),
        "s5_c_im": nrm(ks[25], (DEPTH, 2, S5_GROUPS, S5_GROUP_CH, S5_STATE), (2 * S5_STATE) ** -0.5),
        "s5_d": nrm(ks[26], (DEPTH, D_S5), 1.0),
        "s5_w_glu": nrm(ks[27], (DEPTH, D_S5, D_S5), D_S5 ** -0.5),
        "s5_b_glu": nrm(ks[28], (DEPTH, D_S5), 0.02),
        "gnorm_rg": 1.0 + nrm(ks[29], (DEPTH, D_RG), 0.05),
        "gnorm_s5": 1.0 + nrm(ks[30], (DEPTH, D_S5), 0.05),
        "w_out": nrm(ks[31], (DEPTH, D_MIX, D_MODEL), D_MIX ** -0.5),
        "w_router": nrm(ks[32], (DEPTH, D_MODEL, N_EXPERTS), D_MODEL ** -0.5),
        "w1": nrm(ks[33], (DEPTH, N_EXPERTS, D_MODEL, D_EXPERT), D_MODEL ** -0.5),
        "w3": nrm(ks[34], (DEPTH, N_EXPERTS, D_MODEL, D_EXPERT), D_MODEL ** -0.5),
        "w2": nrm(ks[35], (DEPTH, N_EXPERTS, D_EXPERT, D_MODEL), D_EXPERT ** -0.5),
        "final_g": 1.0 + nrm(ks[36], (D_MODEL,), 0.05),
    }


def reference(x_prompt, x_sample, state_rglru, state_s5_re, state_s5_im, c, c_ctx,
              norm1_g, norm2_g, w_ada, b_ada, w_in, rg_conv_w, rg_conv_b, rg_wa, rg_ba, rg_wx, rg_bx,
              rg_lam, s5_a_re, s5_a_im, s5_log_dt, s5_b_re, s5_b_im, s5_c_re, s5_c_im, s5_d,
              s5_w_glu, s5_b_glu, gnorm_rg, gnorm_s5, w_out, w_router, w1, w3, w2, final_g):
    y_p = x_prompt
    y_s = x_sample
    Bp = x_prompt.shape[0]
    new_rg, new_re, new_im = [], [], []
    for l in range(DEPTH):
        p = dict(norm1_g=norm1_g[l], norm2_g=norm2_g[l], w_in=w_in[l], rg_conv_w=rg_conv_w[l],
                 rg_conv_b=rg_conv_b[l], rg_wa=rg_wa[l], rg_ba=rg_ba[l], rg_wx=rg_wx[l], rg_bx=rg_bx[l],
                 rg_lam=rg_lam[l], s5_a_re=s5_a_re[l], s5_a_im=s5_a_im[l], s5_log_dt=s5_log_dt[l],
                 s5_b_re=s5_b_re[l], s5_b_im=s5_b_im[l], s5_c_re=s5_c_re[l], s5_c_im=s5_c_im[l],
                 s5_d=s5_d[l], s5_w_glu=s5_w_glu[l], s5_b_glu=s5_b_glu[l], gnorm_rg=gnorm_rg[l],
                 gnorm_s5=gnorm_s5[l], w_out=w_out[l], w_router=w_router[l], w1=w1[l], w3=w3[l], w2=w2[l])
        mod_ctx = (jax.nn.silu(c_ctx) @ w_ada[l] + b_ada[l])[None, None]
        mod_lat = (jax.nn.silu(c) @ w_ada[l] + b_ada[l])[:, None]
        z_rg = jnp.zeros((Bp, 2, D_RG), jnp.float32)
        z_s5 = jnp.zeros((Bp, 2, S5_GROUPS, S5_STATE), jnp.float32)
        y_p, fin_rg, fin_re, fin_im = trunk_layer(y_p, mod_ctx, z_rg, z_s5, z_s5, False, p)
        new_rg.append(fin_rg)
        new_re.append(fin_re)
        new_im.append(fin_im)
        y_s, _, _, _ = trunk_layer(y_s, mod_lat, state_rglru[:, l], state_s5_re[:, l],
                                   state_s5_im[:, l], True, p)
    y_prompt = rmsnorm(y_p, final_g)
    y_sample = rmsnorm(y_s, final_g)
    new_state_rglru = jnp.stack(new_rg, axis=1)
    new_state_s5_re = jnp.stack(new_re, axis=1)
    new_state_s5_im = jnp.stack(new_im, axis=1)
    return (y_prompt, y_sample, new_state_rglru, new_state_s5_re, new_state_s5_im)
```

```python
import functools

import jax
import jax.numpy as jnp
from jax import lax
from jax.experimental import pallas as pl
from jax.experimental.pallas import tpu as pltpu

F32 = jnp.float32
BF16 = jnp.bfloat16
HIGHEST = lax.Precision.HIGHEST

LANES = 128
D_MODEL = 2048
D_RG = 1536
D_S5 = 512
D_IN = 2 * D_RG + D_S5
N_SLAB = D_IN // LANES
RG_SLABS = D_RG // LANES
S5_SLABS = D_S5 // LANES
S5_GROUPS = 32
GROUPS_PER_SLAB = 8
CHUNK = 16
N_EXPERTS = 16
N_MOD = 6
EPS = 1e-6
RG_C = 8.0
GRID_W = 64
MIB = 1024 * 1024


def _cparams(vmem_mib):
    return pltpu.CompilerParams(vmem_limit_bytes=vmem_mib * MIB)


def _dot(a, b):
    return jnp.dot(a, b, preferred_element_type=F32)


def _rms(x):
    return x * lax.rsqrt(jnp.mean(x * x, axis=-1, keepdims=True) + EPS)


def _adaln_kernel(c_ref, w_ref, b_ref, o_ref):
    s = jax.nn.silu(c_ref[...]).astype(BF16)
    o_ref[...] = _dot(s, w_ref[...].astype(BF16)) + b_ref[...]


def _adaln(c16, w_ada, b_ada):
    n = w_ada.shape[1]
    bn = 1024
    return pl.pallas_call(
        _adaln_kernel, grid=(n // bn,),
        in_specs=[pl.BlockSpec((16, D_MODEL), lambda j: (0, 0)),
                  pl.BlockSpec((D_MODEL, bn), lambda j: (0, j)),
                  pl.BlockSpec((1, bn), lambda j: (0, j))],
        out_specs=pl.BlockSpec((16, bn), lambda j: (0, j)),
        out_shape=jax.ShapeDtypeStruct((16, n), F32),
        compiler_params=_cparams(40), name="adaln")(c16, w_ada, b_ada.reshape(1, n))


def _inproj_kernel(x_ref, mod_ref, g_ref, w_ref, o_ref, *, B, tT, nch):
    x = x_ref[...]
    shift = mod_ref[:, 0:1, :]
    scale = mod_ref[:, 1:2, :]
    h = _rms(x) * g_ref[...] * (1.0 + scale) + shift
    hb = h.reshape(B * tT, D_MODEL).astype(BF16)
    for c0 in range(0, D_IN, nch):
        p = _dot(hb, w_ref[:, c0:c0 + nch])
        for b in range(B):
            for j in range(nch // LANES):
                o_ref[c0 // LANES + j, pl.ds(b, tT, stride=B), :] = (
                    p[b * tT:(b + 1) * tT, j * LANES:(j + 1) * LANES])


def _inproj(x, mod, g1, w_in_bf16, name):
    B, T, _ = x.shape
    tT = 512 // B
    kern = functools.partial(_inproj_kernel, B=B, tT=tT, nch=512)
    return pl.pallas_call(
        kern, grid=(T // tT,),
        in_specs=[pl.BlockSpec((B, tT, D_MODEL), lambda i: (0, i, 0)),
                  pl.BlockSpec((B, N_MOD, D_MODEL), lambda i: (0, 0, 0)),
                  pl.BlockSpec((1, D_MODEL), lambda i: (0, 0)),
                  pl.BlockSpec((D_MODEL, D_IN), lambda i: (0, 0), pipeline_mode=pl.Buffered(1))],
        out_specs=pl.BlockSpec((N_SLAB, tT * B, LANES), lambda i: (0, i, 0)),
        out_shape=jax.ShapeDtypeStruct((N_SLAB, T * B, LANES), F32),
        compiler_params=_cparams(56), name=name)(x, mod, g1, w_in_bf16)


def _rglru_kernel(xb_ref, gt_ref, cw_ref, cb_ref, wg_ref, bg_ref, lam_ref, h0_ref, y_ref, fin_ref,
                  *, B, T, CT):
    RC = CT * B
    nc = T // CT
    cw = [cw_ref[k:k + 1, :] for k in range(4)]
    cb = cb_ref[...]
    sp = [jax.nn.softplus(-lam_ref[d:d + 1, :]) for d in range(2)]

    def chunk(n, d):
        t0 = n * CT
        row0 = pl.multiple_of(t0 * B, RC)
        cur = xb_ref[0, pl.ds(row0, RC), :]
        p0 = pl.multiple_of(jnp.maximum(t0 - 1, 0) * B, B)
        prev = xb_ref[0, pl.ds(p0, B), :] * (t0 > 0).astype(F32)
        n0 = pl.multiple_of(jnp.minimum(t0 + CT, T - 2) * B, B)
        nxt = xb_ref[0, pl.ds(n0, 2 * B), :] * (t0 + CT < T).astype(F32)
        ext = jnp.concatenate([prev, cur, nxt], axis=0)
        xc = (cw[0] * ext[0:RC] + cw[1] * ext[B:B + RC] + cw[2] * ext[2 * B:2 * B + RC]
              + cw[3] * ext[3 * B:3 * B + RC] + cb)
        g = _dot(xc.astype(BF16), wg_ref[0, d]) + bg_ref[0, d]
        r = jax.nn.sigmoid(g[:, :LANES])
        i = jax.nn.sigmoid(g[:, LANES:])
        a = jnp.exp((-RG_C) * r * sp[d])
        bb = jnp.sqrt(1.0 - a * a) * (i * xc)
        return a, bb, row0

    def fwd(n, h):
        a, bb, row0 = chunk(n, 0)
        for t in range(CT):
            h = a[t * B:(t + 1) * B] * h + bb[t * B:(t + 1) * B]
            y_ref[0, pl.ds(row0 + t * B, B), :] = h
        return h

    fin_ref[0] = lax.fori_loop(0, nc, fwd, h0_ref[0])

    def bwd(k, h):
        a, bb, row0 = chunk(nc - 1 - k, 1)
        gl = jax.nn.gelu(gt_ref[0, pl.ds(row0, RC), :])
        for t in range(CT - 1, -1, -1):
            h = a[t * B:(t + 1) * B] * h + bb[t * B:(t + 1) * B]
            rows = pl.ds(row0 + t * B, B)
            y_ref[0, rows, :] = (y_ref[0, rows, :] + h) * gl[t * B:(t + 1) * B]
        return h

    fin_ref[1] = lax.fori_loop(0, nc, bwd, h0_ref[1])


def _rglru(proj, cw, cb, wg, bg, lam, h0, B, T, name):
    R = T * B
    kern = functools.partial(_rglru_kernel, B=B, T=T, CT=256 // B)
    return pl.pallas_call(
        kern, grid=(RG_SLABS,),
        in_specs=[pl.BlockSpec((1, R, LANES), lambda s: (s, 0, 0)),
                  pl.BlockSpec((1, R, LANES), lambda s: (s + RG_SLABS, 0, 0)),
                  pl.BlockSpec((4, LANES), lambda s: (0, s)),
                  pl.BlockSpec((1, LANES), lambda s: (0, s)),
                  pl.BlockSpec((1, 2, LANES, 2 * LANES), lambda s: (s, 0, 0, 0)),
                  pl.BlockSpec((1, 2, 1, 2 * LANES), lambda s: (s, 0, 0, 0)),
                  pl.BlockSpec((2, LANES), lambda s: (0, s)),
                  pl.BlockSpec((2, B, LANES), lambda s: (0, 0, s))],
        out_specs=[pl.BlockSpec((1, R, LANES), lambda s: (s, 0, 0)),
                   pl.BlockSpec((2, B, LANES), lambda s: (0, 0, s))],
        out_shape=[jax.ShapeDtypeStruct((RG_SLABS, R, LANES), F32),
                   jax.ShapeDtypeStruct((2, B, D_RG), F32)],
        compiler_params=_cparams(58), name=name)(proj, proj, cw, cb, wg, bg, lam, h0)


def _s5prep_kernel(ar_ref, ai_ref, ldt_ref, arc_ref, aic_ref, ldtc_ref, btr_ref, bti_ref, ctr_ref, cti_ref,
                   k_ref, p_ref, q_ref, l_ref):
    ar = ar_ref[0]
    ai = ai_ref[0]
    dt = jnp.exp(ldt_ref[0])
    mag = jnp.exp(dt * ar)
    lr = mag * jnp.cos(dt * ai)
    li = mag * jnp.sin(dt * ai)
    den = ar * ar + ai * ai
    nre = lr - 1.0
    qre = (nre * ar + li * ai) / den
    qim = (li * ar - nre * ai) / den
    btr = btr_ref[0]
    bti = bti_ref[0]
    bbr = qre * btr - qim * bti
    bbi = qre * bti + qim * btr
    rowblk = (lax.broadcasted_iota(jnp.int32, (256, LANES), 0) // CHUNK).astype(F32)
    isf = lax.broadcasted_iota(jnp.int32, (256, LANES), 1) < 64

    def powers(n):
        m = jnp.exp(n * (dt * ar))
        ang = n * (dt * ai)
        return m * jnp.cos(ang), m * jnp.sin(ang)

    ctr = ctr_ref[0]
    cti = cti_ref[0]

    wr, wi = powers(rowblk)
    zr = wr * bbr - wi * bbi
    zi = wr * bbi + wi * bbr
    zero = jnp.zeros_like(zr)

    def mm(a, b):
        return jnp.dot(a, b, precision=HIGHEST, preferred_element_type=F32)

    mall_f = mm(jnp.where(isf, zr, zero), ctr) - mm(jnp.where(isf, zi, zero), cti)
    mall_b = mm(jnp.where(isf, zero, zr), ctr) - mm(jnp.where(isf, zero, zi), cti)
    blocks = [mall_f[(15 - q) * CHUNK:(16 - q) * CHUNK] for q in range(15)]
    blocks.append(mall_f[0:CHUNK] + mall_b[0:CHUNK])
    blocks += [mall_b[l * CHUNK:(l + 1) * CHUNK] for l in range(1, 16)]
    blocks.append(jnp.zeros((CHUNK, 256), F32))
    gw = jnp.concatenate(blocks, axis=0)
    colblk = lax.broadcasted_iota(jnp.int32, (256, 256), 1) // CHUNK
    kmat = jnp.zeros((256, 256), F32)
    for tp in range(CHUNK):
        kmat = jnp.where(colblk == tp, gw[(15 - tp) * CHUNK:(15 - tp) * CHUNK + 256], kmat)
    k_ref[0] = kmat.astype(BF16)

    wr, wi = powers(jnp.where(isf, 15.0 - rowblk, rowblk))
    p_ref[0] = jnp.concatenate([wr * bbr - wi * bbi, wr * bbi + wi * bbr], axis=1).astype(BF16)

    arc = arc_ref[0]
    aic = aic_ref[0]
    dtc = jnp.exp(ldtc_ref[0])
    laneblk = (lax.broadcasted_iota(jnp.int32, (LANES, 256), 1) // CHUNK).astype(F32)
    rowf = lax.broadcasted_iota(jnp.int32, (LANES, 256), 0) < 64
    nq = jnp.where(rowf, laneblk + 1.0, 16.0 - laneblk)
    m = jnp.exp(nq * (dtc * arc))
    ang = nq * (dtc * aic)
    wr = m * jnp.cos(ang)
    wi = m * jnp.sin(ang)
    q_ref[0] = jnp.concatenate([ctr * wr - cti * wi, -(ctr * wi + cti * wr)], axis=0).astype(BF16)

    m16 = jnp.exp(16.0 * dt * ar)
    a16 = 16.0 * dt * ai
    l_ref[0] = jnp.concatenate([m16 * jnp.cos(a16), m16 * jnp.sin(a16)], axis=1)


def _s5prep(a_re, a_im, log_dt, b_re, b_im, c_re, c_im):
    G = S5_GROUPS

    def rows(x):
        return jnp.concatenate([x[0], x[1]], axis=-1).reshape(G, 1, LANES)

    ldt = jnp.broadcast_to(log_dt[:, :, None], (2, G, 64))
    ar, ai, ld = rows(a_re), rows(a_im), rows(ldt)
    arc, aic, ldc = (v.reshape(G, LANES, 1) for v in (ar, ai, ld))

    def btile(b):
        bt = jnp.tile(jnp.swapaxes(b, 2, 3), (1, 1, CHUNK, 1))
        return jnp.concatenate([bt[0], bt[1]], axis=-1)

    def ctile(c):
        ct = jnp.tile(jnp.swapaxes(c, 2, 3), (1, 1, 1, CHUNK))
        return jnp.concatenate([ct[0], ct[1]], axis=1)

    vec = pl.BlockSpec((1, 1, LANES), lambda g: (g, 0, 0))
    col = pl.BlockSpec((1, LANES, 1), lambda g: (g, 0, 0))
    bsp = pl.BlockSpec((1, 256, LANES), lambda g: (g, 0, 0))
    csp = pl.BlockSpec((1, LANES, 256), lambda g: (g, 0, 0))
    msp = pl.BlockSpec((1, 256, 256), lambda g: (g, 0, 0))
    return pl.pallas_call(
        _s5prep_kernel, grid=(G,),
        in_specs=[vec, vec, vec, col, col, col, bsp, bsp, csp, csp],
        out_specs=[msp, msp, msp, pl.BlockSpec((1, 1, 256), lambda g: (g, 0, 0))],
        out_shape=[jax.ShapeDtypeStruct((G, 256, 256), BF16)] * 3 + [jax.ShapeDtypeStruct((G, 1, 256), F32)],
        compiler_params=_cparams(32), name="s5prep")(
            ar, ai, ld, arc, aic, ldc, btile(b_re), btile(b_im), ctile(c_re), ctile(c_im))


def _s5_kernel(u_ref, k_ref, p_ref, q_ref, l_ref, h0_ref, y_ref, fin_ref, ebuf, ybuf, *, A, Bq, B, lat):
    RP = A * Bq
    nchunk = RP // B
    G = GROUPS_PER_SLAB
    tiles = [(a * Bq, [a]) for a in range(A)] if lat else [(0, list(range(A)))]
    nr = Bq * len(tiles[0][1])
    lane_blk = lax.broadcasted_iota(jnp.int32, (nr, LANES), 1) // CHUNK

    for r0, alist in tiles:
        xs = [jnp.concatenate([u_ref[0, a, tau] for a in alist], axis=0) for tau in range(CHUNK)]
        for g in range(G):
            halves = []
            for h in range(2):
                acc = None
                for tp in range(8):
                    sh = ((tp - g) * CHUNK) % LANES
                    r = pltpu.roll(xs[8 * h + tp], sh, axis=1) if sh else xs[8 * h + tp]
                    acc = r if acc is None else jnp.where(lane_blk == tp, r, acc)
                halves.append(acc)
            ug = jnp.concatenate(halves, axis=1).astype(BF16)
            ybuf[g, r0:r0 + nr, :] = _dot(ug, k_ref[g])
            ebuf[g, r0:r0 + nr, :] = _dot(ug, p_ref[g])

    def off(m):
        if lat:
            return ((m & 1) * GRID_W + (m >> 1)) * B
        return m * B

    fmask = (lax.broadcasted_iota(jnp.int32, (B, 2 * LANES), 1) % LANES) < 64
    fmask1 = lax.broadcasted_iota(jnp.int32, (B, LANES), 1) < 64
    lam = [l_ref[g] for g in range(G)]

    def body(i, carry):
        of = pl.multiple_of(off(i), B)
        ob = pl.multiple_of(off(nchunk - 1 - i), B)
        out = []
        for g in range(G):
            re, im = carry[2 * g], carry[2 * g + 1]
            ef = ebuf[g, pl.ds(of, B), :]
            eb = ebuf[g, pl.ds(ob, B), :]
            s = jnp.concatenate([re, im], axis=1)
            ebuf[g, pl.ds(of, B), :] = jnp.where(fmask, s, ef)
            ebuf[g, pl.ds(ob, B), :] = jnp.where(fmask, eb, s)
            lr = lam[g][:, :LANES]
            li = lam[g][:, LANES:]
            e_re = jnp.where(fmask1, ef[:, :LANES], eb[:, :LANES])
            e_im = jnp.where(fmask1, ef[:, LANES:], eb[:, LANES:])
            out.append(lr * re - li * im + e_re)
            out.append(lr * im + li * re + e_im)
        return tuple(out)

    init = []
    for g in range(G):
        h0 = h0_ref[g]
        init += [h0[:, :LANES], h0[:, LANES:]]
    fin = lax.fori_loop(0, nchunk, body, tuple(init))
    for g in range(G):
        fin_ref[g] = jnp.concatenate([fin[2 * g], fin[2 * g + 1]], axis=1)

    for g in range(G):
        ybuf[g] = ybuf[g] + _dot(ebuf[g].astype(BF16), q_ref[g])
    for r0, alist in tiles:
        ys = [[ybuf[g, r0:r0 + nr, h * LANES:(h + 1) * LANES] for h in range(2)] for g in range(G)]
        for tau in range(CHUNK):
            h, tp = divmod(tau, 8)
            acc = None
            for g in range(G):
                sh = ((g - tp) * CHUNK) % LANES
                r = pltpu.roll(ys[g][h], sh, axis=1) if sh else ys[g][h]
                acc = r if acc is None else jnp.where(lane_blk == g, r, acc)
            for k, a in enumerate(alist):
                y_ref[0, a, tau] = acc[k * Bq:(k + 1) * Bq]


def _s5(proj5, kmat, pmat, qmat, lam16, h0, A, Bq, B, lat, name):
    RP = A * Bq
    G = GROUPS_PER_SLAB
    kern = functools.partial(_s5_kernel, A=A, Bq=Bq, B=B, lat=lat)
    msp = pl.BlockSpec((G, 256, 256), lambda s: (s, 0, 0))
    return pl.pallas_call(
        kern, grid=(S5_SLABS,),
        in_specs=[pl.BlockSpec((1, A, CHUNK, Bq, LANES), lambda s: (s + 2 * RG_SLABS, 0, 0, 0, 0)),
                  msp, msp, msp,
                  pl.BlockSpec((G, 1, 256), lambda s: (s, 0, 0)),
                  pl.BlockSpec((G, B, 256), lambda s: (s, 0, 0))],
        out_specs=[pl.BlockSpec((1, A, CHUNK, Bq, LANES), lambda s: (s, 0, 0, 0, 0)),
                   pl.BlockSpec((G, B, 256), lambda s: (s, 0, 0))],
        out_shape=[jax.ShapeDtypeStruct((S5_SLABS, A, CHUNK, Bq, LANES), F32),
                   jax.ShapeDtypeStruct((S5_GROUPS, B, 256), F32)],
        scratch_shapes=[pltpu.VMEM((G, RP, 256), F32), pltpu.VMEM((G, RP, 256), F32)],
        compiler_params=_cparams(58), name=name)(proj5, kmat, pmat, qmat, lam16, h0)


def _post_kernel(yr_ref, ys_ref, u_ref, x_ref, mod_ref, wo_ref, wg_ref, bgl_ref, d_ref, grg_ref, gs5_ref,
                 o_ref, ybuf, *, B, tT):
    yr = jnp.concatenate([yr_ref[s] for s in range(RG_SLABS)], axis=1)
    nr = (_rms(yr) * grg_ref[...]).astype(BF16)
    ys = jnp.concatenate([ys_ref[s] for s in range(S5_SLABS)], axis=1)
    u = jnp.concatenate([u_ref[s] for s in range(S5_SLABS)], axis=1)
    ys = jax.nn.gelu(ys + d_ref[...] * u)
    ys = ys * jax.nn.sigmoid(_dot(ys.astype(BF16), wg_ref[...]) + bgl_ref[...])
    ns = (_rms(ys) * gs5_ref[...]).astype(BF16)
    y = _dot(nr, wo_ref[0:D_RG, :]) + _dot(ns, wo_ref[D_RG:D_MODEL, :])
    nj = D_MODEL // LANES
    for j in range(nj):
        ybuf[j] = y[:, j * LANES:(j + 1) * LANES]
    for b in range(B):
        yb = jnp.concatenate([ybuf[j, pl.ds(b, tT, stride=B), :] for j in range(nj)], axis=1)
        o_ref[b] = x_ref[b] + mod_ref[b, 2:3, :] * yb


def _post(y_rg, y_ssm, proj, x, mod, w_out_bf16, w_glu_bf16, b_glu, s5_d, g_rg, g_s5, name):
    B, T, _ = x.shape
    tT = 512 // B
    rows = tT * B
    kern = functools.partial(_post_kernel, B=B, tT=tT)
    const = lambda i: (0, 0)
    return pl.pallas_call(
        kern, grid=(T // tT,),
        in_specs=[pl.BlockSpec((RG_SLABS, rows, LANES), lambda i: (0, i, 0)),
                  pl.BlockSpec((S5_SLABS, rows, LANES), lambda i: (0, i, 0)),
                  pl.BlockSpec((S5_SLABS, rows, LANES), lambda i: (2 * RG_SLABS // S5_SLABS, i, 0)),
                  pl.BlockSpec((B, tT, D_MODEL), lambda i: (0, i, 0)),
                  pl.BlockSpec((B, N_MOD, D_MODEL), lambda i: (0, 0, 0)),
                  pl.BlockSpec((D_MODEL, D_MODEL), const, pipeline_mode=pl.Buffered(1)),
                  pl.BlockSpec((D_S5, D_S5), const),
                  pl.BlockSpec((1, D_S5), const),
                  pl.BlockSpec((1, D_S5), const),
                  pl.BlockSpec((1, D_RG), const),
                  pl.BlockSpec((1, D_S5), const)],
        out_specs=pl.BlockSpec((B, tT, D_MODEL), lambda i: (0, i, 0)),
        out_shape=jax.ShapeDtypeStruct((B, T, D_MODEL), F32),
        scratch_shapes=[pltpu.VMEM((D_MODEL // LANES, rows, LANES), F32)],
        compiler_params=_cparams(56), name=name)(
            y_rg, y_ssm, proj, x, mod, w_out_bf16, w_glu_bf16, b_glu, s5_d, g_rg, g_s5)


def _premoe_kernel(x_ref, mod_ref, g_ref, wr_ref, h_ref, a_ref):
    h = _rms(x_ref[...]) * g_ref[...] * (1.0 + mod_ref[0, 4:5, :]) + mod_ref[0, 3:4, :]
    h_ref[...] = h.astype(BF16)
    lg = lax.dot_general(wr_ref[...], h, (((1,), (1,)), ((), ())), precision=HIGHEST,
                         preferred_element_type=F32)
    e = jnp.exp(lg - jnp.max(lg, axis=0, keepdims=True))
    a_ref[...] = e / jnp.sum(e, axis=0, keepdims=True)


def _premoe(x1, mod, g2, w_router_t, T, name):
    n = x1.shape[0]
    rows = 512
    per_req = T // rows if mod.shape[0] > 1 else n
    return pl.pallas_call(
        _premoe_kernel, grid=(n // rows,),
        in_specs=[pl.BlockSpec((rows, D_MODEL), lambda i: (i, 0)),
                  pl.BlockSpec((1, N_MOD, D_MODEL), lambda i: (i // per_req, 0, 0)),
                  pl.BlockSpec((1, D_MODEL), lambda i: (0, 0)),
                  pl.BlockSpec((N_EXPERTS, D_MODEL), lambda i: (0, 0))],
        out_specs=[pl.BlockSpec((rows, D_MODEL), lambda i: (i, 0)),
                   pl.BlockSpec((N_EXPERTS, rows), lambda i: (0, i))],
        out_shape=[jax.ShapeDtypeStruct((n, D_MODEL), BF16),
                   jax.ShapeDtypeStruct((N_EXPERTS, n), F32)],
        compiler_params=_cparams(40), name=name)(x1, mod, g2, w_router_t)


def _topk_kernel(a_ref, u_ref, s_ref, st_ref, *, cap, T, nreq):
    capf = float(cap)

    def count_ge(r, v):
        return jnp.sum((a_ref[:, r * T:(r + 1) * T] >= v).astype(F32), axis=1, keepdims=True)

    def bit_step(i, thrs):
        bit = jnp.int32(1) << (30 - i)
        out = []
        for r in range(nreq):
            cand = thrs[r] | bit
            ok = count_ge(r, pltpu.bitcast(cand, F32)) >= capf
            out.append(jnp.where(ok, cand, thrs[r]))
        return tuple(out)

    zero = jnp.zeros((N_EXPERTS, 1), jnp.int32)
    thrs = lax.fori_loop(0, 31, bit_step, (zero,) * nreq)

    def mid_step(i, c):
        out = []
        for r in range(nreq):
            lo, hi = c[2 * r], c[2 * r + 1]
            mid = 0.5 * (lo + hi)
            ok = count_ge(r, mid) >= capf
            out += [jnp.where(ok, mid, lo), jnp.where(ok, hi, mid)]
        return tuple(out)

    init = []
    for r in range(nreq):
        init += [pltpu.bitcast(thrs[r], F32), pltpu.bitcast(thrs[r] + 1, F32)]
    bounds = lax.fori_loop(0, 24, mid_step, tuple(init))

    u = u_ref[...]
    for r in range(nreq):
        aff = a_ref[:, r * T:(r + 1) * T]
        lo, hi = bounds[2 * r], bounds[2 * r + 1]
        gt = aff >= hi
        eq = (aff >= lo) & jnp.logical_not(gt)
        need = capf - jnp.sum(gt.astype(F32), axis=1, keepdims=True)
        pe = _dot(eq.astype(F32).astype(BF16), u)
        sel = gt | (eq & (pe <= need))
        ps = _dot(sel.astype(F32).astype(BF16), u)
        slot = jnp.where(sel, ps - 1.0, -1.0)
        s_ref[:, r * T:(r + 1) * T] = slot
        st_ref[r * T:(r + 1) * T, :] = slot.T


def _topk(aff_t, tri, T, name):
    n = aff_t.shape[1]
    kern = functools.partial(_topk_kernel, cap=T // 8, T=T, nreq=n // T)
    return pl.pallas_call(
        kern, grid=(1,),
        in_specs=[pl.BlockSpec((N_EXPERTS, n), lambda i: (0, 0)),
                  pl.BlockSpec((T, T), lambda i: (0, 0))],
        out_specs=[pl.BlockSpec((N_EXPERTS, n), lambda i: (0, 0)),
                   pl.BlockSpec((n, N_EXPERTS), lambda i: (0, 0))],
        out_shape=[jax.ShapeDtypeStruct((N_EXPERTS, n), F32),
                   jax.ShapeDtypeStruct((n, N_EXPERTS), F32)],
        compiler_params=_cparams(48), name=name)(aff_t, tri)


def _gather_kernel(h_ref, s_ref, a_ref, xg_ref, gt_ref, *, cap, T):
    e = pl.program_id(1)
    srow = s_ref[pl.ds(e, 1), :]
    arow = a_ref[pl.ds(e, 1), :]
    j = lax.broadcasted_iota(jnp.int32, (cap, T), 0).astype(F32)
    oh = srow == j
    xg_ref[0] = _dot(oh.astype(F32).astype(BF16), h_ref[...]).astype(BF16)
    gt_ref[0] = jnp.sum(jnp.where(oh, arow, 0.0), axis=1, keepdims=True)


def _gather(h2, slot, aff_t, T, name):
    n = h2.shape[0]
    cap = T // 8
    nreq = n // T
    kern = functools.partial(_gather_kernel, cap=cap, T=T)
    return pl.pallas_call(
        kern, grid=(nreq, N_EXPERTS),
        in_specs=[pl.BlockSpec((T, D_MODEL), lambda b, e: (b, 0)),
                  pl.BlockSpec((N_EXPERTS, T), lambda b, e: (0, b)),
                  pl.BlockSpec((N_EXPERTS, T), lambda b, e: (0, b))],
        out_specs=[pl.BlockSpec((1, cap, D_MODEL), lambda b, e: (e, b, 0)),
                   pl.BlockSpec((1, cap, 1), lambda b, e: (e, b, 0))],
        out_shape=[jax.ShapeDtypeStruct((N_EXPERTS, nreq * cap, D_MODEL), BF16),
                   jax.ShapeDtypeStruct((N_EXPERTS, nreq * cap, 1), F32)],
        compiler_params=_cparams(48), name=name)(h2, slot, aff_t)


def _ffn_kernel(xc_ref, xl_ref, gc_ref, gl_ref, w1_ref, w3_ref, w2_ref, yc_ref, yl_ref, xbuf, acc, *, rc, nf, rsub):
    f = pl.program_id(2)

    @pl.when(f == 0)
    def _():
        xbuf[0:rc, :] = xc_ref[0]
        xbuf[rc:, :] = xl_ref[0]

    @pl.when(f == 0)
    def _():
        acc[...] = jnp.zeros_like(acc)

    w1 = w1_ref[0].astype(BF16)
    w3 = w3_ref[0].astype(BF16)
    w2 = w2_ref[0].astype(BF16)
    rows = xbuf.shape[0]
    for r0 in range(0, rows, rsub):
        x = xbuf[r0:r0 + rsub, :]
        hid = jax.nn.silu(_dot(x, w1)) * _dot(x, w3)
        acc[r0:r0 + rsub, :] += _dot(hid.astype(BF16), w2)

    @pl.when(f == nf - 1)
    def _():
        yc_ref[0] = (acc[0:rc, :] * gc_ref[0]).astype(BF16)
        yl_ref[0] = (acc[rc:, :] * gl_ref[0]).astype(BF16)


def _ffn(xg_c, xg_l, gt_c, gt_l, w1, w3, w2):
    rc = xg_c.shape[1] // 2
    rl = xg_l.shape[1] // 2
    bf = 256
    nf = D_MODEL // bf
    kern = functools.partial(_ffn_kernel, rc=rc, nf=nf, rsub=(rc + rl) // 4)
    rowsp = lambda r, w: pl.BlockSpec((1, r, w), lambda e, h, f: (e, h, 0))
    return pl.pallas_call(
        kern, grid=(N_EXPERTS, 2, nf),
        in_specs=[rowsp(rc, D_MODEL), rowsp(rl, D_MODEL), rowsp(rc, 1), rowsp(rl, 1),
                  pl.BlockSpec((1, D_MODEL, bf), lambda e, h, f: (e, 0, f)),
                  pl.BlockSpec((1, D_MODEL, bf), lambda e, h, f: (e, 0, f)),
                  pl.BlockSpec((1, bf, D_MODEL), lambda e, h, f: (e, f, 0))],
        out_specs=[rowsp(rc, D_MODEL), rowsp(rl, D_MODEL)],
        out_shape=[jax.ShapeDtypeStruct(xg_c.shape, BF16), jax.ShapeDtypeStruct(xg_l.shape, BF16)],
        scratch_shapes=[pltpu.VMEM((rc + rl, D_MODEL), BF16), pltpu.VMEM((rc + rl, D_MODEL), F32)],
        compiler_params=_cparams(56), name="ffn")(xg_c, xg_l, gt_c, gt_l, w1, w3, w2)


def _combine_kernel(x_ref, y_ref, st_ref, mod_ref, fg_ref, o_ref, *, cap, TB):
    K = N_EXPERTS * cap
    st = st_ref[...]
    expand = (lax.broadcasted_iota(jnp.int32, (N_EXPERTS, K), 0)
              == lax.broadcasted_iota(jnp.int32, (N_EXPERTS, K), 1) // cap)
    rep = _dot(st.astype(BF16), expand.astype(F32).astype(BF16))
    jl = (lax.broadcasted_iota(jnp.int32, (TB, K), 1) % cap).astype(F32)
    oh = (rep == jl).astype(F32).astype(BF16)
    moe = _dot(oh, y_ref[...].reshape(K, D_MODEL))
    x2 = x_ref[...] + mod_ref[0, 5:6, :] * moe
    o_ref[...] = _rms(x2) * fg_ref[...]


def _combine(x1, ybuf, slot_t, mod, final_g, T, name):
    n = x1.shape[0]
    cap = T // 8
    TB = 256
    nb = T // TB
    multi = mod.shape[0] > 1
    kern = functools.partial(_combine_kernel, cap=cap, TB=TB)
    return pl.pallas_call(
        kern, grid=(n // T, nb),
        in_specs=[pl.BlockSpec((TB, D_MODEL), lambda b, k: (b * nb + k, 0)),
                  pl.BlockSpec((N_EXPERTS, cap, D_MODEL), lambda b, k: (0, b, 0)),
                  pl.BlockSpec((TB, N_EXPERTS), lambda b, k: (b * nb + k, 0)),
                  pl.BlockSpec((1, N_MOD, D_MODEL), lambda b, k: (b if multi else 0, 0, 0)),
                  pl.BlockSpec((1, D_MODEL), lambda b, k: (0, 0))],
        out_specs=pl.BlockSpec((TB, D_MODEL), lambda b, k: (b * nb + k, 0)),
        out_shape=jax.ShapeDtypeStruct((n, D_MODEL), F32),
        compiler_params=_cparams(56), name=name)(x1, ybuf, slot_t, mod, final_g)


def _blockdiag_pairs(w):
    w = w.reshape(2, RG_SLABS, 2, 64, 64)
    bd = jnp.einsum('dshij,hk->dshikj', w, jnp.eye(2, dtype=w.dtype))
    return bd.reshape(2, RG_SLABS, LANES, LANES).transpose(1, 0, 2, 3)


def kernel(x_prompt, x_sample, state_rglru, state_s5_re, state_s5_im, c, c_ctx, norm1_g, norm2_g, w_ada, b_ada, w_in, rg_conv_w, rg_conv_b, rg_wa, rg_ba, rg_wx, rg_bx, rg_lam, s5_a_re, s5_a_im, s5_log_dt, s5_b_re, s5_b_im, s5_c_re, s5_c_im, s5_d, s5_w_glu, s5_b_glu, gnorm_rg, gnorm_s5, w_out, w_router, w1, w3, w2, final_g):
    l = 0
    Bp, Tp, _ = x_prompt.shape
    Bs, Ts, _ = x_sample.shape

    c16 = jnp.concatenate([c, c_ctx[None], jnp.zeros((16 - Bs - 1, D_MODEL), F32)], axis=0)
    mod = _adaln(c16, w_ada[l], b_ada[l]).reshape(16, N_MOD, D_MODEL)
    mod_lat = mod[:Bs]
    mod_ctx1 = mod[Bs:Bs + 1]
    mod_ctx = jnp.broadcast_to(mod_ctx1, (Bp, N_MOD, D_MODEL))

    w_in_b = w_in[l].astype(BF16)
    w_out_b = w_out[l].astype(BF16)
    w_glu_b = s5_w_glu[l].astype(BF16)
    row = lambda v: v.reshape(1, -1)

    wg = jnp.concatenate([_blockdiag_pairs(rg_wa[l]), _blockdiag_pairs(rg_wx[l])], axis=-1).astype(BF16)
    bg = jnp.concatenate([rg_ba[l].reshape(2, RG_SLABS, LANES), rg_bx[l].reshape(2, RG_SLABS, LANES)],
                         axis=-1).transpose(1, 0, 2).reshape(RG_SLABS, 2, 1, 2 * LANES)
    kmat, pmat, qmat, lam16 = _s5prep(s5_a_re[l], s5_a_im[l], s5_log_dt[l], s5_b_re[l], s5_b_im[l],
                                       s5_c_re[l], s5_c_im[l])

    def mixer(x, modp, rg_h0, s5_h0, lat, tag):
        B, T, _ = x.shape
        R = T * B
        proj = _inproj(x, modp, row(norm1_g[l]), w_in_b, "inproj_" + tag)
        y_rg, fin_rg = _rglru(proj, rg_conv_w[l], row(rg_conv_b[l]), wg, bg, rg_lam[l], rg_h0, B, T,
                              "rglru_" + tag)
        if lat:
            A, Bq = T // GRID_W // CHUNK, GRID_W * B
        else:
            A, Bq = T // CHUNK, B
        y_ssm, fin_s5 = _s5(proj.reshape(N_SLAB, A, CHUNK, Bq, LANES), kmat, pmat, qmat, lam16, s5_h0,
                            A, Bq, B, lat, "s5_" + tag)
        x1 = _post(y_rg, y_ssm.reshape(S5_SLABS, R, LANES), proj, x, modp, w_out_b, w_glu_b,
                   row(s5_b_glu[l]), row(s5_d[l]), row(gnorm_rg[l]), row(gnorm_s5[l]), "post_" + tag)
        return x1.reshape(R, D_MODEL), fin_rg, fin_s5

    z_rg = jnp.zeros((2, Bp, D_RG), F32)
    z_s5 = jnp.zeros((S5_GROUPS, Bp, 256), F32)
    x1_c, fin_rg, fin_s5 = mixer(x_prompt, mod_ctx, z_rg, z_s5, False, "ctx")
    rg_h0 = state_rglru[:, l].transpose(1, 0, 2)
    s5_h0 = jnp.stack([state_s5_re[:, l, 0], state_s5_re[:, l, 1], state_s5_im[:, l, 0], state_s5_im[:, l, 1]],
                      axis=0)
    s5_h0 = s5_h0.transpose(2, 1, 0, 3).reshape(S5_GROUPS, Bs, 256)
    x1_s, _, _ = mixer(x_sample, mod_lat, rg_h0, s5_h0, True, "lat")

    w_router_t = w_router[l].T
    g2 = row(norm2_g[l])

    def route(x1, modp, T, tag):
        h2, aff_t = _premoe(x1, modp, g2, w_router_t, T, "premoe_" + tag)
        tri = jnp.triu(jnp.ones((T, T), BF16))
        slot, slot_t = _topk(aff_t, tri, T, "topk_" + tag)
        xg, gates = _gather(h2, slot, aff_t, T, "gather_" + tag)
        return xg, gates, slot_t

    xg_c, gt_c, st_c = route(x1_c, mod_ctx1, Tp, "ctx")
    xg_s, gt_s, st_s = route(x1_s, mod_lat, Ts, "lat")
    y_c, y_s = _ffn(xg_c, xg_s, gt_c, gt_s, w1[l], w3[l], w2[l])
    fg = row(final_g)
    y_prompt = _combine(x1_c, y_c, st_c, mod_ctx1, fg, Tp, "combine_ctx").reshape(Bp, Tp, D_MODEL)
    y_sample = _combine(x1_s, y_s, st_s, mod_lat, fg, Ts, "combine_lat").reshape(Bs, Ts, D_MODEL)

    new_rg = fin_rg.transpose(1, 0, 2)[:, None]
    f4 = fin_s5.reshape(S5_GROUPS, Bp, 4, 64).transpose(1, 2, 0, 3)
    new_re = f4[:, None, 0:2]
    new_im = f4[:, None, 2:4]
    return (y_prompt, y_sample, new_rg, new_re, new_im)
```

```python
import functools

import jax
import jax.numpy as jnp
from jax import lax
from jax.experimental import pallas as pl
from jax.experimental.pallas import tpu as pltpu

F32 = jnp.float32
BF16 = jnp.bfloat16
HIGHEST = lax.Precision.HIGHEST

LANES = 128
D_MODEL = 2048
D_RG = 1536
D_S5 = 512
D_IN = 2 * D_RG + D_S5
N_SLAB = D_IN // LANES
RG_SLABS = D_RG // LANES
S5_SLABS = D_S5 // LANES
S5_GROUPS = 32
GROUPS_PER_SLAB = 8
CHUNK = 16
N_EXPERTS = 16
N_MOD = 6
EPS = 1e-6
RG_C = 8.0
GRID_W = 64
MIB = 1024 * 1024


def _cparams(vmem_mib):
    return pltpu.CompilerParams(vmem_limit_bytes=vmem_mib * MIB)


def _dot(a, b):
    return jnp.dot(a, b, preferred_element_type=F32)


def _rms(x):
    return x * lax.rsqrt(jnp.mean(x * x, axis=-1, keepdims=True) + EPS)


def _adaln_kernel(c_ref, w_ref, b_ref, o_ref):
    s = jax.nn.silu(c_ref[...]).astype(BF16)
    o_ref[...] = _dot(s, w_ref[...].astype(BF16)) + b_ref[...]


def _adaln(c16, w_ada, b_ada):
    n = w_ada.shape[1]
    bn = 1024
    return pl.pallas_call(
        _adaln_kernel, grid=(n // bn,),
        in_specs=[pl.BlockSpec((16, D_MODEL), lambda j: (0, 0)),
                  pl.BlockSpec((D_MODEL, bn), lambda j: (0, j)),
                  pl.BlockSpec((1, bn), lambda j: (0, j))],
        out_specs=pl.BlockSpec((16, bn), lambda j: (0, j)),
        out_shape=jax.ShapeDtypeStruct((16, n), F32),
        compiler_params=_cparams(40), name="adaln")(c16, w_ada, b_ada.reshape(1, n))


def _inproj_kernel(x_ref, mod_ref, g_ref, w_ref, o_ref, *, B, tT, nch):
    x = x_ref[...]
    shift = mod_ref[:, 0:1, :]
    scale = mod_ref[:, 1:2, :]
    h = _rms(x) * g_ref[...] * (1.0 + scale) + shift
    hb = h.reshape(B * tT, D_MODEL).astype(BF16)
    for c0 in range(0, D_IN, nch):
        p = _dot(hb, w_ref[:, c0:c0 + nch])
        for b in range(B):
            for j in range(nch // LANES):
                o_ref[c0 // LANES + j, pl.ds(b, tT, stride=B), :] = (
                    p[b * tT:(b + 1) * tT, j * LANES:(j + 1) * LANES])


def _inproj(x, mod, g1, w_in_bf16, name):
    B, T, _ = x.shape
    tT = 512 // B
    kern = functools.partial(_inproj_kernel, B=B, tT=tT, nch=512)
    return pl.pallas_call(
        kern, grid=(T // tT,),
        in_specs=[pl.BlockSpec((B, tT, D_MODEL), lambda i: (0, i, 0)),
                  pl.BlockSpec((B, N_MOD, D_MODEL), lambda i: (0, 0, 0)),
                  pl.BlockSpec((1, D_MODEL), lambda i: (0, 0)),
                  pl.BlockSpec((D_MODEL, D_IN), lambda i: (0, 0), pipeline_mode=pl.Buffered(1))],
        out_specs=pl.BlockSpec((N_SLAB, tT * B, LANES), lambda i: (0, i, 0)),
        out_shape=jax.ShapeDtypeStruct((N_SLAB, T * B, LANES), F32),
        compiler_params=_cparams(56), name=name)(x, mod, g1, w_in_bf16)


def _rglru_kernel(xb_ref, gt_ref, cw_ref, cb_ref, wg_ref, bg_ref, lam_ref, h0_ref, y_ref, fin_ref,
                  *, B, T, CT):
    RC = CT * B
    nc = T // CT
    cw = [cw_ref[k:k + 1, :] for k in range(4)]
    cb = cb_ref[...]
    nsp = [(-RG_C) * jax.nn.softplus(-lam_ref[d:d + 1, :]) for d in range(2)]

    def chunk(n, d):
        t0 = n * CT
        row0 = pl.multiple_of(t0 * B, RC)
        cur = xb_ref[0, pl.ds(row0, RC), :]
        p0 = pl.multiple_of(jnp.maximum(t0 - 1, 0) * B, B)
        prev = xb_ref[0, pl.ds(p0, B), :] * jnp.where(t0 > 0, 1.0, 0.0)
        n0 = pl.multiple_of(jnp.minimum(t0 + CT, T - 2) * B, B)
        nxt = xb_ref[0, pl.ds(n0, 2 * B), :] * jnp.where(t0 + CT < T, 1.0, 0.0)
        ext = jnp.concatenate([prev, cur, nxt], axis=0)
        xc = (cw[0] * ext[0:RC] + cw[1] * ext[B:B + RC] + cw[2] * ext[2 * B:2 * B + RC]
              + cw[3] * ext[3 * B:3 * B + RC] + cb)
        g = _dot(xc.astype(BF16), wg_ref[0, d]) + bg_ref[0, d]
        th = jnp.tanh(0.5 * g)
        r = 0.5 + 0.5 * th[:, :LANES]
        i = 0.5 + 0.5 * th[:, LANES:]
        a = jnp.exp(r * nsp[d])
        om = 1.0 - a * a
        bb = jnp.where(om > 0.0, om * lax.rsqrt(om), 0.0) * (i * xc)
        return a, bb, row0

    def make_body(finish):
        def body(n, carry):
            hf, hb = carry
            af, bf, rf = chunk(n, 0)
            ab, bk, rb = chunk(nc - 1 - n, 1)
            if finish:
                glf = jax.nn.gelu(gt_ref[0, pl.ds(rf, RC), :])
                glb = jax.nn.gelu(gt_ref[0, pl.ds(rb, RC), :])
                of = y_ref[0, pl.ds(rf, RC), :]
                ob = y_ref[0, pl.ds(rb, RC), :]
            for t in range(CT):
                u = CT - 1 - t
                hf = af[t * B:(t + 1) * B] * hf + bf[t * B:(t + 1) * B]
                hb = ab[u * B:(u + 1) * B] * hb + bk[u * B:(u + 1) * B]
                rows_f = pl.ds(rf + t * B, B)
                rows_b = pl.ds(rb + u * B, B)
                if finish:
                    y_ref[0, rows_f, :] = (of[t * B:(t + 1) * B] + hf) * glf[t * B:(t + 1) * B]
                    y_ref[0, rows_b, :] = (ob[u * B:(u + 1) * B] + hb) * glb[u * B:(u + 1) * B]
                else:
                    y_ref[0, rows_f, :] = hf
                    y_ref[0, rows_b, :] = hb
            return hf, hb
        return body

    carry = lax.fori_loop(0, nc // 2, make_body(False), (h0_ref[0], h0_ref[1]))
    hf, hb = lax.fori_loop(nc // 2, nc, make_body(True), carry)
    fin_ref[0] = hf
    fin_ref[1] = hb


def _rglru(proj, cw, cb, wg, bg, lam, h0, B, T, name):
    R = T * B
    kern = functools.partial(_rglru_kernel, B=B, T=T, CT=256 // B)
    return pl.pallas_call(
        kern, grid=(RG_SLABS,),
        in_specs=[pl.BlockSpec((1, R, LANES), lambda s: (s, 0, 0)),
                  pl.BlockSpec((1, R, LANES), lambda s: (s + RG_SLABS, 0, 0)),
                  pl.BlockSpec((4, LANES), lambda s: (0, s)),
                  pl.BlockSpec((1, LANES), lambda s: (0, s)),
                  pl.BlockSpec((1, 2, LANES, 2 * LANES), lambda s: (s, 0, 0, 0)),
                  pl.BlockSpec((1, 2, 1, 2 * LANES), lambda s: (s, 0, 0, 0)),
                  pl.BlockSpec((2, LANES), lambda s: (0, s)),
                  pl.BlockSpec((2, B, LANES), lambda s: (0, 0, s))],
        out_specs=[pl.BlockSpec((1, R, LANES), lambda s: (s, 0, 0)),
                   pl.BlockSpec((2, B, LANES), lambda s: (0, 0, s))],
        out_shape=[jax.ShapeDtypeStruct((RG_SLABS, R, LANES), F32),
                   jax.ShapeDtypeStruct((2, B, D_RG), F32)],
        compiler_params=_cparams(58), name=name)(proj, proj, cw, cb, wg, bg, lam, h0)


def _s5prep_kernel(ar_ref, ai_ref, ldt_ref, arc_ref, aic_ref, ldtc_ref, btr_ref, bti_ref, ctr_ref, cti_ref,
                   k_ref, p_ref, q_ref, l_ref):
    ar = ar_ref[0]
    ai = ai_ref[0]
    dt = jnp.exp(ldt_ref[0])
    mag = jnp.exp(dt * ar)
    lr = mag * jnp.cos(dt * ai)
    li = mag * jnp.sin(dt * ai)
    den = ar * ar + ai * ai
    nre = lr - 1.0
    qre = (nre * ar + li * ai) / den
    qim = (li * ar - nre * ai) / den
    btr = btr_ref[0]
    bti = bti_ref[0]
    bbr = qre * btr - qim * bti
    bbi = qre * bti + qim * btr
    rowblk = (lax.broadcasted_iota(jnp.int32, (256, LANES), 0) // CHUNK).astype(F32)
    isf = lax.broadcasted_iota(jnp.int32, (256, LANES), 1) < 64

    def powers(n):
        m = jnp.exp(n * (dt * ar))
        ang = n * (dt * ai)
        return m * jnp.cos(ang), m * jnp.sin(ang)

    ctr = ctr_ref[0]
    cti = cti_ref[0]

    wr, wi = powers(rowblk)
    zr = wr * bbr - wi * bbi
    zi = wr * bbi + wi * bbr
    zero = jnp.zeros_like(zr)

    def mm(a, b):
        return jnp.dot(a, b, precision=HIGHEST, preferred_element_type=F32)

    mall_f = mm(jnp.where(isf, zr, zero), ctr) - mm(jnp.where(isf, zi, zero), cti)
    mall_b = mm(jnp.where(isf, zero, zr), ctr) - mm(jnp.where(isf, zero, zi), cti)
    blocks = [mall_f[(15 - q) * CHUNK:(16 - q) * CHUNK] for q in range(15)]
    blocks.append(mall_f[0:CHUNK] + mall_b[0:CHUNK])
    blocks += [mall_b[l * CHUNK:(l + 1) * CHUNK] for l in range(1, 16)]
    blocks.append(jnp.zeros((CHUNK, 256), F32))
    gw = jnp.concatenate(blocks, axis=0)
    colblk = lax.broadcasted_iota(jnp.int32, (256, 256), 1) // CHUNK
    kmat = jnp.zeros((256, 256), F32)
    for tp in range(CHUNK):
        kmat = jnp.where(colblk == tp, gw[(15 - tp) * CHUNK:(15 - tp) * CHUNK + 256], kmat)
    k_ref[0] = kmat.astype(BF16)

    wr, wi = powers(jnp.where(isf, 15.0 - rowblk, rowblk))
    p_ref[0] = jnp.concatenate([wr * bbr - wi * bbi, wr * bbi + wi * bbr], axis=1).astype(BF16)

    arc = arc_ref[0]
    aic = aic_ref[0]
    dtc = jnp.exp(ldtc_ref[0])
    laneblk = (lax.broadcasted_iota(jnp.int32, (LANES, 256), 1) // CHUNK).astype(F32)
    rowf = lax.broadcasted_iota(jnp.int32, (LANES, 256), 0) < 64
    nq = jnp.where(rowf, laneblk + 1.0, 16.0 - laneblk)
    m = jnp.exp(nq * (dtc * arc))
    ang = nq * (dtc * aic)
    wr = m * jnp.cos(ang)
    wi = m * jnp.sin(ang)
    q_ref[0] = jnp.concatenate([ctr * wr - cti * wi, -(ctr * wi + cti * wr)], axis=0).astype(BF16)

    m16 = jnp.exp(16.0 * dt * ar)
    a16 = 16.0 * dt * ai
    l_ref[0] = jnp.concatenate([m16 * jnp.cos(a16), m16 * jnp.sin(a16)], axis=1)


def _s5prep(a_re, a_im, log_dt, b_re, b_im, c_re, c_im):
    G = S5_GROUPS

    def rows(x):
        return jnp.concatenate([x[0], x[1]], axis=-1).reshape(G, 1, LANES)

    ldt = jnp.broadcast_to(log_dt[:, :, None], (2, G, 64))
    ar, ai, ld = rows(a_re), rows(a_im), rows(ldt)
    arc, aic, ldc = (v.reshape(G, LANES, 1) for v in (ar, ai, ld))

    def btile(b):
        bt = jnp.tile(jnp.swapaxes(b, 2, 3), (1, 1, CHUNK, 1))
        return jnp.concatenate([bt[0], bt[1]], axis=-1)

    def ctile(c):
        ct = jnp.tile(jnp.swapaxes(c, 2, 3), (1, 1, 1, CHUNK))
        return jnp.concatenate([ct[0], ct[1]], axis=1)

    vec = pl.BlockSpec((1, 1, LANES), lambda g: (g, 0, 0))
    col = pl.BlockSpec((1, LANES, 1), lambda g: (g, 0, 0))
    bsp = pl.BlockSpec((1, 256, LANES), lambda g: (g, 0, 0))
    csp = pl.BlockSpec((1, LANES, 256), lambda g: (g, 0, 0))
    msp = pl.BlockSpec((1, 256, 256), lambda g: (g, 0, 0))
    return pl.pallas_call(
        _s5prep_kernel, grid=(G,),
        in_specs=[vec, vec, vec, col, col, col, bsp, bsp, csp, csp],
        out_specs=[msp, msp, msp, pl.BlockSpec((1, 1, 256), lambda g: (g, 0, 0))],
        out_shape=[jax.ShapeDtypeStruct((G, 256, 256), BF16)] * 3 + [jax.ShapeDtypeStruct((G, 1, 256), F32)],
        compiler_params=_cparams(32), name="s5prep")(
            ar, ai, ld, arc, aic, ldc, btile(b_re), btile(b_im), ctile(c_re), ctile(c_im))


def _s5_kernel(u_ref, k_ref, p_ref, q_ref, l_ref, h0_ref, y_ref, fin_ref, ebuf, ybuf, *, A, Bq, B, lat):
    RP = A * Bq
    nchunk = RP // B
    G = GROUPS_PER_SLAB
    tiles = [(a * Bq, [a]) for a in range(A)] if lat else [(0, list(range(A)))]
    nr = Bq * len(tiles[0][1])
    lane_blk = lax.broadcasted_iota(jnp.int32, (nr, LANES), 1) // CHUNK

    for r0, alist in tiles:
        xs = [jnp.concatenate([u_ref[0, a, tau] for a in alist], axis=0) for tau in range(CHUNK)]
        for g in range(G):
            halves = []
            for h in range(2):
                acc = None
                for tp in range(8):
                    sh = ((tp - g) * CHUNK) % LANES
                    r = pltpu.roll(xs[8 * h + tp], sh, axis=1) if sh else xs[8 * h + tp]
                    acc = r if acc is None else jnp.where(lane_blk == tp, r, acc)
                halves.append(acc)
            ug = jnp.concatenate(halves, axis=1).astype(BF16)
            ybuf[g, r0:r0 + nr, :] = _dot(ug, k_ref[g])
            ebuf[g, r0:r0 + nr, :] = _dot(ug, p_ref[g])

    def off(m):
        if lat:
            return ((m & 1) * GRID_W + (m >> 1)) * B
        return m * B

    fmask = (lax.broadcasted_iota(jnp.int32, (B, 2 * LANES), 1) % LANES) < 64
    fmask1 = lax.broadcasted_iota(jnp.int32, (B, LANES), 1) < 64
    lam = [l_ref[g] for g in range(G)]

    def body(i, carry):
        of = pl.multiple_of(off(i), B)
        ob = pl.multiple_of(off(nchunk - 1 - i), B)
        out = []
        for g in range(G):
            re, im = carry[2 * g], carry[2 * g + 1]
            ef = ebuf[g, pl.ds(of, B), :]
            eb = ebuf[g, pl.ds(ob, B), :]
            s = jnp.concatenate([re, im], axis=1)
            ebuf[g, pl.ds(of, B), :] = jnp.where(fmask, s, ef)
            ebuf[g, pl.ds(ob, B), :] = jnp.where(fmask, eb, s)
            lr = lam[g][:, :LANES]
            li = lam[g][:, LANES:]
            e_re = jnp.where(fmask1, ef[:, :LANES], eb[:, :LANES])
            e_im = jnp.where(fmask1, ef[:, LANES:], eb[:, LANES:])
            out.append(lr * re - li * im + e_re)
            out.append(lr * im + li * re + e_im)
        return tuple(out)

    init = []
    for g in range(G):
        h0 = h0_ref[g]
        init += [h0[:, :LANES], h0[:, LANES:]]
    fin = lax.fori_loop(0, nchunk, body, tuple(init))
    for g in range(G):
        fin_ref[g] = jnp.concatenate([fin[2 * g], fin[2 * g + 1]], axis=1)

    for g in range(G):
        ybuf[g] = ybuf[g] + _dot(ebuf[g].astype(BF16), q_ref[g])
    for r0, alist in tiles:
        ys = [[ybuf[g, r0:r0 + nr, h * LANES:(h + 1) * LANES] for h in range(2)] for g in range(G)]
        for tau in range(CHUNK):
            h, tp = divmod(tau, 8)
            acc = None
            for g in range(G):
                sh = ((g - tp) * CHUNK) % LANES
                r = pltpu.roll(ys[g][h], sh, axis=1) if sh else ys[g][h]
                acc = r if acc is None else jnp.where(lane_blk == g, r, acc)
            for k, a in enumerate(alist):
                y_ref[0, a, tau] = acc[k * Bq:(k + 1) * Bq]


def _s5(proj5, kmat, pmat, qmat, lam16, h0, A, Bq, B, lat, name):
    RP = A * Bq
    G = GROUPS_PER_SLAB
    kern = functools.partial(_s5_kernel, A=A, Bq=Bq, B=B, lat=lat)
    msp = pl.BlockSpec((G, 256, 256), lambda s: (s, 0, 0))
    return pl.pallas_call(
        kern, grid=(S5_SLABS,),
        in_specs=[pl.BlockSpec((1, A, CHUNK, Bq, LANES), lambda s: (s + 2 * RG_SLABS, 0, 0, 0, 0)),
                  msp, msp, msp,
                  pl.BlockSpec((G, 1, 256), lambda s: (s, 0, 0)),
                  pl.BlockSpec((G, B, 256), lambda s: (s, 0, 0))],
        out_specs=[pl.BlockSpec((1, A, CHUNK, Bq, LANES), lambda s: (s, 0, 0, 0, 0)),
                   pl.BlockSpec((G, B, 256), lambda s: (s, 0, 0))],
        out_shape=[jax.ShapeDtypeStruct((S5_SLABS, A, CHUNK, Bq, LANES), F32),
                   jax.ShapeDtypeStruct((S5_GROUPS, B, 256), F32)],
        scratch_shapes=[pltpu.VMEM((G, RP, 256), F32), pltpu.VMEM((G, RP, 256), F32)],
        compiler_params=_cparams(58), name=name)(proj5, kmat, pmat, qmat, lam16, h0)


def _post_kernel(yr_ref, ys_ref, u_ref, x_ref, mod_ref, wo_ref, wg_ref, bgl_ref, d_ref, grg_ref, gs5_ref,
                 o_ref, ybuf, *, B, tT):
    yr = jnp.concatenate([yr_ref[s] for s in range(RG_SLABS)], axis=1)
    nr = (_rms(yr) * grg_ref[...]).astype(BF16)
    ys = jnp.concatenate([ys_ref[s] for s in range(S5_SLABS)], axis=1)
    u = jnp.concatenate([u_ref[s] for s in range(S5_SLABS)], axis=1)
    ys = jax.nn.gelu(ys + d_ref[...] * u)
    ys = ys * jax.nn.sigmoid(_dot(ys.astype(BF16), wg_ref[...]) + bgl_ref[...])
    ns = (_rms(ys) * gs5_ref[...]).astype(BF16)
    y = _dot(nr, wo_ref[0:D_RG, :]) + _dot(ns, wo_ref[D_RG:D_MODEL, :])
    nj = D_MODEL // LANES
    for j in range(nj):
        ybuf[j] = y[:, j * LANES:(j + 1) * LANES]
    for b in range(B):
        yb = jnp.concatenate([ybuf[j, pl.ds(b, tT, stride=B), :] for j in range(nj)], axis=1)
        o_ref[b] = x_ref[b] + mod_ref[b, 2:3, :] * yb


def _post(y_rg, y_ssm, proj, x, mod, w_out_bf16, w_glu_bf16, b_glu, s5_d, g_rg, g_s5, name):
    B, T, _ = x.shape
    tT = 512 // B
    rows = tT * B
    kern = functools.partial(_post_kernel, B=B, tT=tT)
    const = lambda i: (0, 0)
    return pl.pallas_call(
        kern, grid=(T // tT,),
        in_specs=[pl.BlockSpec((RG_SLABS, rows, LANES), lambda i: (0, i, 0)),
                  pl.BlockSpec((S5_SLABS, rows, LANES), lambda i: (0, i, 0)),
                  pl.BlockSpec((S5_SLABS, rows, LANES), lambda i: (2 * RG_SLABS // S5_SLABS, i, 0)),
                  pl.BlockSpec((B, tT, D_MODEL), lambda i: (0, i, 0)),
                  pl.BlockSpec((B, N_MOD, D_MODEL), lambda i: (0, 0, 0)),
                  pl.BlockSpec((D_MODEL, D_MODEL), const, pipeline_mode=pl.Buffered(1)),
                  pl.BlockSpec((D_S5, D_S5), const),
                  pl.BlockSpec((1, D_S5), const),
                  pl.BlockSpec((1, D_S5), const),
                  pl.BlockSpec((1, D_RG), const),
                  pl.BlockSpec((1, D_S5), const)],
        out_specs=pl.BlockSpec((B, tT, D_MODEL), lambda i: (0, i, 0)),
        out_shape=jax.ShapeDtypeStruct((B, T, D_MODEL), F32),
        scratch_shapes=[pltpu.VMEM((D_MODEL // LANES, rows, LANES), F32)],
        compiler_params=_cparams(56), name=name)(
            y_rg, y_ssm, proj, x, mod, w_out_bf16, w_glu_bf16, b_glu, s5_d, g_rg, g_s5)


def _premoe_kernel(x_ref, mod_ref, g_ref, wr_ref, h_ref, a_ref):
    h = _rms(x_ref[...]) * g_ref[...] * (1.0 + mod_ref[0, 4:5, :]) + mod_ref[0, 3:4, :]
    h_ref[...] = h.astype(BF16)
    lg = lax.dot_general(wr_ref[...], h, (((1,), (1,)), ((), ())), precision=HIGHEST,
                         preferred_element_type=F32)
    e = jnp.exp(lg - jnp.max(lg, axis=0, keepdims=True))
    a_ref[...] = e / jnp.sum(e, axis=0, keepdims=True)


def _premoe(x1, mod, g2, w_router_t, T, name):
    n = x1.shape[0]
    rows = 512
    per_req = T // rows if mod.shape[0] > 1 else n
    return pl.pallas_call(
        _premoe_kernel, grid=(n // rows,),
        in_specs=[pl.BlockSpec((rows, D_MODEL), lambda i: (i, 0)),
                  pl.BlockSpec((1, N_MOD, D_MODEL), lambda i: (i // per_req, 0, 0)),
                  pl.BlockSpec((1, D_MODEL), lambda i: (0, 0)),
                  pl.BlockSpec((N_EXPERTS, D_MODEL), lambda i: (0, 0))],
        out_specs=[pl.BlockSpec((rows, D_MODEL), lambda i: (i, 0)),
                   pl.BlockSpec((N_EXPERTS, rows), lambda i: (0, i))],
        out_shape=[jax.ShapeDtypeStruct((n, D_MODEL), BF16),
                   jax.ShapeDtypeStruct((N_EXPERTS, n), F32)],
        compiler_params=_cparams(40), name=name)(x1, mod, g2, w_router_t)


def _topk_kernel(a_ref, u_ref, s_ref, st_ref, *, cap, T, nreq):
    capf = float(cap)

    def count_ge(r, v):
        return jnp.sum((a_ref[:, r * T:(r + 1) * T] >= v).astype(F32), axis=1, keepdims=True)

    def bit_step(i, thrs):
        bit = jnp.int32(1) << (30 - i)
        out = []
        for r in range(nreq):
            cand = thrs[r] | bit
            ok = count_ge(r, pltpu.bitcast(cand, F32)) >= capf
            out.append(jnp.where(ok, cand, thrs[r]))
        return tuple(out)

    zero = jnp.zeros((N_EXPERTS, 1), jnp.int32)
    thrs = lax.fori_loop(0, 31, bit_step, (zero,) * nreq)

    def mid_step(i, c):
        out = []
        for r in range(nreq):
            lo, hi = c[2 * r], c[2 * r + 1]
            mid = 0.5 * (lo + hi)
            ok = count_ge(r, mid) >= capf
            out += [jnp.where(ok, mid, lo), jnp.where(ok, hi, mid)]
        return tuple(out)

    init = []
    for r in range(nreq):
        init += [pltpu.bitcast(thrs[r], F32), pltpu.bitcast(thrs[r] + 1, F32)]
    bounds = lax.fori_loop(0, 24, mid_step, tuple(init))

    u = u_ref[...]
    for r in range(nreq):
        aff = a_ref[:, r * T:(r + 1) * T]
        lo, hi = bounds[2 * r], bounds[2 * r + 1]
        gt = aff >= hi
        eq = (aff >= lo) & jnp.logical_not(gt)
        need = capf - jnp.sum(gt.astype(F32), axis=1, keepdims=True)
        pe = _dot(eq.astype(F32).astype(BF16), u)
        sel = gt | (eq & (pe <= need))
        ps = _dot(sel.astype(F32).astype(BF16), u)
        slot = jnp.where(sel, ps - 1.0, -1.0)
        s_ref[:, r * T:(r + 1) * T] = slot
        st_ref[r * T:(r + 1) * T, :] = slot.T


def _topk(aff_t, tri, T, name):
    n = aff_t.shape[1]
    kern = functools.partial(_topk_kernel, cap=T // 8, T=T, nreq=n // T)
    return pl.pallas_call(
        kern, grid=(1,),
        in_specs=[pl.BlockSpec((N_EXPERTS, n), lambda i: (0, 0)),
                  pl.BlockSpec((T, T), lambda i: (0, 0))],
        out_specs=[pl.BlockSpec((N_EXPERTS, n), lambda i: (0, 0)),
                   pl.BlockSpec((n, N_EXPERTS), lambda i: (0, 0))],
        out_shape=[jax.ShapeDtypeStruct((N_EXPERTS, n), F32),
                   jax.ShapeDtypeStruct((n, N_EXPERTS), F32)],
        compiler_params=_cparams(48), name=name)(aff_t, tri)


def _gather_kernel(h_ref, s_ref, a_ref, xg_ref, gt_ref, *, cap, T, eb):
    e0 = pl.program_id(1) * eb
    j = lax.broadcasted_iota(jnp.int32, (cap, T), 0).astype(F32)
    h = h_ref[...]
    for k in range(eb):
        srow = s_ref[pl.ds(e0 + k, 1), :]
        arow = a_ref[pl.ds(e0 + k, 1), :]
        oh = srow == j
        xg_ref[k] = _dot(oh.astype(F32).astype(BF16), h).astype(BF16)
        gt_ref[k] = jnp.sum(jnp.where(oh, arow, 0.0), axis=1, keepdims=True)


def _gather(h2, slot, aff_t, T, name):
    n = h2.shape[0]
    cap = T // 8
    nreq = n // T
    eb = max(1, min(N_EXPERTS, 512 // cap))
    kern = functools.partial(_gather_kernel, cap=cap, T=T, eb=eb)
    return pl.pallas_call(
        kern, grid=(nreq, N_EXPERTS // eb),
        in_specs=[pl.BlockSpec((T, D_MODEL), lambda b, e: (b, 0)),
                  pl.BlockSpec((N_EXPERTS, T), lambda b, e: (0, b)),
                  pl.BlockSpec((N_EXPERTS, T), lambda b, e: (0, b))],
        out_specs=[pl.BlockSpec((eb, cap, D_MODEL), lambda b, e: (e, b, 0)),
                   pl.BlockSpec((eb, cap, 1), lambda b, e: (e, b, 0))],
        out_shape=[jax.ShapeDtypeStruct((N_EXPERTS, nreq * cap, D_MODEL), BF16),
                   jax.ShapeDtypeStruct((N_EXPERTS, nreq * cap, 1), F32)],
        compiler_params=_cparams(48), name=name)(h2, slot, aff_t)


def _ffn_kernel(xc_ref, xl_ref, gc_ref, gl_ref, w1_ref, w3_ref, w2_ref, yc_ref, yl_ref, xbuf, acc, *, rc, nf, rsub):
    f = pl.program_id(2)

    @pl.when(f == 0)
    def _():
        xbuf[0:rc, :] = xc_ref[0]
        xbuf[rc:, :] = xl_ref[0]

    @pl.when(f == 0)
    def _():
        acc[...] = jnp.zeros_like(acc)

    w1 = w1_ref[0].astype(BF16)
    w3 = w3_ref[0].astype(BF16)
    w2 = w2_ref[0].astype(BF16)
    rows = xbuf.shape[0]
    for r0 in range(0, rows, rsub):
        x = xbuf[r0:r0 + rsub, :]
        hid = jax.nn.silu(_dot(x, w1)) * _dot(x, w3)
        acc[r0:r0 + rsub, :] += _dot(hid.astype(BF16), w2)

    @pl.when(f == nf - 1)
    def _():
        yc_ref[0] = (acc[0:rc, :] * gc_ref[0]).astype(BF16)
        yl_ref[0] = (acc[rc:, :] * gl_ref[0]).astype(BF16)


def _ffn(xg_c, xg_l, gt_c, gt_l, w1, w3, w2):
    rc = xg_c.shape[1] // 2
    rl = xg_l.shape[1] // 2
    bf = 256
    nf = D_MODEL // bf
    kern = functools.partial(_ffn_kernel, rc=rc, nf=nf, rsub=(rc + rl) // 2)
    rowsp = lambda r, w: pl.BlockSpec((1, r, w), lambda e, h, f: (e, h, 0))
    return pl.pallas_call(
        kern, grid=(N_EXPERTS, 2, nf),
        in_specs=[rowsp(rc, D_MODEL), rowsp(rl, D_MODEL), rowsp(rc, 1), rowsp(rl, 1),
                  pl.BlockSpec((1, D_MODEL, bf), lambda e, h, f: (e, 0, f)),
                  pl.BlockSpec((1, D_MODEL, bf), lambda e, h, f: (e, 0, f)),
                  pl.BlockSpec((1, bf, D_MODEL), lambda e, h, f: (e, f, 0))],
        out_specs=[rowsp(rc, D_MODEL), rowsp(rl, D_MODEL)],
        out_shape=[jax.ShapeDtypeStruct(xg_c.shape, BF16), jax.ShapeDtypeStruct(xg_l.shape, BF16)],
        scratch_shapes=[pltpu.VMEM((rc + rl, D_MODEL), BF16), pltpu.VMEM((rc + rl, D_MODEL), F32)],
        compiler_params=_cparams(56), name="ffn")(xg_c, xg_l, gt_c, gt_l, w1, w3, w2)


def _combine_kernel(x_ref, y_ref, st_ref, mod_ref, fg_ref, o_ref, *, cap, TB):
    K = N_EXPERTS * cap
    st = st_ref[...]
    expand = (lax.broadcasted_iota(jnp.int32, (N_EXPERTS, K), 0)
              == lax.broadcasted_iota(jnp.int32, (N_EXPERTS, K), 1) // cap)
    rep = _dot(st.astype(BF16), expand.astype(F32).astype(BF16))
    jl = (lax.broadcasted_iota(jnp.int32, (TB, K), 1) % cap).astype(F32)
    oh = (rep == jl).astype(F32).astype(BF16)
    moe = _dot(oh, y_ref[...].reshape(K, D_MODEL))
    x2 = x_ref[...] + mod_ref[0, 5:6, :] * moe
    o_ref[...] = _rms(x2) * fg_ref[...]


def _combine(x1, ybuf, slot_t, mod, final_g, T, name):
    n = x1.shape[0]
    cap = T // 8
    TB = 256
    nb = T // TB
    multi = mod.shape[0] > 1
    kern = functools.partial(_combine_kernel, cap=cap, TB=TB)
    return pl.pallas_call(
        kern, grid=(n // T, nb),
        in_specs=[pl.BlockSpec((TB, D_MODEL), lambda b, k: (b * nb + k, 0)),
                  pl.BlockSpec((N_EXPERTS, cap, D_MODEL), lambda b, k: (0, b, 0)),
                  pl.BlockSpec((TB, N_EXPERTS), lambda b, k: (b * nb + k, 0)),
                  pl.BlockSpec((1, N_MOD, D_MODEL), lambda b, k: (b if multi else 0, 0, 0)),
                  pl.BlockSpec((1, D_MODEL), lambda b, k: (0, 0))],
        out_specs=pl.BlockSpec((TB, D_MODEL), lambda b, k: (b * nb + k, 0)),
        out_shape=jax.ShapeDtypeStruct((n, D_MODEL), F32),
        compiler_params=_cparams(56), name=name)(x1, ybuf, slot_t, mod, final_g)


def _blockdiag_pairs(w):
    w = w.reshape(2, RG_SLABS, 2, 64, 64)
    bd = jnp.einsum('dshij,hk->dshikj', w, jnp.eye(2, dtype=w.dtype))
    return bd.reshape(2, RG_SLABS, LANES, LANES).transpose(1, 0, 2, 3)


def kernel(x_prompt, x_sample, state_rglru, state_s5_re, state_s5_im, c, c_ctx, norm1_g, norm2_g, w_ada, b_ada, w_in, rg_conv_w, rg_conv_b, rg_wa, rg_ba, rg_wx, rg_bx, rg_lam, s5_a_re, s5_a_im, s5_log_dt, s5_b_re, s5_b_im, s5_c_re, s5_c_im, s5_d, s5_w_glu, s5_b_glu, gnorm_rg, gnorm_s5, w_out, w_router, w1, w3, w2, final_g):
    l = 0
    Bp, Tp, _ = x_prompt.shape
    Bs, Ts, _ = x_sample.shape

    c16 = jnp.concatenate([c, c_ctx[None], jnp.zeros((16 - Bs - 1, D_MODEL), F32)], axis=0)
    mod = _adaln(c16, w_ada[l], b_ada[l]).reshape(16, N_MOD, D_MODEL)
    mod_lat = mod[:Bs]
    mod_ctx1 = mod[Bs:Bs + 1]
    mod_ctx = jnp.broadcast_to(mod_ctx1, (Bp, N_MOD, D_MODEL))

    w_in_b = w_in[l].astype(BF16)
    w_out_b = w_out[l].astype(BF16)
    w_glu_b = s5_w_glu[l].astype(BF16)
    row = lambda v: v.reshape(1, -1)

    wg = jnp.concatenate([_blockdiag_pairs(rg_wa[l]), _blockdiag_pairs(rg_wx[l])], axis=-1).astype(BF16)
    bg = jnp.concatenate([rg_ba[l].reshape(2, RG_SLABS, LANES), rg_bx[l].reshape(2, RG_SLABS, LANES)],
                         axis=-1).transpose(1, 0, 2).reshape(RG_SLABS, 2, 1, 2 * LANES)
    kmat, pmat, qmat, lam16 = _s5prep(s5_a_re[l], s5_a_im[l], s5_log_dt[l], s5_b_re[l], s5_b_im[l],
                                       s5_c_re[l], s5_c_im[l])

    def mixer(x, modp, rg_h0, s5_h0, lat, tag):
        B, T, _ = x.shape
        R = T * B
        proj = _inproj(x, modp, row(norm1_g[l]), w_in_b, "inproj_" + tag)
        y_rg, fin_rg = _rglru(proj, rg_conv_w[l], row(rg_conv_b[l]), wg, bg, rg_lam[l], rg_h0, B, T,
                              "rglru_" + tag)
        if lat:
            A, Bq = T // GRID_W // CHUNK, GRID_W * B
        else:
            A, Bq = T // CHUNK, B
        y_ssm, fin_s5 = _s5(proj.reshape(N_SLAB, A, CHUNK, Bq, LANES), kmat, pmat, qmat, lam16, s5_h0,
                            A, Bq, B, lat, "s5_" + tag)
        x1 = _post(y_rg, y_ssm.reshape(S5_SLABS, R, LANES), proj, x, modp, w_out_b, w_glu_b,
                   row(s5_b_glu[l]), row(s5_d[l]), row(gnorm_rg[l]), row(gnorm_s5[l]), "post_" + tag)
        return x1.reshape(R, D_MODEL), fin_rg, fin_s5

    z_rg = jnp.zeros((2, Bp, D_RG), F32)
    z_s5 = jnp.zeros((S5_GROUPS, Bp, 256), F32)
    x1_c, fin_rg, fin_s5 = mixer(x_prompt, mod_ctx, z_rg, z_s5, False, "ctx")
    rg_h0 = state_rglru[:, l].transpose(1, 0, 2)
    s5_h0 = jnp.stack([state_s5_re[:, l, 0], state_s5_re[:, l, 1], state_s5_im[:, l, 0], state_s5_im[:, l, 1]],
                      axis=0)
    s5_h0 = s5_h0.transpose(2, 1, 0, 3).reshape(S5_GROUPS, Bs, 256)
    x1_s, _, _ = mixer(x_sample, mod_lat, rg_h0, s5_h0, True, "lat")

    w_router_t = w_router[l].T
    g2 = row(norm2_g[l])

    def route(x1, modp, T, tag):
        h2, aff_t = _premoe(x1, modp, g2, w_router_t, T, "premoe_" + tag)
        tri = jnp.triu(jnp.ones((T, T), BF16))
        slot, slot_t = _topk(aff_t, tri, T, "topk_" + tag)
        xg, gates = _gather(h2, slot, aff_t, T, "gather_" + tag)
        return xg, gates, slot_t

    xg_c, gt_c, st_c = route(x1_c, mod_ctx1, Tp, "ctx")
    xg_s, gt_s, st_s = route(x1_s, mod_lat, Ts, "lat")
    y_c, y_s = _ffn(xg_c, xg_s, gt_c, gt_s, w1[l], w3[l], w2[l])
    fg = row(final_g)
    y_prompt = _combine(x1_c, y_c, st_c, mod_ctx1, fg, Tp, "combine_ctx").reshape(Bp, Tp, D_MODEL)
    y_sample = _combine(x1_s, y_s, st_s, mod_lat, fg, Ts, "combine_lat").reshape(Bs, Ts, D_MODEL)

    new_rg = fin_rg.transpose(1, 0, 2)[:, None]
    f4 = fin_s5.reshape(S5_GROUPS, Bp, 4, 64).transpose(1, 2, 0, 3)
    new_re = f4[:, None, 0:2]
    new_im = f4[:, None, 2:4]
    return (y_prompt, y_sample, new_rg, new_re, new_im)
```

```python
import functools

import jax
import jax.numpy as jnp
from jax import lax
from jax.experimental import pallas as pl
from jax.experimental.pallas import tpu as pltpu

F32 = jnp.float32
BF16 = jnp.bfloat16
HIGHEST = lax.Precision.HIGHEST

LANES = 128
D_MODEL = 2048
D_RG = 1536
D_S5 = 512
D_IN = 2 * D_RG + D_S5
N_SLAB = D_IN // LANES
RG_SLABS = D_RG // LANES
S5_SLABS = D_S5 // LANES
S5_GROUPS = 32
GROUPS_PER_SLAB = 8
CHUNK = 16
N_EXPERTS = 16
N_MOD = 6
EPS = 1e-6
RG_C = 8.0
GRID_W = 64
TOKEN_BLOCK = 256
SLOT_WINDOW = 80
MIB = 1024 * 1024


def _cparams(vmem_mib):
    return pltpu.CompilerParams(vmem_limit_bytes=vmem_mib * MIB)


def _dot(a, b):
    return jnp.dot(a, b, preferred_element_type=F32)


def _rms(x):
    return x * lax.rsqrt(jnp.mean(x * x, axis=-1, keepdims=True) + EPS)


def _adaln_kernel(c_ref, w_ref, b_ref, o_ref):
    s = jax.nn.silu(c_ref[...]).astype(BF16)
    o_ref[...] = _dot(s, w_ref[...].astype(BF16)) + b_ref[...]


def _adaln(c16, w_ada, b_ada):
    n = w_ada.shape[1]
    bn = 1024
    return pl.pallas_call(
        _adaln_kernel, grid=(n // bn,),
        in_specs=[pl.BlockSpec((16, D_MODEL), lambda j: (0, 0)),
                  pl.BlockSpec((D_MODEL, bn), lambda j: (0, j)),
                  pl.BlockSpec((1, bn), lambda j: (0, j))],
        out_specs=pl.BlockSpec((16, bn), lambda j: (0, j)),
        out_shape=jax.ShapeDtypeStruct((16, n), F32),
        compiler_params=_cparams(40), name="adaln")(c16, w_ada, b_ada.reshape(1, n))


def _inproj_kernel(x_ref, mod_ref, g_ref, w_ref, o_ref, *, B, tT, nch):
    x = x_ref[...]
    shift = mod_ref[:, 0:1, :]
    scale = mod_ref[:, 1:2, :]
    h = _rms(x) * g_ref[...] * (1.0 + scale) + shift
    hb = h.reshape(B * tT, D_MODEL).astype(BF16)
    for c0 in range(0, D_IN, nch):
        p = _dot(hb, w_ref[:, c0:c0 + nch])
        for b in range(B):
            for j in range(nch // LANES):
                o_ref[c0 // LANES + j, pl.ds(b, tT, stride=B), :] = (
                    p[b * tT:(b + 1) * tT, j * LANES:(j + 1) * LANES])


def _inproj(x, mod, g1, w_in_bf16, name):
    B, T, _ = x.shape
    tT = 512 // B
    kern = functools.partial(_inproj_kernel, B=B, tT=tT, nch=512)
    return pl.pallas_call(
        kern, grid=(T // tT,),
        in_specs=[pl.BlockSpec((B, tT, D_MODEL), lambda i: (0, i, 0)),
                  pl.BlockSpec((B, N_MOD, D_MODEL), lambda i: (0, 0, 0)),
                  pl.BlockSpec((1, D_MODEL), lambda i: (0, 0)),
                  pl.BlockSpec((D_MODEL, D_IN), lambda i: (0, 0), pipeline_mode=pl.Buffered(1))],
        out_specs=pl.BlockSpec((N_SLAB, tT * B, LANES), lambda i: (0, i, 0)),
        out_shape=jax.ShapeDtypeStruct((N_SLAB, T * B, LANES), F32),
        compiler_params=_cparams(56), name=name)(x, mod, g1, w_in_bf16)


def _rglru_kernel(xb_ref, gt_ref, cw_ref, cb_ref, wg_ref, bg_ref, lam_ref, h0_ref, y_ref, fin_ref,
                  *, B, T, CT):
    RC = CT * B
    nc = T // CT
    cw = [cw_ref[k:k + 1, :] for k in range(4)]
    cb = cb_ref[...]
    nsp = [(-RG_C) * jax.nn.softplus(-lam_ref[d:d + 1, :]) for d in range(2)]

    def chunk(n, d):
        t0 = n * CT
        row0 = pl.multiple_of(t0 * B, RC)
        cur = xb_ref[0, pl.ds(row0, RC), :]
        p0 = pl.multiple_of(jnp.maximum(t0 - 1, 0) * B, B)
        prev = xb_ref[0, pl.ds(p0, B), :] * jnp.where(t0 > 0, 1.0, 0.0)
        n0 = pl.multiple_of(jnp.minimum(t0 + CT, T - 2) * B, B)
        nxt = xb_ref[0, pl.ds(n0, 2 * B), :] * jnp.where(t0 + CT < T, 1.0, 0.0)
        ext = jnp.concatenate([prev, cur, nxt], axis=0)
        xc = (cw[0] * ext[0:RC] + cw[1] * ext[B:B + RC] + cw[2] * ext[2 * B:2 * B + RC]
              + cw[3] * ext[3 * B:3 * B + RC] + cb)
        g = _dot(xc.astype(BF16), wg_ref[0, d]) + bg_ref[0, d]
        th = jnp.tanh(0.5 * g)
        r = 0.5 + 0.5 * th[:, :LANES]
        i = 0.5 + 0.5 * th[:, LANES:]
        a = jnp.exp(r * nsp[d])
        om = 1.0 - a * a
        bb = jnp.where(om > 0.0, om * lax.rsqrt(om), 0.0) * (i * xc)
        return a, bb, row0

    def make_body(finish):
        def body(n, carry):
            hf, hb = carry
            af, bf, rf = chunk(n, 0)
            ab, bk, rb = chunk(nc - 1 - n, 1)
            if finish:
                glf = jax.nn.gelu(gt_ref[0, pl.ds(rf, RC), :])
                glb = jax.nn.gelu(gt_ref[0, pl.ds(rb, RC), :])
                of = y_ref[0, pl.ds(rf, RC), :]
                ob = y_ref[0, pl.ds(rb, RC), :]
            for t in range(CT):
                u = CT - 1 - t
                hf = af[t * B:(t + 1) * B] * hf + bf[t * B:(t + 1) * B]
                hb = ab[u * B:(u + 1) * B] * hb + bk[u * B:(u + 1) * B]
                rows_f = pl.ds(rf + t * B, B)
                rows_b = pl.ds(rb + u * B, B)
                if finish:
                    y_ref[0, rows_f, :] = (of[t * B:(t + 1) * B] + hf) * glf[t * B:(t + 1) * B]
                    y_ref[0, rows_b, :] = (ob[u * B:(u + 1) * B] + hb) * glb[u * B:(u + 1) * B]
                else:
                    y_ref[0, rows_f, :] = hf
                    y_ref[0, rows_b, :] = hb
            return hf, hb
        return body

    carry = lax.fori_loop(0, nc // 2, make_body(False), (h0_ref[0], h0_ref[1]))
    hf, hb = lax.fori_loop(nc // 2, nc, make_body(True), carry)
    fin_ref[0] = hf
    fin_ref[1] = hb


def _rglru(proj, cw, cb, wg, bg, lam, h0, B, T, name):
    R = T * B
    kern = functools.partial(_rglru_kernel, B=B, T=T, CT=256 // B)
    return pl.pallas_call(
        kern, grid=(RG_SLABS,),
        in_specs=[pl.BlockSpec((1, R, LANES), lambda s: (s, 0, 0)),
                  pl.BlockSpec((1, R, LANES), lambda s: (s + RG_SLABS, 0, 0)),
                  pl.BlockSpec((4, LANES), lambda s: (0, s)),
                  pl.BlockSpec((1, LANES), lambda s: (0, s)),
                  pl.BlockSpec((1, 2, LANES, 2 * LANES), lambda s: (s, 0, 0, 0)),
                  pl.BlockSpec((1, 2, 1, 2 * LANES), lambda s: (s, 0, 0, 0)),
                  pl.BlockSpec((2, LANES), lambda s: (0, s)),
                  pl.BlockSpec((2, B, LANES), lambda s: (0, 0, s))],
        out_specs=[pl.BlockSpec((1, R, LANES), lambda s: (s, 0, 0)),
                   pl.BlockSpec((2, B, LANES), lambda s: (0, 0, s))],
        out_shape=[jax.ShapeDtypeStruct((RG_SLABS, R, LANES), F32),
                   jax.ShapeDtypeStruct((2, B, D_RG), F32)],
        compiler_params=_cparams(58), name=name)(proj, proj, cw, cb, wg, bg, lam, h0)


def _s5prep_kernel(ar_ref, ai_ref, ldt_ref, arc_ref, aic_ref, ldtc_ref, btr_ref, bti_ref, ctr_ref, cti_ref,
                   k_ref, p_ref, q_ref, l_ref):
    ar = ar_ref[0]
    ai = ai_ref[0]
    dt = jnp.exp(ldt_ref[0])
    mag = jnp.exp(dt * ar)
    lr = mag * jnp.cos(dt * ai)
    li = mag * jnp.sin(dt * ai)
    den = ar * ar + ai * ai
    nre = lr - 1.0
    qre = (nre * ar + li * ai) / den
    qim = (li * ar - nre * ai) / den
    btr = btr_ref[0]
    bti = bti_ref[0]
    bbr = qre * btr - qim * bti
    bbi = qre * bti + qim * btr
    rowblk = (lax.broadcasted_iota(jnp.int32, (256, LANES), 0) // CHUNK).astype(F32)
    isf = lax.broadcasted_iota(jnp.int32, (256, LANES), 1) < 64

    def powers(n):
        m = jnp.exp(n * (dt * ar))
        ang = n * (dt * ai)
        return m * jnp.cos(ang), m * jnp.sin(ang)

    ctr = ctr_ref[0]
    cti = cti_ref[0]

    wr, wi = powers(rowblk)
    zr = wr * bbr - wi * bbi
    zi = wr * bbi + wi * bbr
    zero = jnp.zeros_like(zr)

    def mm(a, b):
        return jnp.dot(a, b, precision=HIGHEST, preferred_element_type=F32)

    mall_f = mm(jnp.where(isf, zr, zero), ctr) - mm(jnp.where(isf, zi, zero), cti)
    mall_b = mm(jnp.where(isf, zero, zr), ctr) - mm(jnp.where(isf, zero, zi), cti)
    blocks = [mall_f[(15 - q) * CHUNK:(16 - q) * CHUNK] for q in range(15)]
    blocks.append(mall_f[0:CHUNK] + mall_b[0:CHUNK])
    blocks += [mall_b[l * CHUNK:(l + 1) * CHUNK] for l in range(1, 16)]
    blocks.append(jnp.zeros((CHUNK, 256), F32))
    gw = jnp.concatenate(blocks, axis=0)
    colblk = lax.broadcasted_iota(jnp.int32, (256, 256), 1) // CHUNK
    kmat = jnp.zeros((256, 256), F32)
    for tp in range(CHUNK):
        kmat = jnp.where(colblk == tp, gw[(15 - tp) * CHUNK:(15 - tp) * CHUNK + 256], kmat)
    k_ref[0] = kmat.astype(BF16)

    wr, wi = powers(jnp.where(isf, 15.0 - rowblk, rowblk))
    p_ref[0] = jnp.concatenate([wr * bbr - wi * bbi, wr * bbi + wi * bbr], axis=1).astype(BF16)

    arc = arc_ref[0]
    aic = aic_ref[0]
    dtc = jnp.exp(ldtc_ref[0])
    laneblk = (lax.broadcasted_iota(jnp.int32, (LANES, 256), 1) // CHUNK).astype(F32)
    rowf = lax.broadcasted_iota(jnp.int32, (LANES, 256), 0) < 64
    nq = jnp.where(rowf, laneblk + 1.0, 16.0 - laneblk)
    m = jnp.exp(nq * (dtc * arc))
    ang = nq * (dtc * aic)
    wr = m * jnp.cos(ang)
    wi = m * jnp.sin(ang)
    q_ref[0] = jnp.concatenate([ctr * wr - cti * wi, -(ctr * wi + cti * wr)], axis=0).astype(BF16)

    m16 = jnp.exp(16.0 * dt * ar)
    a16 = 16.0 * dt * ai
    l_ref[0] = jnp.concatenate([m16 * jnp.cos(a16), m16 * jnp.sin(a16)], axis=1)


def _s5prep(a_re, a_im, log_dt, b_re, b_im, c_re, c_im):
    G = S5_GROUPS

    def rows(x):
        return jnp.concatenate([x[0], x[1]], axis=-1).reshape(G, 1, LANES)

    ldt = jnp.broadcast_to(log_dt[:, :, None], (2, G, 64))
    ar, ai, ld = rows(a_re), rows(a_im), rows(ldt)
    arc, aic, ldc = (v.reshape(G, LANES, 1) for v in (ar, ai, ld))

    def btile(b):
        bt = jnp.tile(jnp.swapaxes(b, 2, 3), (1, 1, CHUNK, 1))
        return jnp.concatenate([bt[0], bt[1]], axis=-1)

    def ctile(c):
        ct = jnp.tile(jnp.swapaxes(c, 2, 3), (1, 1, 1, CHUNK))
        return jnp.concatenate([ct[0], ct[1]], axis=1)

    vec = pl.BlockSpec((1, 1, LANES), lambda g: (g, 0, 0))
    col = pl.BlockSpec((1, LANES, 1), lambda g: (g, 0, 0))
    bsp = pl.BlockSpec((1, 256, LANES), lambda g: (g, 0, 0))
    csp = pl.BlockSpec((1, LANES, 256), lambda g: (g, 0, 0))
    msp = pl.BlockSpec((1, 256, 256), lambda g: (g, 0, 0))
    return pl.pallas_call(
        _s5prep_kernel, grid=(G,),
        in_specs=[vec, vec, vec, col, col, col, bsp, bsp, csp, csp],
        out_specs=[msp, msp, msp, pl.BlockSpec((1, 1, 256), lambda g: (g, 0, 0))],
        out_shape=[jax.ShapeDtypeStruct((G, 256, 256), BF16)] * 3 + [jax.ShapeDtypeStruct((G, 1, 256), F32)],
        compiler_params=_cparams(32), name="s5prep")(
            ar, ai, ld, arc, aic, ldc, btile(b_re), btile(b_im), ctile(c_re), ctile(c_im))


def _s5_kernel(u_ref, k_ref, p_ref, q_ref, l_ref, h0_ref, y_ref, fin_ref, ebuf, ybuf, *, A, Bq, B, lat):
    RP = A * Bq
    nchunk = RP // B
    G = GROUPS_PER_SLAB
    tiles = [(a * Bq, [a]) for a in range(A)] if lat else [(0, list(range(A)))]
    nr = Bq * len(tiles[0][1])
    lane_blk = lax.broadcasted_iota(jnp.int32, (nr, LANES), 1) // CHUNK

    for r0, alist in tiles:
        xs = [jnp.concatenate([u_ref[0, a, tau] for a in alist], axis=0) for tau in range(CHUNK)]
        for g in range(G):
            halves = []
            for h in range(2):
                acc = None
                for tp in range(8):
                    sh = ((tp - g) * CHUNK) % LANES
                    r = pltpu.roll(xs[8 * h + tp], sh, axis=1) if sh else xs[8 * h + tp]
                    acc = r if acc is None else jnp.where(lane_blk == tp, r, acc)
                halves.append(acc)
            ug = jnp.concatenate(halves, axis=1).astype(BF16)
            ybuf[g, r0:r0 + nr, :] = _dot(ug, k_ref[g])
            ebuf[g, r0:r0 + nr, :] = _dot(ug, p_ref[g])

    def off(m):
        if lat:
            return ((m & 1) * GRID_W + (m >> 1)) * B
        return m * B

    fmask = (lax.broadcasted_iota(jnp.int32, (B, 2 * LANES), 1) % LANES) < 64
    fmask1 = lax.broadcasted_iota(jnp.int32, (B, LANES), 1) < 64
    lam = [l_ref[g] for g in range(G)]

    def body(i, carry):
        of = pl.multiple_of(off(i), B)
        ob = pl.multiple_of(off(nchunk - 1 - i), B)
        out = []
        for g in range(G):
            re, im = carry[2 * g], carry[2 * g + 1]
            ef = ebuf[g, pl.ds(of, B), :]
            eb = ebuf[g, pl.ds(ob, B), :]
            s = jnp.concatenate([re, im], axis=1)
            ebuf[g, pl.ds(of, B), :] = jnp.where(fmask, s, ef)
            ebuf[g, pl.ds(ob, B), :] = jnp.where(fmask, eb, s)
            lr = lam[g][:, :LANES]
            li = lam[g][:, LANES:]
            e_re = jnp.where(fmask1, ef[:, :LANES], eb[:, :LANES])
            e_im = jnp.where(fmask1, ef[:, LANES:], eb[:, LANES:])
            out.append(lr * re - li * im + e_re)
            out.append(lr * im + li * re + e_im)
        return tuple(out)

    init = []
    for g in range(G):
        h0 = h0_ref[g]
        init += [h0[:, :LANES], h0[:, LANES:]]
    fin = lax.fori_loop(0, nchunk, body, tuple(init))
    for g in range(G):
        fin_ref[g] = jnp.concatenate([fin[2 * g], fin[2 * g + 1]], axis=1)

    for g in range(G):
        ybuf[g] = ybuf[g] + _dot(ebuf[g].astype(BF16), q_ref[g])
    for r0, alist in tiles:
        ys = [[ybuf[g, r0:r0 + nr, h * LANES:(h + 1) * LANES] for h in range(2)] for g in range(G)]
        for tau in range(CHUNK):
            h, tp = divmod(tau, 8)
            acc = None
            for g in range(G):
                sh = ((g - tp) * CHUNK) % LANES
                r = pltpu.roll(ys[g][h], sh, axis=1) if sh else ys[g][h]
                acc = r if acc is None else jnp.where(lane_blk == g, r, acc)
            for k, a in enumerate(alist):
                y_ref[0, a, tau] = acc[k * Bq:(k + 1) * Bq]


def _s5(proj5, kmat, pmat, qmat, lam16, h0, A, Bq, B, lat, name):
    RP = A * Bq
    G = GROUPS_PER_SLAB
    kern = functools.partial(_s5_kernel, A=A, Bq=Bq, B=B, lat=lat)
    msp = pl.BlockSpec((G, 256, 256), lambda s: (s, 0, 0))
    return pl.pallas_call(
        kern, grid=(S5_SLABS,),
        in_specs=[pl.BlockSpec((1, A, CHUNK, Bq, LANES), lambda s: (s + 2 * RG_SLABS, 0, 0, 0, 0)),
                  msp, msp, msp,
                  pl.BlockSpec((G, 1, 256), lambda s: (s, 0, 0)),
                  pl.BlockSpec((G, B, 256), lambda s: (s, 0, 0))],
        out_specs=[pl.BlockSpec((1, A, CHUNK, Bq, LANES), lambda s: (s, 0, 0, 0, 0)),
                   pl.BlockSpec((G, B, 256), lambda s: (s, 0, 0))],
        out_shape=[jax.ShapeDtypeStruct((S5_SLABS, A, CHUNK, Bq, LANES), F32),
                   jax.ShapeDtypeStruct((S5_GROUPS, B, 256), F32)],
        scratch_shapes=[pltpu.VMEM((G, RP, 256), F32), pltpu.VMEM((G, RP, 256), F32)],
        compiler_params=_cparams(58), name=name)(proj5, kmat, pmat, qmat, lam16, h0)


def _post_kernel(yr_ref, ys_ref, u_ref, x_ref, mod_ref, wo_ref, wg_ref, bgl_ref, d_ref, grg_ref, gs5_ref,
                 o_ref, ybuf, *, B, tT):
    yr = jnp.concatenate([yr_ref[s] for s in range(RG_SLABS)], axis=1)
    nr = (_rms(yr) * grg_ref[...]).astype(BF16)
    ys = jnp.concatenate([ys_ref[s] for s in range(S5_SLABS)], axis=1)
    u = jnp.concatenate([u_ref[s] for s in range(S5_SLABS)], axis=1)
    ys = jax.nn.gelu(ys + d_ref[...] * u)
    ys = ys * jax.nn.sigmoid(_dot(ys.astype(BF16), wg_ref[...]) + bgl_ref[...])
    ns = (_rms(ys) * gs5_ref[...]).astype(BF16)
    y = _dot(nr, wo_ref[0:D_RG, :]) + _dot(ns, wo_ref[D_RG:D_MODEL, :])
    nj = D_MODEL // LANES
    for j in range(nj):
        ybuf[j] = y[:, j * LANES:(j + 1) * LANES]
    for b in range(B):
        yb = jnp.concatenate([ybuf[j, pl.ds(b, tT, stride=B), :] for j in range(nj)], axis=1)
        o_ref[b] = x_ref[b] + mod_ref[b, 2:3, :] * yb


def _post(y_rg, y_ssm, proj, x, mod, w_out_bf16, w_glu_bf16, b_glu, s5_d, g_rg, g_s5, name):
    B, T, _ = x.shape
    tT = 512 // B
    rows = tT * B
    kern = functools.partial(_post_kernel, B=B, tT=tT)
    const = lambda i: (0, 0)
    return pl.pallas_call(
        kern, grid=(T // tT,),
        in_specs=[pl.BlockSpec((RG_SLABS, rows, LANES), lambda i: (0, i, 0)),
                  pl.BlockSpec((S5_SLABS, rows, LANES), lambda i: (0, i, 0)),
                  pl.BlockSpec((S5_SLABS, rows, LANES), lambda i: (2 * RG_SLABS // S5_SLABS, i, 0)),
                  pl.BlockSpec((B, tT, D_MODEL), lambda i: (0, i, 0)),
                  pl.BlockSpec((B, N_MOD, D_MODEL), lambda i: (0, 0, 0)),
                  pl.BlockSpec((D_MODEL, D_MODEL), const, pipeline_mode=pl.Buffered(1)),
                  pl.BlockSpec((D_S5, D_S5), const),
                  pl.BlockSpec((1, D_S5), const),
                  pl.BlockSpec((1, D_S5), const),
                  pl.BlockSpec((1, D_RG), const),
                  pl.BlockSpec((1, D_S5), const)],
        out_specs=pl.BlockSpec((B, tT, D_MODEL), lambda i: (0, i, 0)),
        out_shape=jax.ShapeDtypeStruct((B, T, D_MODEL), F32),
        scratch_shapes=[pltpu.VMEM((D_MODEL // LANES, rows, LANES), F32)],
        compiler_params=_cparams(56), name=name)(
            y_rg, y_ssm, proj, x, mod, w_out_bf16, w_glu_bf16, b_glu, s5_d, g_rg, g_s5)


def _premoe_kernel(x_ref, mod_ref, g_ref, wr_ref, h_ref, a_ref):
    h = _rms(x_ref[...]) * g_ref[...] * (1.0 + mod_ref[0, 4:5, :]) + mod_ref[0, 3:4, :]
    h_ref[...] = h.astype(BF16)
    lg = lax.dot_general(wr_ref[...], h, (((1,), (1,)), ((), ())), precision=HIGHEST,
                         preferred_element_type=F32)
    e = jnp.exp(lg - jnp.max(lg, axis=0, keepdims=True))
    a_ref[...] = e / jnp.sum(e, axis=0, keepdims=True)


def _premoe(x1, mod, g2, w_router_t, T, name):
    n = x1.shape[0]
    rows = 512
    per_req = T // rows if mod.shape[0] > 1 else n
    return pl.pallas_call(
        _premoe_kernel, grid=(n // rows,),
        in_specs=[pl.BlockSpec((rows, D_MODEL), lambda i: (i, 0)),
                  pl.BlockSpec((1, N_MOD, D_MODEL), lambda i: (i // per_req, 0, 0)),
                  pl.BlockSpec((1, D_MODEL), lambda i: (0, 0)),
                  pl.BlockSpec((N_EXPERTS, D_MODEL), lambda i: (0, 0))],
        out_specs=[pl.BlockSpec((rows, D_MODEL), lambda i: (i, 0)),
                   pl.BlockSpec((N_EXPERTS, rows), lambda i: (0, i))],
        out_shape=[jax.ShapeDtypeStruct((n, D_MODEL), BF16),
                   jax.ShapeDtypeStruct((N_EXPERTS, n), F32)],
        compiler_params=_cparams(40), name=name)(x1, mod, g2, w_router_t)


def _topk_kernel(a_ref, u_ref, s_ref, st_ref, c_ref, *, cap, T, nreq):
    capf = float(cap)

    def count_ge(r, v):
        return jnp.sum((a_ref[:, r * T:(r + 1) * T] >= v).astype(F32), axis=1, keepdims=True)

    def bit_step(i, thrs):
        bit = jnp.int32(1) << (30 - i)
        out = []
        for r in range(nreq):
            cand = thrs[r] | bit
            ok = count_ge(r, pltpu.bitcast(cand, F32)) >= capf
            out.append(jnp.where(ok, cand, thrs[r]))
        return tuple(out)

    zero = jnp.zeros((N_EXPERTS, 1), jnp.int32)
    thrs = lax.fori_loop(0, 31, bit_step, (zero,) * nreq)

    def mid_step(i, c):
        out = []
        for r in range(nreq):
            lo, hi = c[2 * r], c[2 * r + 1]
            mid = 0.5 * (lo + hi)
            ok = count_ge(r, mid) >= capf
            out += [jnp.where(ok, mid, lo), jnp.where(ok, hi, mid)]
        return tuple(out)

    init = []
    for r in range(nreq):
        init += [pltpu.bitcast(thrs[r], F32), pltpu.bitcast(thrs[r] + 1, F32)]
    bounds = lax.fori_loop(0, 24, mid_step, tuple(init))

    u = u_ref[...]
    nblk = T // TOKEN_BLOCK
    cl = lax.broadcasted_iota(jnp.int32, c_ref.shape, 1)
    cnt = jnp.zeros(c_ref.shape, F32)
    for r in range(nreq):
        aff = a_ref[:, r * T:(r + 1) * T]
        lo, hi = bounds[2 * r], bounds[2 * r + 1]
        gt = aff >= hi
        eq = (aff >= lo) & jnp.logical_not(gt)
        need = capf - jnp.sum(gt.astype(F32), axis=1, keepdims=True)
        pe = _dot(eq.astype(F32).astype(BF16), u)
        sel = gt | (eq & (pe <= need))
        ps = _dot(sel.astype(F32).astype(BF16), u)
        slot = jnp.where(sel, ps - 1.0, -1.0)
        s_ref[:, r * T:(r + 1) * T] = slot
        st_ref[r * T:(r + 1) * T, :] = slot.T
        for k in range(nblk):
            end = (k + 1) * TOKEN_BLOCK
            cnt = jnp.where(cl == r * nblk + k, ps[:, end - 1:end], cnt)
    c_ref[...] = cnt


def _topk(aff_t, tri, T, name):
    n = aff_t.shape[1]
    nreq = n // T
    ncnt = nreq * (T // TOKEN_BLOCK)
    kern = functools.partial(_topk_kernel, cap=T // 8, T=T, nreq=nreq)
    return pl.pallas_call(
        kern, grid=(1,),
        in_specs=[pl.BlockSpec((N_EXPERTS, n), lambda i: (0, 0)),
                  pl.BlockSpec((T, T), lambda i: (0, 0))],
        out_specs=[pl.BlockSpec((N_EXPERTS, n), lambda i: (0, 0)),
                   pl.BlockSpec((n, N_EXPERTS), lambda i: (0, 0)),
                   pl.BlockSpec((N_EXPERTS, ncnt), lambda i: (0, 0))],
        out_shape=[jax.ShapeDtypeStruct((N_EXPERTS, n), F32),
                   jax.ShapeDtypeStruct((n, N_EXPERTS), F32),
                   jax.ShapeDtypeStruct((N_EXPERTS, ncnt), F32)],
        compiler_params=_cparams(48), name=name)(aff_t, tri)


def _window_starts(start_ref, nblk):
    base = (pl.program_id(0) * nblk + pl.program_id(1)) * N_EXPERTS
    return [pl.multiple_of(start_ref[base + e], 16) for e in range(N_EXPERTS)]


def _gather_kernel(start_ref, ok_ref, h_ref, s_ref, a_ref, xg_ref, gt_ref, *, cap, W, nblk):
    @pl.when(pl.program_id(1) == 0)
    def _():
        xg_ref[...] = jnp.zeros_like(xg_ref)
        gt_ref[...] = jnp.zeros_like(gt_ref)

    h = h_ref[...]
    nch = 512

    def step(w, starts):
        j = lax.broadcasted_iota(jnp.int32, (w, TOKEN_BLOCK), 0).astype(F32)
        ohs = []
        for e in range(N_EXPERTS):
            rel = s_ref[e:e + 1, :]
            if starts is not None:
                rel = rel - starts[e].astype(F32)
            oh = rel == j
            rows = pl.ds(starts[e], w) if starts is not None else slice(0, w)
            gt_ref[e, rows, :] += jnp.sum(jnp.where(oh, a_ref[e:e + 1, :], 0.0), axis=1, keepdims=True)
            ohs.append(oh.astype(F32).astype(BF16))
        ohm = jnp.concatenate(ohs, axis=0)
        for c in range(0, D_MODEL, nch):
            res = _dot(ohm, h[:, c:c + nch])
            for e in range(N_EXPERTS):
                rows = pl.ds(starts[e], w) if starts is not None else slice(0, w)
                xg_ref[e, rows, c:c + nch] += res[e * w:(e + 1) * w].astype(BF16)

    if W == cap:
        step(cap, None)
    else:
        ok = ok_ref[pl.program_id(0)] > 0

        @pl.when(ok)
        def _():
            step(W, _window_starts(start_ref, nblk))

        @pl.when(jnp.logical_not(ok))
        def _():
            step(cap, None)


def _gather(h2, slot, aff_t, starts, ok, T, name):
    n = h2.shape[0]
    cap = T // 8
    nreq = n // T
    nblk = T // TOKEN_BLOCK
    kern = functools.partial(_gather_kernel, cap=cap, W=min(SLOT_WINDOW, cap), nblk=nblk)
    tok = lambda b, k, *_: (0, b * nblk + k)
    req = lambda b, k, *_: (0, b, 0)
    return pl.pallas_call(
        kern,
        grid_spec=pltpu.PrefetchScalarGridSpec(
            num_scalar_prefetch=2, grid=(nreq, nblk),
            in_specs=[pl.BlockSpec((TOKEN_BLOCK, D_MODEL), lambda b, k, *_: (b * nblk + k, 0)),
                      pl.BlockSpec((N_EXPERTS, TOKEN_BLOCK), tok),
                      pl.BlockSpec((N_EXPERTS, TOKEN_BLOCK), tok)],
            out_specs=[pl.BlockSpec((N_EXPERTS, cap, D_MODEL), req),
                       pl.BlockSpec((N_EXPERTS, cap, 1), req)]),
        out_shape=[jax.ShapeDtypeStruct((N_EXPERTS, nreq * cap, D_MODEL), BF16),
                   jax.ShapeDtypeStruct((N_EXPERTS, nreq * cap, 1), F32)],
        compiler_params=_cparams(56), name=name)(starts, ok, h2, slot, aff_t)


def _ffn_kernel(xc_ref, xl_ref, gc_ref, gl_ref, w1_ref, w3_ref, w2_ref, yc_ref, yl_ref, xbuf, acc, *, rc, nf, rsub):
    f = pl.program_id(2)

    @pl.when(f == 0)
    def _():
        xbuf[0:rc, :] = xc_ref[0]
        xbuf[rc:, :] = xl_ref[0]

    @pl.when(f == 0)
    def _():
        acc[...] = jnp.zeros_like(acc)

    w1 = w1_ref[0].astype(BF16)
    w3 = w3_ref[0].astype(BF16)
    w2 = w2_ref[0].astype(BF16)
    rows = xbuf.shape[0]
    for r0 in range(0, rows, rsub):
        x = xbuf[r0:r0 + rsub, :]
        hid = jax.nn.silu(_dot(x, w1)) * _dot(x, w3)
        acc[r0:r0 + rsub, :] += _dot(hid.astype(BF16), w2)

    @pl.when(f == nf - 1)
    def _():
        yc_ref[0] = (acc[0:rc, :] * gc_ref[0]).astype(BF16)
        yl_ref[0] = (acc[rc:, :] * gl_ref[0]).astype(BF16)


def _ffn(xg_c, xg_l, gt_c, gt_l, w1, w3, w2):
    rc = xg_c.shape[1] // 2
    rl = xg_l.shape[1] // 2
    bf = 256
    nf = D_MODEL // bf
    kern = functools.partial(_ffn_kernel, rc=rc, nf=nf, rsub=(rc + rl) // 2)
    rowsp = lambda r, w: pl.BlockSpec((1, r, w), lambda e, h, f: (e, h, 0))
    return pl.pallas_call(
        kern, grid=(N_EXPERTS, 2, nf),
        in_specs=[rowsp(rc, D_MODEL), rowsp(rl, D_MODEL), rowsp(rc, 1), rowsp(rl, 1),
                  pl.BlockSpec((1, D_MODEL, bf), lambda e, h, f: (e, 0, f)),
                  pl.BlockSpec((1, D_MODEL, bf), lambda e, h, f: (e, 0, f)),
                  pl.BlockSpec((1, bf, D_MODEL), lambda e, h, f: (e, f, 0))],
        out_specs=[rowsp(rc, D_MODEL), rowsp(rl, D_MODEL)],
        out_shape=[jax.ShapeDtypeStruct(xg_c.shape, BF16), jax.ShapeDtypeStruct(xg_l.shape, BF16)],
        scratch_shapes=[pltpu.VMEM((rc + rl, D_MODEL), BF16), pltpu.VMEM((rc + rl, D_MODEL), F32)],
        compiler_params=_cparams(56), name="ffn")(xg_c, xg_l, gt_c, gt_l, w1, w3, w2)


def _expand_consts(w):
    lane = jnp.arange(N_EXPERTS * w)
    expand = (jnp.arange(N_EXPERTS)[:, None] == (lane // w)[None, :]).astype(BF16)
    return expand, (lane % w).astype(F32)[None, :]


def _combine_kernel(start_ref, ok_ref, x_ref, y_ref, st_ref, sv_ref, mod_ref, fg_ref, exw_ref, jlw_ref,
                    exd_ref, jld_ref, o_ref, stage, *, cap, W, nblk):
    st = st_ref[...]

    def finish(moe):
        x2 = x_ref[...] + mod_ref[0, 5:6, :] * moe
        o_ref[...] = _rms(x2) * fg_ref[...]

    def dense():
        rep = _dot(st.astype(BF16), exd_ref[...])
        oh = (rep == jld_ref[...]).astype(F32).astype(BF16)
        finish(_dot(oh, y_ref[...].reshape(N_EXPERTS * cap, D_MODEL)))

    if W == cap:
        dense()
        return
    ok = ok_ref[pl.program_id(0)] > 0

    @pl.when(ok)
    def _():
        starts = _window_starts(start_ref, nblk)
        for e in range(N_EXPERTS):
            stage[e * W:(e + 1) * W, :] = y_ref[e, pl.ds(starts[e], W), :]
        rep = _dot((st - sv_ref[0]).astype(BF16), exw_ref[...])
        oh = (rep == jlw_ref[...]).astype(F32).astype(BF16)
        finish(_dot(oh, stage[...]))

    @pl.when(jnp.logical_not(ok))
    def _():
        dense()


def _combine(x1, ybuf, slot_t, starts, ok, starts_f, mod, final_g, T, name):
    n = x1.shape[0]
    cap = T // 8
    nblk = T // TOKEN_BLOCK
    W = min(SLOT_WINDOW, cap)
    multi = mod.shape[0] > 1
    exw, jlw = _expand_consts(W)
    exd, jld = _expand_consts(cap)
    kern = functools.partial(_combine_kernel, cap=cap, W=W, nblk=nblk)
    tok = lambda b, k, *_: (b * nblk + k, 0)
    const = lambda b, k, *_: (0, 0)
    return pl.pallas_call(
        kern,
        grid_spec=pltpu.PrefetchScalarGridSpec(
            num_scalar_prefetch=2, grid=(n // T, nblk),
            in_specs=[pl.BlockSpec((TOKEN_BLOCK, D_MODEL), tok),
                      pl.BlockSpec((N_EXPERTS, cap, D_MODEL), lambda b, k, *_: (0, b, 0)),
                      pl.BlockSpec((TOKEN_BLOCK, N_EXPERTS), tok),
                      pl.BlockSpec((1, 1, N_EXPERTS), lambda b, k, *_: (b * nblk + k, 0, 0)),
                      pl.BlockSpec((1, N_MOD, D_MODEL), lambda b, k, *_: (b if multi else 0, 0, 0)),
                      pl.BlockSpec((1, D_MODEL), const),
                      pl.BlockSpec(exw.shape, const), pl.BlockSpec(jlw.shape, const),
                      pl.BlockSpec(exd.shape, const), pl.BlockSpec(jld.shape, const)],
            out_specs=pl.BlockSpec((TOKEN_BLOCK, D_MODEL), tok),
            scratch_shapes=[pltpu.VMEM((N_EXPERTS * W, D_MODEL), BF16)]),
        out_shape=jax.ShapeDtypeStruct((n, D_MODEL), F32),
        compiler_params=_cparams(56), name=name)(starts, ok, x1, ybuf, slot_t, starts_f, mod, final_g,
                                                 exw, jlw, exd, jld)


def _window_tables(cnt, T):
    cap = T // 8
    nblk = T // TOKEN_BLOCK
    nreq = cnt.shape[1] // nblk
    W = min(SLOT_WINDOW, cap)
    end = cnt.astype(jnp.int32).reshape(N_EXPERTS, nreq, nblk)
    first = jnp.concatenate([jnp.zeros_like(end[..., :1]), end[..., :-1]], axis=-1)
    start = jnp.minimum((first // 16) * 16, cap - W)
    ok = jnp.all(end - start <= W, axis=(0, 2)).astype(jnp.int32)
    start = start.transpose(1, 2, 0)
    return start.reshape(-1), ok, start.astype(F32).reshape(nreq * nblk, 1, N_EXPERTS)


def _blockdiag_pairs(w):
    w = w.reshape(2, RG_SLABS, 2, 64, 64)
    bd = jnp.einsum('dshij,hk->dshikj', w, jnp.eye(2, dtype=w.dtype))
    return bd.reshape(2, RG_SLABS, LANES, LANES).transpose(1, 0, 2, 3)


def kernel(x_prompt, x_sample, state_rglru, state_s5_re, state_s5_im, c, c_ctx, norm1_g, norm2_g, w_ada, b_ada, w_in, rg_conv_w, rg_conv_b, rg_wa, rg_ba, rg_wx, rg_bx, rg_lam, s5_a_re, s5_a_im, s5_log_dt, s5_b_re, s5_b_im, s5_c_re, s5_c_im, s5_d, s5_w_glu, s5_b_glu, gnorm_rg, gnorm_s5, w_out, w_router, w1, w3, w2, final_g):
    l = 0
    Bp, Tp, _ = x_prompt.shape
    Bs, Ts, _ = x_sample.shape

    c16 = jnp.concatenate([c, c_ctx[None], jnp.zeros((16 - Bs - 1, D_MODEL), F32)], axis=0)
    mod = _adaln(c16, w_ada[l], b_ada[l]).reshape(16, N_MOD, D_MODEL)
    mod_lat = mod[:Bs]
    mod_ctx1 = mod[Bs:Bs + 1]
    mod_ctx = jnp.broadcast_to(mod_ctx1, (Bp, N_MOD, D_MODEL))

    w_in_b = w_in[l].astype(BF16)
    w_out_b = w_out[l].astype(BF16)
    w_glu_b = s5_w_glu[l].astype(BF16)
    row = lambda v: v.reshape(1, -1)

    wg = jnp.concatenate([_blockdiag_pairs(rg_wa[l]), _blockdiag_pairs(rg_wx[l])], axis=-1).astype(BF16)
    bg = jnp.concatenate([rg_ba[l].reshape(2, RG_SLABS, LANES), rg_bx[l].reshape(2, RG_SLABS, LANES)],
                         axis=-1).transpose(1, 0, 2).reshape(RG_SLABS, 2, 1, 2 * LANES)
    kmat, pmat, qmat, lam16 = _s5prep(s5_a_re[l], s5_a_im[l], s5_log_dt[l], s5_b_re[l], s5_b_im[l],
                                       s5_c_re[l], s5_c_im[l])

    def mixer(x, modp, rg_h0, s5_h0, lat, tag):
        B, T, _ = x.shape
        R = T * B
        proj = _inproj(x, modp, row(norm1_g[l]), w_in_b, "inproj_" + tag)
        y_rg, fin_rg = _rglru(proj, rg_conv_w[l], row(rg_conv_b[l]), wg, bg, rg_lam[l], rg_h0, B, T,
                              "rglru_" + tag)
        if lat:
            A, Bq = T // GRID_W // CHUNK, GRID_W * B
        else:
            A, Bq = T // CHUNK, B
        y_ssm, fin_s5 = _s5(proj.reshape(N_SLAB, A, CHUNK, Bq, LANES), kmat, pmat, qmat, lam16, s5_h0,
                            A, Bq, B, lat, "s5_" + tag)
        x1 = _post(y_rg, y_ssm.reshape(S5_SLABS, R, LANES), proj, x, modp, w_out_b, w_glu_b,
                   row(s5_b_glu[l]), row(s5_d[l]), row(gnorm_rg[l]), row(gnorm_s5[l]), "post_" + tag)
        return x1.reshape(R, D_MODEL), fin_rg, fin_s5

    z_rg = jnp.zeros((2, Bp, D_RG), F32)
    z_s5 = jnp.zeros((S5_GROUPS, Bp, 256), F32)
    x1_c, fin_rg, fin_s5 = mixer(x_prompt, mod_ctx, z_rg, z_s5, False, "ctx")
    rg_h0 = state_rglru[:, l].transpose(1, 0, 2)
    s5_h0 = jnp.stack([state_s5_re[:, l, 0], state_s5_re[:, l, 1], state_s5_im[:, l, 0], state_s5_im[:, l, 1]],
                      axis=0)
    s5_h0 = s5_h0.transpose(2, 1, 0, 3).reshape(S5_GROUPS, Bs, 256)
    x1_s, _, _ = mixer(x_sample, mod_lat, rg_h0, s5_h0, True, "lat")

    w_router_t = w_router[l].T
    g2 = row(norm2_g[l])

    def route(x1, modp, T, tag):
        h2, aff_t = _premoe(x1, modp, g2, w_router_t, T, "premoe_" + tag)
        tri = jnp.triu(jnp.ones((T, T), BF16))
        slot, slot_t, cnt = _topk(aff_t, tri, T, "topk_" + tag)
        win = _window_tables(cnt, T)
        xg, gates = _gather(h2, slot, aff_t, win[0], win[1], T, "gather_" + tag)
        return xg, gates, slot_t, win

    xg_c, gt_c, st_c, win_c = route(x1_c, mod_ctx1, Tp, "ctx")
    xg_s, gt_s, st_s, win_s = route(x1_s, mod_lat, Ts, "lat")
    y_c, y_s = _ffn(xg_c, xg_s, gt_c, gt_s, w1[l], w3[l], w2[l])
    fg = row(final_g)
    y_prompt = _combine(x1_c, y_c, st_c, *win_c, mod_ctx1, fg, Tp, "combine_ctx").reshape(Bp, Tp, D_MODEL)
    y_sample = _combine(x1_s, y_s, st_s, *win_s, mod_lat, fg, Ts, "combine_lat").reshape(Bs, Ts, D_MODEL)

    new_rg = fin_rg.transpose(1, 0, 2)[:, None]
    f4 = fin_s5.reshape(S5_GROUPS, Bp, 4, 64).transpose(1, 2, 0, 3)
    new_re = f4[:, None, 0:2]
    new_im = f4[:, None, 2:4]
    return (y_prompt, y_sample, new_rg, new_re, new_im)
```

```python
import functools

import jax
import jax.numpy as jnp
from jax import lax
from jax.experimental import pallas as pl
from jax.experimental.pallas import tpu as pltpu

F32 = jnp.float32
BF16 = jnp.bfloat16
HIGHEST = lax.Precision.HIGHEST

LANES = 128
D_MODEL = 2048
D_RG = 1536
D_S5 = 512
D_IN = 2 * D_RG + D_S5
N_SLAB = D_IN // LANES
RG_SLABS = D_RG // LANES
S5_SLABS = D_S5 // LANES
S5_GROUPS = 32
GROUPS_PER_SLAB = 8
CHUNK = 16
N_EXPERTS = 16
N_MOD = 6
EPS = 1e-6
RG_C = 8.0
GRID_W = 64
TOKEN_BLOCK = 256
SLOT_WINDOW = 80
MIB = 1024 * 1024


def _cparams(vmem_mib):
    return pltpu.CompilerParams(vmem_limit_bytes=vmem_mib * MIB)


def _dot(a, b):
    return jnp.dot(a, b, preferred_element_type=F32)


def _rms(x):
    return x * lax.rsqrt(jnp.mean(x * x, axis=-1, keepdims=True) + EPS)


def _adaln_kernel(c_ref, w_ref, b_ref, o_ref):
    s = jax.nn.silu(c_ref[...]).astype(BF16)
    o_ref[...] = _dot(s, w_ref[...].astype(BF16)) + b_ref[...]


def _adaln(c16, w_ada, b_ada):
    n = w_ada.shape[1]
    bn = 1024
    return pl.pallas_call(
        _adaln_kernel, grid=(n // bn,),
        in_specs=[pl.BlockSpec((16, D_MODEL), lambda j: (0, 0)),
                  pl.BlockSpec((D_MODEL, bn), lambda j: (0, j)),
                  pl.BlockSpec((1, bn), lambda j: (0, j))],
        out_specs=pl.BlockSpec((16, bn), lambda j: (0, j)),
        out_shape=jax.ShapeDtypeStruct((16, n), F32),
        compiler_params=_cparams(40), name="adaln")(c16, w_ada, b_ada.reshape(1, n))


def _inproj_kernel(x_ref, mod_ref, g_ref, w_ref, o_ref, *, B, tT, nch):
    x = x_ref[...]
    shift = mod_ref[:, 0:1, :]
    scale = mod_ref[:, 1:2, :]
    h = _rms(x) * g_ref[...] * (1.0 + scale) + shift
    hb = h.reshape(B * tT, D_MODEL).astype(BF16)
    for c0 in range(0, D_IN, nch):
        p = _dot(hb, w_ref[:, c0:c0 + nch])
        for b in range(B):
            for j in range(nch // LANES):
                o_ref[c0 // LANES + j, pl.ds(b, tT, stride=B), :] = (
                    p[b * tT:(b + 1) * tT, j * LANES:(j + 1) * LANES])


def _inproj(x, mod, g1, w_in_bf16, name):
    B, T, _ = x.shape
    tT = 512 // B
    kern = functools.partial(_inproj_kernel, B=B, tT=tT, nch=512)
    return pl.pallas_call(
        kern, grid=(T // tT,),
        in_specs=[pl.BlockSpec((B, tT, D_MODEL), lambda i: (0, i, 0)),
                  pl.BlockSpec((B, N_MOD, D_MODEL), lambda i: (0, 0, 0)),
                  pl.BlockSpec((1, D_MODEL), lambda i: (0, 0)),
                  pl.BlockSpec((D_MODEL, D_IN), lambda i: (0, 0), pipeline_mode=pl.Buffered(1))],
        out_specs=pl.BlockSpec((N_SLAB, tT * B, LANES), lambda i: (0, i, 0)),
        out_shape=jax.ShapeDtypeStruct((N_SLAB, T * B, LANES), F32),
        compiler_params=_cparams(56), name=name)(x, mod, g1, w_in_bf16)


def _rglru_kernel(xb_ref, gt_ref, cw_ref, cb_ref, wg_ref, bg_ref, lam_ref, h0_ref, y_ref, fin_ref,
                  *, B, T, CT):
    RC = CT * B
    nc = T // CT
    cw = [cw_ref[k:k + 1, :] for k in range(4)]
    cb = cb_ref[...]
    nsp = [(-RG_C) * jax.nn.softplus(-lam_ref[d:d + 1, :]) for d in range(2)]

    def chunk(n, d):
        t0 = n * CT
        row0 = pl.multiple_of(t0 * B, RC)
        cur = xb_ref[0, pl.ds(row0, RC), :]
        p0 = pl.multiple_of(jnp.maximum(t0 - 1, 0) * B, B)
        prev = xb_ref[0, pl.ds(p0, B), :] * jnp.where(t0 > 0, 1.0, 0.0)
        n0 = pl.multiple_of(jnp.minimum(t0 + CT, T - 2) * B, B)
        nxt = xb_ref[0, pl.ds(n0, 2 * B), :] * jnp.where(t0 + CT < T, 1.0, 0.0)
        ext = jnp.concatenate([prev, cur, nxt], axis=0)
        xc = (cw[0] * ext[0:RC] + cw[1] * ext[B:B + RC] + cw[2] * ext[2 * B:2 * B + RC]
              + cw[3] * ext[3 * B:3 * B + RC] + cb)
        g = _dot(xc.astype(BF16), wg_ref[0, d]) + bg_ref[0, d]
        th = jnp.tanh(0.5 * g)
        r = 0.5 + 0.5 * th[:, :LANES]
        i = 0.5 + 0.5 * th[:, LANES:]
        a = jnp.exp(r * nsp[d])
        om = 1.0 - a * a
        bb = jnp.where(om > 0.0, om * lax.rsqrt(om), 0.0) * (i * xc)
        return a, bb, row0

    def make_body(finish):
        def body(n, carry):
            hf, hb = carry
            af, bf, rf = chunk(n, 0)
            ab, bk, rb = chunk(nc - 1 - n, 1)
            if finish:
                glf = jax.nn.gelu(gt_ref[0, pl.ds(rf, RC), :])
                glb = jax.nn.gelu(gt_ref[0, pl.ds(rb, RC), :])
                of = y_ref[0, pl.ds(rf, RC), :]
                ob = y_ref[0, pl.ds(rb, RC), :]
            for t in range(CT):
                u = CT - 1 - t
                hf = af[t * B:(t + 1) * B] * hf + bf[t * B:(t + 1) * B]
                hb = ab[u * B:(u + 1) * B] * hb + bk[u * B:(u + 1) * B]
                rows_f = pl.ds(rf + t * B, B)
                rows_b = pl.ds(rb + u * B, B)
                if finish:
                    y_ref[0, rows_f, :] = (of[t * B:(t + 1) * B] + hf) * glf[t * B:(t + 1) * B]
                    y_ref[0, rows_b, :] = (ob[u * B:(u + 1) * B] + hb) * glb[u * B:(u + 1) * B]
                else:
                    y_ref[0, rows_f, :] = hf
                    y_ref[0, rows_b, :] = hb
            return hf, hb
        return body

    carry = lax.fori_loop(0, nc // 2, make_body(False), (h0_ref[0], h0_ref[1]))
    hf, hb = lax.fori_loop(nc // 2, nc, make_body(True), carry)
    fin_ref[0] = hf
    fin_ref[1] = hb


def _rglru(proj, cw, cb, wg, bg, lam, h0, B, T, name):
    R = T * B
    kern = functools.partial(_rglru_kernel, B=B, T=T, CT=256 // B)
    return pl.pallas_call(
        kern, grid=(RG_SLABS,),
        in_specs=[pl.BlockSpec((1, R, LANES), lambda s: (s, 0, 0)),
                  pl.BlockSpec((1, R, LANES), lambda s: (s + RG_SLABS, 0, 0)),
                  pl.BlockSpec((4, LANES), lambda s: (0, s)),
                  pl.BlockSpec((1, LANES), lambda s: (0, s)),
                  pl.BlockSpec((1, 2, LANES, 2 * LANES), lambda s: (s, 0, 0, 0)),
                  pl.BlockSpec((1, 2, 1, 2 * LANES), lambda s: (s, 0, 0, 0)),
                  pl.BlockSpec((2, LANES), lambda s: (0, s)),
                  pl.BlockSpec((2, B, LANES), lambda s: (0, 0, s))],
        out_specs=[pl.BlockSpec((1, R, LANES), lambda s: (s, 0, 0)),
                   pl.BlockSpec((2, B, LANES), lambda s: (0, 0, s))],
        out_shape=[jax.ShapeDtypeStruct((RG_SLABS, R, LANES), F32),
                   jax.ShapeDtypeStruct((2, B, D_RG), F32)],
        compiler_params=_cparams(58), name=name)(proj, proj, cw, cb, wg, bg, lam, h0)


def _cpow(pr, pi, n):
    wr = jnp.ones(n.shape, F32)
    wi = jnp.zeros(n.shape, F32)
    for k in range(4):
        bit = (n & (1 << k)) != 0
        wr, wi = jnp.where(bit, wr * pr - wi * pi, wr), jnp.where(bit, wr * pi + wi * pr, wi)
        pr, pi = pr * pr - pi * pi, 2.0 * pr * pi
    return wr, wi


def _s5prep_kernel(ar_ref, ai_ref, ldt_ref, arc_ref, aic_ref, ldtc_ref, btr_ref, bti_ref, ctr_ref, cti_ref,
                   k_ref, p_ref, q_ref, l_ref):
    ar = ar_ref[0]
    ai = ai_ref[0]
    dt = jnp.exp(ldt_ref[0])
    mag = jnp.exp(dt * ar)
    lr = mag * jnp.cos(dt * ai)
    li = mag * jnp.sin(dt * ai)
    den = ar * ar + ai * ai
    nre = lr - 1.0
    qre = (nre * ar + li * ai) / den
    qim = (li * ar - nre * ai) / den
    btr = btr_ref[0]
    bti = bti_ref[0]
    bbr = qre * btr - qim * bti
    bbi = qre * bti + qim * btr
    rowblk = lax.broadcasted_iota(jnp.int32, (256, LANES), 0) // CHUNK
    isf = lax.broadcasted_iota(jnp.int32, (256, LANES), 1) < 64

    def powers(n):
        return _cpow(lr, li, n)

    ctr = ctr_ref[0]
    cti = cti_ref[0]

    wr, wi = powers(rowblk)
    zr = wr * bbr - wi * bbi
    zi = wr * bbi + wi * bbr
    zero = jnp.zeros_like(zr)

    def mm(a, b):
        return jnp.dot(a, b, precision=HIGHEST, preferred_element_type=F32)

    mall_f = mm(jnp.where(isf, zr, zero), ctr) - mm(jnp.where(isf, zi, zero), cti)
    mall_b = mm(jnp.where(isf, zero, zr), ctr) - mm(jnp.where(isf, zero, zi), cti)
    blocks = [mall_f[(15 - q) * CHUNK:(16 - q) * CHUNK] for q in range(15)]
    blocks.append(mall_f[0:CHUNK] + mall_b[0:CHUNK])
    blocks += [mall_b[l * CHUNK:(l + 1) * CHUNK] for l in range(1, 16)]
    blocks.append(jnp.zeros((CHUNK, 256), F32))
    gw = jnp.concatenate(blocks, axis=0)
    colblk = lax.broadcasted_iota(jnp.int32, (256, 256), 1) // CHUNK
    kmat = jnp.zeros((256, 256), F32)
    for tp in range(CHUNK):
        kmat = jnp.where(colblk == tp, gw[(15 - tp) * CHUNK:(15 - tp) * CHUNK + 256], kmat)
    k_ref[0] = kmat.astype(BF16)

    wr, wi = powers(jnp.where(isf, 15 - rowblk, rowblk))
    p_ref[0] = jnp.concatenate([wr * bbr - wi * bbi, wr * bbi + wi * bbr], axis=1).astype(BF16)

    arc = arc_ref[0]
    aic = aic_ref[0]
    dtc = jnp.exp(ldtc_ref[0])
    laneblk = lax.broadcasted_iota(jnp.int32, (LANES, 256), 1) // CHUNK
    rowf = lax.broadcasted_iota(jnp.int32, (LANES, 256), 0) < 64
    magc = jnp.exp(dtc * arc)
    lrc = magc * jnp.cos(dtc * aic)
    lic = magc * jnp.sin(dtc * aic)
    vr, vi = _cpow(lrc, lic, jnp.where(rowf, laneblk, 15 - laneblk))
    wr = vr * lrc - vi * lic
    wi = vr * lic + vi * lrc
    q_ref[0] = jnp.concatenate([ctr * wr - cti * wi, -(ctr * wi + cti * wr)], axis=0).astype(BF16)

    pr, pi = lr, li
    for _ in range(4):
        pr, pi = pr * pr - pi * pi, 2.0 * pr * pi
    l_ref[0] = jnp.concatenate([pr, pi], axis=1)


def _s5prep(a_re, a_im, log_dt, b_re, b_im, c_re, c_im):
    G = S5_GROUPS

    def rows(x):
        return jnp.concatenate([x[0], x[1]], axis=-1).reshape(G, 1, LANES)

    ldt = jnp.broadcast_to(log_dt[:, :, None], (2, G, 64))
    ar, ai, ld = rows(a_re), rows(a_im), rows(ldt)
    arc, aic, ldc = (v.reshape(G, LANES, 1) for v in (ar, ai, ld))

    def btile(b):
        bt = jnp.tile(jnp.swapaxes(b, 2, 3), (1, 1, CHUNK, 1))
        return jnp.concatenate([bt[0], bt[1]], axis=-1)

    def ctile(c):
        ct = jnp.tile(jnp.swapaxes(c, 2, 3), (1, 1, 1, CHUNK))
        return jnp.concatenate([ct[0], ct[1]], axis=1)

    vec = pl.BlockSpec((1, 1, LANES), lambda g: (g, 0, 0))
    col = pl.BlockSpec((1, LANES, 1), lambda g: (g, 0, 0))
    bsp = pl.BlockSpec((1, 256, LANES), lambda g: (g, 0, 0))
    csp = pl.BlockSpec((1, LANES, 256), lambda g: (g, 0, 0))
    msp = pl.BlockSpec((1, 256, 256), lambda g: (g, 0, 0))
    return pl.pallas_call(
        _s5prep_kernel, grid=(G,),
        in_specs=[vec, vec, vec, col, col, col, bsp, bsp, csp, csp],
        out_specs=[msp, msp, msp, pl.BlockSpec((1, 1, 256), lambda g: (g, 0, 0))],
        out_shape=[jax.ShapeDtypeStruct((G, 256, 256), BF16)] * 3 + [jax.ShapeDtypeStruct((G, 1, 256), F32)],
        compiler_params=_cparams(32), name="s5prep")(
            ar, ai, ld, arc, aic, ldc, btile(b_re), btile(b_im), ctile(c_re), ctile(c_im))


def _seg_transpose8(vs, seg):
    vs = list(vs)
    for d in (4, 2, 1):
        hi = (seg & d) != 0
        new = list(vs)
        for i in range(8):
            if i & d:
                continue
            j = i + d
            new[i] = jnp.where(hi, pltpu.roll(vs[j], d * CHUNK, axis=1), vs[i])
            new[j] = jnp.where(hi, vs[j], pltpu.roll(vs[i], LANES - d * CHUNK, axis=1))
        vs = new
    return vs


def _s5_kernel(u_ref, k_ref, p_ref, q_ref, l_ref, h0_ref, y_ref, fin_ref, ebuf, ybuf, *, A, Bq, B, lat):
    RP = A * Bq
    nchunk = RP // B
    G = GROUPS_PER_SLAB
    nr = 256
    qn = min(Bq, nr)
    tiles = [(r0, [((r0 + i * qn) // Bq, (r0 + i * qn) % Bq) for i in range(nr // qn)])
             for r0 in range(0, RP, nr)]
    lane_blk = lax.broadcasted_iota(jnp.int32, (nr, LANES), 1) // CHUNK

    for r0, pieces in tiles:
        xs = [jnp.concatenate([u_ref[0, a, tau, q0:q0 + qn, :] for a, q0 in pieces], axis=0)
              for tau in range(CHUNK)]
        halves = [_seg_transpose8(xs[8 * h:8 * h + 8], lane_blk) for h in range(2)]
        for g in range(G):
            ug = jnp.concatenate([halves[0][g], halves[1][g]], axis=1).astype(BF16)
            ybuf[g, r0:r0 + nr, :] = _dot(ug, k_ref[g])
            ebuf[g, r0:r0 + nr, :] = _dot(ug, p_ref[g])

    def off(m):
        if lat:
            return ((m & 1) * GRID_W + (m >> 1)) * B
        return m * B

    fmask = (lax.broadcasted_iota(jnp.int32, (B, 2 * LANES), 1) % LANES) < 64
    fmask1 = lax.broadcasted_iota(jnp.int32, (B, LANES), 1) < 64
    lam = [l_ref[g] for g in range(G)]

    def body(i, carry):
        of = pl.multiple_of(off(i), B)
        ob = pl.multiple_of(off(nchunk - 1 - i), B)
        out = []
        for g in range(G):
            re, im = carry[2 * g], carry[2 * g + 1]
            ef = ebuf[g, pl.ds(of, B), :]
            eb = ebuf[g, pl.ds(ob, B), :]
            s = jnp.concatenate([re, im], axis=1)
            ebuf[g, pl.ds(of, B), :] = jnp.where(fmask, s, ef)
            ebuf[g, pl.ds(ob, B), :] = jnp.where(fmask, eb, s)
            lr = lam[g][:, :LANES]
            li = lam[g][:, LANES:]
            e_re = jnp.where(fmask1, ef[:, :LANES], eb[:, :LANES])
            e_im = jnp.where(fmask1, ef[:, LANES:], eb[:, LANES:])
            out.append(lr * re - li * im + e_re)
            out.append(lr * im + li * re + e_im)
        return tuple(out)

    init = []
    for g in range(G):
        h0 = h0_ref[g]
        init += [h0[:, :LANES], h0[:, LANES:]]
    fin = lax.fori_loop(0, nchunk, body, tuple(init))
    for g in range(G):
        fin_ref[g] = jnp.concatenate([fin[2 * g], fin[2 * g + 1]], axis=1)

    for g in range(G):
        ybuf[g] = ybuf[g] + _dot(ebuf[g].astype(BF16), q_ref[g])
    for r0, pieces in tiles:
        for h in range(2):
            yt = _seg_transpose8([ybuf[g, r0:r0 + nr, h * LANES:(h + 1) * LANES] for g in range(G)], lane_blk)
            for tp in range(8):
                for k, (a, q0) in enumerate(pieces):
                    y_ref[0, a, 8 * h + tp, q0:q0 + qn, :] = yt[tp][k * qn:(k + 1) * qn]


def _s5(proj5, kmat, pmat, qmat, lam16, h0, A, Bq, B, lat, name):
    RP = A * Bq
    G = GROUPS_PER_SLAB
    kern = functools.partial(_s5_kernel, A=A, Bq=Bq, B=B, lat=lat)
    msp = pl.BlockSpec((G, 256, 256), lambda s: (s, 0, 0))
    return pl.pallas_call(
        kern, grid=(S5_SLABS,),
        in_specs=[pl.BlockSpec((1, A, CHUNK, Bq, LANES), lambda s: (s + 2 * RG_SLABS, 0, 0, 0, 0)),
                  msp, msp, msp,
                  pl.BlockSpec((G, 1, 256), lambda s: (s, 0, 0)),
                  pl.BlockSpec((G, B, 256), lambda s: (s, 0, 0))],
        out_specs=[pl.BlockSpec((1, A, CHUNK, Bq, LANES), lambda s: (s, 0, 0, 0, 0)),
                   pl.BlockSpec((G, B, 256), lambda s: (s, 0, 0))],
        out_shape=[jax.ShapeDtypeStruct((S5_SLABS, A, CHUNK, Bq, LANES), F32),
                   jax.ShapeDtypeStruct((S5_GROUPS, B, 256), F32)],
        scratch_shapes=[pltpu.VMEM((G, RP, 256), F32), pltpu.VMEM((G, RP, 256), F32)],
        compiler_params=_cparams(58), name=name)(proj5, kmat, pmat, qmat, lam16, h0)


def _post_kernel(yr_ref, ys_ref, u_ref, x_ref, mod_ref, wo_ref, wg_ref, bgl_ref, d_ref, grg_ref, gs5_ref,
                 o_ref, ybuf, *, B, tT):
    yr = jnp.concatenate([yr_ref[s] for s in range(RG_SLABS)], axis=1)
    nr = (_rms(yr) * grg_ref[...]).astype(BF16)
    ys = jnp.concatenate([ys_ref[s] for s in range(S5_SLABS)], axis=1)
    u = jnp.concatenate([u_ref[s] for s in range(S5_SLABS)], axis=1)
    ys = jax.nn.gelu(ys + d_ref[...] * u)
    ys = ys * jax.nn.sigmoid(_dot(ys.astype(BF16), wg_ref[...]) + bgl_ref[...])
    ns = (_rms(ys) * gs5_ref[...]).astype(BF16)
    y = _dot(nr, wo_ref[0:D_RG, :]) + _dot(ns, wo_ref[D_RG:D_MODEL, :])
    nj = D_MODEL // LANES
    for j in range(nj):
        ybuf[j] = y[:, j * LANES:(j + 1) * LANES]
    for b in range(B):
        yb = jnp.concatenate([ybuf[j, pl.ds(b, tT, stride=B), :] for j in range(nj)], axis=1)
        o_ref[b] = x_ref[b] + mod_ref[b, 2:3, :] * yb


def _post(y_rg, y_ssm, proj, x, mod, w_out_bf16, w_glu_bf16, b_glu, s5_d, g_rg, g_s5, name):
    B, T, _ = x.shape
    tT = 512 // B
    rows = tT * B
    kern = functools.partial(_post_kernel, B=B, tT=tT)
    const = lambda i: (0, 0)
    return pl.pallas_call(
        kern, grid=(T // tT,),
        in_specs=[pl.BlockSpec((RG_SLABS, rows, LANES), lambda i: (0, i, 0)),
                  pl.BlockSpec((S5_SLABS, rows, LANES), lambda i: (0, i, 0)),
                  pl.BlockSpec((S5_SLABS, rows, LANES), lambda i: (2 * RG_SLABS // S5_SLABS, i, 0)),
                  pl.BlockSpec((B, tT, D_MODEL), lambda i: (0, i, 0)),
                  pl.BlockSpec((B, N_MOD, D_MODEL), lambda i: (0, 0, 0)),
                  pl.BlockSpec((D_MODEL, D_MODEL), const, pipeline_mode=pl.Buffered(1)),
                  pl.BlockSpec((D_S5, D_S5), const),
                  pl.BlockSpec((1, D_S5), const),
                  pl.BlockSpec((1, D_S5), const),
                  pl.BlockSpec((1, D_RG), const),
                  pl.BlockSpec((1, D_S5), const)],
        out_specs=pl.BlockSpec((B, tT, D_MODEL), lambda i: (0, i, 0)),
        out_shape=jax.ShapeDtypeStruct((B, T, D_MODEL), F32),
        scratch_shapes=[pltpu.VMEM((D_MODEL // LANES, rows, LANES), F32)],
        compiler_params=_cparams(56), name=name)(
            y_rg, y_ssm, proj, x, mod, w_out_bf16, w_glu_bf16, b_glu, s5_d, g_rg, g_s5)


def _premoe_kernel(x_ref, mod_ref, g_ref, wr_ref, h_ref, a_ref):
    h = _rms(x_ref[...]) * g_ref[...] * (1.0 + mod_ref[0, 4:5, :]) + mod_ref[0, 3:4, :]
    hhi = h.astype(BF16)
    h_ref[...] = hhi
    hlo = (h - hhi.astype(F32)).astype(BF16)
    w = wr_ref[...]
    whi = w.astype(BF16)
    wlo = (w - whi.astype(F32)).astype(BF16)
    nt = lambda a, b: lax.dot_general(a, b, (((1,), (1,)), ((), ())), preferred_element_type=F32)
    lg = nt(whi, hhi) + nt(wlo, hhi) + nt(whi, hlo)
    e = jnp.exp(lg - jnp.max(lg, axis=0, keepdims=True))
    a_ref[...] = e / jnp.sum(e, axis=0, keepdims=True)


def _premoe(x1, mod, g2, w_router_t, T, name):
    n = x1.shape[0]
    rows = 512
    per_req = T // rows if mod.shape[0] > 1 else n
    return pl.pallas_call(
        _premoe_kernel, grid=(n // rows,),
        in_specs=[pl.BlockSpec((rows, D_MODEL), lambda i: (i, 0)),
                  pl.BlockSpec((1, N_MOD, D_MODEL), lambda i: (i // per_req, 0, 0)),
                  pl.BlockSpec((1, D_MODEL), lambda i: (0, 0)),
                  pl.BlockSpec((N_EXPERTS, D_MODEL), lambda i: (0, 0))],
        out_specs=[pl.BlockSpec((rows, D_MODEL), lambda i: (i, 0)),
                   pl.BlockSpec((N_EXPERTS, rows), lambda i: (0, i))],
        out_shape=[jax.ShapeDtypeStruct((n, D_MODEL), BF16),
                   jax.ShapeDtypeStruct((N_EXPERTS, n), F32)],
        compiler_params=_cparams(40), name=name)(x1, mod, g2, w_router_t)


def _topk_kernel(a_ref, u_ref, s_ref, st_ref, c_ref, *, cap, T, nreq):
    capf = float(cap)

    def count_ge(r, v):
        return jnp.sum((a_ref[:, r * T:(r + 1) * T] >= v).astype(F32), axis=1, keepdims=True)

    def bit_step(i, thrs):
        bit = jnp.int32(1) << (30 - i)
        out = []
        for r in range(nreq):
            cand = thrs[r] | bit
            ok = count_ge(r, pltpu.bitcast(cand, F32)) >= capf
            out.append(jnp.where(ok, cand, thrs[r]))
        return tuple(out)

    zero = jnp.zeros((N_EXPERTS, 1), jnp.int32)
    thrs = lax.fori_loop(0, 31, bit_step, (zero,) * nreq)

    def mid_step(i, c):
        out = []
        for r in range(nreq):
            lo, hi = c[2 * r], c[2 * r + 1]
            mid = 0.5 * (lo + hi)
            ok = count_ge(r, mid) >= capf
            out += [jnp.where(ok, mid, lo), jnp.where(ok, hi, mid)]
        return tuple(out)

    init = []
    for r in range(nreq):
        init += [pltpu.bitcast(thrs[r], F32), pltpu.bitcast(thrs[r] + 1, F32)]
    bounds = lax.fori_loop(0, 24, mid_step, tuple(init))

    u = u_ref[...]
    nblk = T // TOKEN_BLOCK
    cl = lax.broadcasted_iota(jnp.int32, c_ref.shape, 1)
    cnt = jnp.zeros(c_ref.shape, F32)
    for r in range(nreq):
        aff = a_ref[:, r * T:(r + 1) * T]
        lo, hi = bounds[2 * r], bounds[2 * r + 1]
        gt = aff >= hi
        eq = (aff >= lo) & jnp.logical_not(gt)
        need = capf - jnp.sum(gt.astype(F32), axis=1, keepdims=True)
        pe = _dot(eq.astype(F32).astype(BF16), u)
        sel = gt | (eq & (pe <= need))
        ps = _dot(sel.astype(F32).astype(BF16), u)
        slot = jnp.where(sel, ps - 1.0, -1.0)
        s_ref[:, r * T:(r + 1) * T] = slot
        st_ref[r * T:(r + 1) * T, :] = slot.T
        for k in range(nblk):
            end = (k + 1) * TOKEN_BLOCK
            cnt = jnp.where(cl == r * nblk + k, ps[:, end - 1:end], cnt)
    c_ref[...] = cnt


def _topk(aff_t, tri, T, name):
    n = aff_t.shape[1]
    nreq = n // T
    ncnt = nreq * (T // TOKEN_BLOCK)
    kern = functools.partial(_topk_kernel, cap=T // 8, T=T, nreq=nreq)
    return pl.pallas_call(
        kern, grid=(1,),
        in_specs=[pl.BlockSpec((N_EXPERTS, n), lambda i: (0, 0)),
                  pl.BlockSpec((T, T), lambda i: (0, 0))],
        out_specs=[pl.BlockSpec((N_EXPERTS, n), lambda i: (0, 0)),
                   pl.BlockSpec((n, N_EXPERTS), lambda i: (0, 0)),
                   pl.BlockSpec((N_EXPERTS, ncnt), lambda i: (0, 0))],
        out_shape=[jax.ShapeDtypeStruct((N_EXPERTS, n), F32),
                   jax.ShapeDtypeStruct((n, N_EXPERTS), F32),
                   jax.ShapeDtypeStruct((N_EXPERTS, ncnt), F32)],
        compiler_params=_cparams(48), name=name)(aff_t, tri)


def _window_starts(start_ref, nblk):
    base = (pl.program_id(0) * nblk + pl.program_id(1)) * N_EXPERTS
    return [pl.multiple_of(start_ref[base + e], 16) for e in range(N_EXPERTS)]


def _gather_kernel(start_ref, ok_ref, h_ref, s_ref, a_ref, xg_ref, gt_ref, *, cap, W, nblk):
    @pl.when(pl.program_id(1) == 0)
    def _():
        xg_ref[...] = jnp.zeros_like(xg_ref)
        gt_ref[...] = jnp.zeros_like(gt_ref)

    h = h_ref[...]
    nch = 512

    def step(w, starts):
        j = lax.broadcasted_iota(jnp.int32, (w, TOKEN_BLOCK), 0).astype(F32)
        ohs = []
        for e in range(N_EXPERTS):
            rel = s_ref[e:e + 1, :]
            if starts is not None:
                rel = rel - starts[e].astype(F32)
            oh = rel == j
            rows = pl.ds(starts[e], w) if starts is not None else slice(0, w)
            gt_ref[e, rows, :] += jnp.sum(jnp.where(oh, a_ref[e:e + 1, :], 0.0), axis=1, keepdims=True)
            ohs.append(oh.astype(F32).astype(BF16))
        ohm = jnp.concatenate(ohs, axis=0)
        for c in range(0, D_MODEL, nch):
            res = _dot(ohm, h[:, c:c + nch])
            for e in range(N_EXPERTS):
                rows = pl.ds(starts[e], w) if starts is not None else slice(0, w)
                xg_ref[e, rows, c:c + nch] += res[e * w:(e + 1) * w].astype(BF16)

    if W == cap:
        step(cap, None)
    else:
        ok = ok_ref[pl.program_id(0)] > 0

        @pl.when(ok)
        def _():
            step(W, _window_starts(start_ref, nblk))

        @pl.when(jnp.logical_not(ok))
        def _():
            step(cap, None)


def _gather(h2, slot, aff_t, starts, ok, T, name):
    n = h2.shape[0]
    cap = T // 8
    nreq = n // T
    nblk = T // TOKEN_BLOCK
    kern = functools.partial(_gather_kernel, cap=cap, W=min(SLOT_WINDOW, cap), nblk=nblk)
    tok = lambda b, k, *_: (0, b * nblk + k)
    req = lambda b, k, *_: (0, b, 0)
    return pl.pallas_call(
        kern,
        grid_spec=pltpu.PrefetchScalarGridSpec(
            num_scalar_prefetch=2, grid=(nreq, nblk),
            in_specs=[pl.BlockSpec((TOKEN_BLOCK, D_MODEL), lambda b, k, *_: (b * nblk + k, 0)),
                      pl.BlockSpec((N_EXPERTS, TOKEN_BLOCK), tok),
                      pl.BlockSpec((N_EXPERTS, TOKEN_BLOCK), tok)],
            out_specs=[pl.BlockSpec((N_EXPERTS, cap, D_MODEL), req),
                       pl.BlockSpec((N_EXPERTS, cap, 1), req)]),
        out_shape=[jax.ShapeDtypeStruct((N_EXPERTS, nreq * cap, D_MODEL), BF16),
                   jax.ShapeDtypeStruct((N_EXPERTS, nreq * cap, 1), F32)],
        compiler_params=_cparams(56), name=name)(starts, ok, h2, slot, aff_t)


def _ffn_kernel(xc_ref, xl_ref, gc_ref, gl_ref, w1_ref, w3_ref, w2_ref, yc_ref, yl_ref, xbuf, acc, *, rc, nf, rsub):
    f = pl.program_id(2)

    @pl.when(f == 0)
    def _():
        xbuf[0:rc, :] = xc_ref[0]
        xbuf[rc:, :] = xl_ref[0]

    @pl.when(f == 0)
    def _():
        acc[...] = jnp.zeros_like(acc)

    w1 = w1_ref[0].astype(BF16)
    w3 = w3_ref[0].astype(BF16)
    w2 = w2_ref[0].astype(BF16)
    rows = xbuf.shape[0]
    for r0 in range(0, rows, rsub):
        x = xbuf[r0:r0 + rsub, :]
        hid = jax.nn.silu(_dot(x, w1)) * _dot(x, w3)
        acc[r0:r0 + rsub, :] += _dot(hid.astype(BF16), w2)

    @pl.when(f == nf - 1)
    def _():
        yc_ref[0] = (acc[0:rc, :] * gc_ref[0]).astype(BF16)
        yl_ref[0] = (acc[rc:, :] * gl_ref[0]).astype(BF16)


def _ffn(xg_c, xg_l, gt_c, gt_l, w1, w3, w2):
    rc = xg_c.shape[1] // 2
    rl = xg_l.shape[1] // 2
    bf = 256
    nf = D_MODEL // bf
    kern = functools.partial(_ffn_kernel, rc=rc, nf=nf, rsub=(rc + rl) // 2)
    rowsp = lambda r, w: pl.BlockSpec((1, r, w), lambda e, h, f: (e, h, 0))
    return pl.pallas_call(
        kern, grid=(N_EXPERTS, 2, nf),
        in_specs=[rowsp(rc, D_MODEL), rowsp(rl, D_MODEL), rowsp(rc, 1), rowsp(rl, 1),
                  pl.BlockSpec((1, D_MODEL, bf), lambda e, h, f: (e, 0, f)),
                  pl.BlockSpec((1, D_MODEL, bf), lambda e, h, f: (e, 0, f)),
                  pl.BlockSpec((1, bf, D_MODEL), lambda e, h, f: (e, f, 0))],
        out_specs=[rowsp(rc, D_MODEL), rowsp(rl, D_MODEL)],
        out_shape=[jax.ShapeDtypeStruct(xg_c.shape, BF16), jax.ShapeDtypeStruct(xg_l.shape, BF16)],
        scratch_shapes=[pltpu.VMEM((rc + rl, D_MODEL), BF16), pltpu.VMEM((rc + rl, D_MODEL), F32)],
        compiler_params=_cparams(56), name="ffn")(xg_c, xg_l, gt_c, gt_l, w1, w3, w2)


def _expand_consts(w):
    lane = jnp.arange(N_EXPERTS * w)
    expand = (jnp.arange(N_EXPERTS)[:, None] == (lane // w)[None, :]).astype(BF16)
    return expand, (lane % w).astype(F32)[None, :]


def _combine_kernel(start_ref, ok_ref, x_ref, y_ref, st_ref, sv_ref, mod_ref, fg_ref, exw_ref, jlw_ref,
                    exd_ref, jld_ref, o_ref, stage, *, cap, W, nblk):
    st = st_ref[...]

    def finish(moe):
        x2 = x_ref[...] + mod_ref[0, 5:6, :] * moe
        o_ref[...] = _rms(x2) * fg_ref[...]

    def dense():
        rep = _dot(st.astype(BF16), exd_ref[...])
        oh = (rep == jld_ref[...]).astype(F32).astype(BF16)
        finish(_dot(oh, y_ref[...].reshape(N_EXPERTS * cap, D_MODEL)))

    if W == cap:
        dense()
        return
    ok = ok_ref[pl.program_id(0)] > 0

    @pl.when(ok)
    def _():
        starts = _window_starts(start_ref, nblk)
        for e in range(N_EXPERTS):
            stage[e * W:(e + 1) * W, :] = y_ref[e, pl.ds(starts[e], W), :]
        rep = _dot((st - sv_ref[0]).astype(BF16), exw_ref[...])
        oh = (rep == jlw_ref[...]).astype(F32).astype(BF16)
        finish(_dot(oh, stage[...]))

    @pl.when(jnp.logical_not(ok))
    def _():
        dense()


def _combine(x1, ybuf, slot_t, starts, ok, starts_f, mod, final_g, T, name):
    n = x1.shape[0]
    cap = T // 8
    nblk = T // TOKEN_BLOCK
    W = min(SLOT_WINDOW, cap)
    multi = mod.shape[0] > 1
    exw, jlw = _expand_consts(W)
    exd, jld = _expand_consts(cap)
    kern = functools.partial(_combine_kernel, cap=cap, W=W, nblk=nblk)
    tok = lambda b, k, *_: (b * nblk + k, 0)
    const = lambda b, k, *_: (0, 0)
    return pl.pallas_call(
        kern,
        grid_spec=pltpu.PrefetchScalarGridSpec(
            num_scalar_prefetch=2, grid=(n // T, nblk),
            in_specs=[pl.BlockSpec((TOKEN_BLOCK, D_MODEL), tok),
                      pl.BlockSpec((N_EXPERTS, cap, D_MODEL), lambda b, k, *_: (0, b, 0)),
                      pl.BlockSpec((TOKEN_BLOCK, N_EXPERTS), tok),
                      pl.BlockSpec((1, 1, N_EXPERTS), lambda b, k, *_: (b * nblk + k, 0, 0)),
                      pl.BlockSpec((1, N_MOD, D_MODEL), lambda b, k, *_: (b if multi else 0, 0, 0)),
                      pl.BlockSpec((1, D_MODEL), const),
                      pl.BlockSpec(exw.shape, const), pl.BlockSpec(jlw.shape, const),
                      pl.BlockSpec(exd.shape, const), pl.BlockSpec(jld.shape, const)],
            out_specs=pl.BlockSpec((TOKEN_BLOCK, D_MODEL), tok),
            scratch_shapes=[pltpu.VMEM((N_EXPERTS * W, D_MODEL), BF16)]),
        out_shape=jax.ShapeDtypeStruct((n, D_MODEL), F32),
        compiler_params=_cparams(56), name=name)(starts, ok, x1, ybuf, slot_t, starts_f, mod, final_g,
                                                 exw, jlw, exd, jld)


def _window_tables(cnt, T):
    cap = T // 8
    nblk = T // TOKEN_BLOCK
    nreq = cnt.shape[1] // nblk
    W = min(SLOT_WINDOW, cap)
    end = cnt.astype(jnp.int32).reshape(N_EXPERTS, nreq, nblk)
    first = jnp.concatenate([jnp.zeros_like(end[..., :1]), end[..., :-1]], axis=-1)
    start = jnp.minimum((first // 16) * 16, cap - W)
    ok = jnp.all(end - start <= W, axis=(0, 2)).astype(jnp.int32)
    start = start.transpose(1, 2, 0)
    return start.reshape(-1), ok, start.astype(F32).reshape(nreq * nblk, 1, N_EXPERTS)


def _blockdiag_pairs(w):
    w = w.reshape(2, RG_SLABS, 2, 64, 64)
    bd = jnp.einsum('dshij,hk->dshikj', w, jnp.eye(2, dtype=w.dtype))
    return bd.reshape(2, RG_SLABS, LANES, LANES).transpose(1, 0, 2, 3)


def kernel(x_prompt, x_sample, state_rglru, state_s5_re, state_s5_im, c, c_ctx, norm1_g, norm2_g, w_ada, b_ada, w_in, rg_conv_w, rg_conv_b, rg_wa, rg_ba, rg_wx, rg_bx, rg_lam, s5_a_re, s5_a_im, s5_log_dt, s5_b_re, s5_b_im, s5_c_re, s5_c_im, s5_d, s5_w_glu, s5_b_glu, gnorm_rg, gnorm_s5, w_out, w_router, w1, w3, w2, final_g):
    l = 0
    Bp, Tp, _ = x_prompt.shape
    Bs, Ts, _ = x_sample.shape

    c16 = jnp.concatenate([c, c_ctx[None], jnp.zeros((16 - Bs - 1, D_MODEL), F32)], axis=0)
    mod = _adaln(c16, w_ada[l], b_ada[l]).reshape(16, N_MOD, D_MODEL)
    mod_lat = mod[:Bs]
    mod_ctx1 = mod[Bs:Bs + 1]
    mod_ctx = jnp.broadcast_to(mod_ctx1, (Bp, N_MOD, D_MODEL))

    w_in_b = w_in[l].astype(BF16)
    w_out_b = w_out[l].astype(BF16)
    w_glu_b = s5_w_glu[l].astype(BF16)
    row = lambda v: v.reshape(1, -1)

    wg = jnp.concatenate([_blockdiag_pairs(rg_wa[l]), _blockdiag_pairs(rg_wx[l])], axis=-1).astype(BF16)
    bg = jnp.concatenate([rg_ba[l].reshape(2, RG_SLABS, LANES), rg_bx[l].reshape(2, RG_SLABS, LANES)],
                         axis=-1).transpose(1, 0, 2).reshape(RG_SLABS, 2, 1, 2 * LANES)
    kmat, pmat, qmat, lam16 = _s5prep(s5_a_re[l], s5_a_im[l], s5_log_dt[l], s5_b_re[l], s5_b_im[l],
                                       s5_c_re[l], s5_c_im[l])

    def mixer(x, modp, rg_h0, s5_h0, lat, tag):
        B, T, _ = x.shape
        R = T * B
        proj = _inproj(x, modp, row(norm1_g[l]), w_in_b, "inproj_" + tag)
        y_rg, fin_rg = _rglru(proj, rg_conv_w[l], row(rg_conv_b[l]), wg, bg, rg_lam[l], rg_h0, B, T,
                              "rglru_" + tag)
        if lat:
            A, Bq = T // GRID_W // CHUNK, GRID_W * B
        else:
            A, Bq = T // CHUNK, B
        y_ssm, fin_s5 = _s5(proj.reshape(N_SLAB, A, CHUNK, Bq, LANES), kmat, pmat, qmat, lam16, s5_h0,
                            A, Bq, B, lat, "s5_" + tag)
        x1 = _post(y_rg, y_ssm.reshape(S5_SLABS, R, LANES), proj, x, modp, w_out_b, w_glu_b,
                   row(s5_b_glu[l]), row(s5_d[l]), row(gnorm_rg[l]), row(gnorm_s5[l]), "post_" + tag)
        return x1.reshape(R, D_MODEL), fin_rg, fin_s5

    z_rg = jnp.zeros((2, Bp, D_RG), F32)
    z_s5 = jnp.zeros((S5_GROUPS, Bp, 256), F32)
    x1_c, fin_rg, fin_s5 = mixer(x_prompt, mod_ctx, z_rg, z_s5, False, "ctx")
    rg_h0 = state_rglru[:, l].transpose(1, 0, 2)
    s5_h0 = jnp.stack([state_s5_re[:, l, 0], state_s5_re[:, l, 1], state_s5_im[:, l, 0], state_s5_im[:, l, 1]],
                      axis=0)
    s5_h0 = s5_h0.transpose(2, 1, 0, 3).reshape(S5_GROUPS, Bs, 256)
    x1_s, _, _ = mixer(x_sample, mod_lat, rg_h0, s5_h0, True, "lat")

    w_router_t = w_router[l].T
    g2 = row(norm2_g[l])

    def route(x1, modp, T, tag):
        h2, aff_t = _premoe(x1, modp, g2, w_router_t, T, "premoe_" + tag)
        tri = jnp.triu(jnp.ones((T, T), BF16))
        slot, slot_t, cnt = _topk(aff_t, tri, T, "topk_" + tag)
        win = _window_tables(cnt, T)
        xg, gates = _gather(h2, slot, aff_t, win[0], win[1], T, "gather_" + tag)
        return xg, gates, slot_t, win

    xg_c, gt_c, st_c, win_c = route(x1_c, mod_ctx1, Tp, "ctx")
    xg_s, gt_s, st_s, win_s = route(x1_s, mod_lat, Ts, "lat")
    y_c, y_s = _ffn(xg_c, xg_s, gt_c, gt_s, w1[l], w3[l], w2[l])
    fg = row(final_g)
    y_prompt = _combine(x1_c, y_c, st_c, *win_c, mod_ctx1, fg, Tp, "combine_ctx").reshape(Bp, Tp, D_MODEL)
    y_sample = _combine(x1_s, y_s, st_s, *win_s, mod_lat, fg, Ts, "combine_lat").reshape(Bs, Ts, D_MODEL)

    new_rg = fin_rg.transpose(1, 0, 2)[:, None]
    f4 = fin_s5.reshape(S5_GROUPS, Bp, 4, 64).transpose(1, 2, 0, 3)
    new_re = f4[:, None, 0:2]
    new_im = f4[:, None, 2:4]
    return (y_prompt, y_sample, new_rg, new_re, new_im)
```

```python
import functools

import jax
import jax.numpy as jnp
from jax import lax
from jax.experimental import pallas as pl
from jax.experimental.pallas import tpu as pltpu

F32 = jnp.float32
BF16 = jnp.bfloat16
HIGHEST = lax.Precision.HIGHEST

LANES = 128
D_MODEL = 2048
D_RG = 1536
D_S5 = 512
D_IN = 2 * D_RG + D_S5
N_SLAB = D_IN // LANES
RG_SLABS = D_RG // LANES
S5_SLABS = D_S5 // LANES
S5_GROUPS = 32
GROUPS_PER_SLAB = 8
CHUNK = 16
N_EXPERTS = 16
N_MOD = 6
EPS = 1e-6
RG_C = 8.0
GRID_W = 64
TOKEN_BLOCK = 256
SLOT_WINDOW = 80
MIB = 1024 * 1024


def _cparams(vmem_mib):
    return pltpu.CompilerParams(vmem_limit_bytes=vmem_mib * MIB)


def _dot(a, b):
    return jnp.dot(a, b, preferred_element_type=F32)


def _rms(x):
    return x * lax.rsqrt(jnp.mean(x * x, axis=-1, keepdims=True) + EPS)


def _adaln_kernel(c_ref, w_ref, b_ref, o_ref):
    s = jax.nn.silu(c_ref[...]).astype(BF16)
    o_ref[...] = _dot(s, w_ref[...].astype(BF16)) + b_ref[...]


def _adaln(c16, w_ada, b_ada):
    n = w_ada.shape[1]
    bn = 1024
    return pl.pallas_call(
        _adaln_kernel, grid=(n // bn,),
        in_specs=[pl.BlockSpec((16, D_MODEL), lambda j: (0, 0)),
                  pl.BlockSpec((D_MODEL, bn), lambda j: (0, j)),
                  pl.BlockSpec((1, bn), lambda j: (0, j))],
        out_specs=pl.BlockSpec((16, bn), lambda j: (0, j)),
        out_shape=jax.ShapeDtypeStruct((16, n), F32),
        compiler_params=_cparams(40), name="adaln")(c16, w_ada, b_ada.reshape(1, n))


def _inproj_kernel(x_ref, mod_ref, g_ref, w_ref, o_ref, *, B, tT, nch):
    x = x_ref[...]
    shift = mod_ref[:, 0:1, :]
    scale = mod_ref[:, 1:2, :]
    h = _rms(x) * g_ref[...] * (1.0 + scale) + shift
    hb = h.reshape(B * tT, D_MODEL).astype(BF16)
    for c0 in range(0, D_IN, nch):
        p = _dot(hb, w_ref[:, c0:c0 + nch])
        for b in range(B):
            for j in range(nch // LANES):
                o_ref[c0 // LANES + j, pl.ds(b, tT, stride=B), :] = (
                    p[b * tT:(b + 1) * tT, j * LANES:(j + 1) * LANES])


def _inproj(x, mod, g1, w_in_bf16, name):
    B, T, _ = x.shape
    tT = 512 // B
    kern = functools.partial(_inproj_kernel, B=B, tT=tT, nch=512)
    return pl.pallas_call(
        kern, grid=(T // tT,),
        in_specs=[pl.BlockSpec((B, tT, D_MODEL), lambda i: (0, i, 0)),
                  pl.BlockSpec((B, N_MOD, D_MODEL), lambda i: (0, 0, 0)),
                  pl.BlockSpec((1, D_MODEL), lambda i: (0, 0)),
                  pl.BlockSpec((D_MODEL, D_IN), lambda i: (0, 0), pipeline_mode=pl.Buffered(1))],
        out_specs=pl.BlockSpec((N_SLAB, tT * B, LANES), lambda i: (0, i, 0)),
        out_shape=jax.ShapeDtypeStruct((N_SLAB, T * B, LANES), F32),
        compiler_params=_cparams(56), name=name)(x, mod, g1, w_in_bf16)


def _rglru_kernel(xb_ref, gt_ref, cw_ref, cb_ref, wg_ref, bg_ref, lam_ref, h0_ref, y_ref, fin_ref,
                  *, B, T, CT):
    RC = CT * B
    nc = T // CT
    cw = [cw_ref[k:k + 1, :] for k in range(4)]
    cb = cb_ref[...]
    nsp = [(-RG_C) * jax.nn.softplus(-lam_ref[d:d + 1, :]) for d in range(2)]

    def chunk(n, d):
        t0 = n * CT
        row0 = pl.multiple_of(t0 * B, RC)
        cur = xb_ref[0, pl.ds(row0, RC), :]
        p0 = pl.multiple_of(jnp.maximum(t0 - 1, 0) * B, B)
        prev = xb_ref[0, pl.ds(p0, B), :] * jnp.where(t0 > 0, 1.0, 0.0)
        n0 = pl.multiple_of(jnp.minimum(t0 + CT, T - 2) * B, B)
        nxt = xb_ref[0, pl.ds(n0, 2 * B), :] * jnp.where(t0 + CT < T, 1.0, 0.0)
        ext = jnp.concatenate([prev, cur, nxt], axis=0)
        xc = (cw[0] * ext[0:RC] + cw[1] * ext[B:B + RC] + cw[2] * ext[2 * B:2 * B + RC]
              + cw[3] * ext[3 * B:3 * B + RC] + cb)
        g = _dot(xc.astype(BF16), wg_ref[0, d]) + bg_ref[0, d]
        th = jnp.tanh(0.5 * g)
        r = 0.5 + 0.5 * th[:, :LANES]
        i = 0.5 + 0.5 * th[:, LANES:]
        a = jnp.exp(r * nsp[d])
        om = 1.0 - a * a
        bb = jnp.where(om > 0.0, om * lax.rsqrt(om), 0.0) * (i * xc)
        return a, bb, row0

    def make_body(finish):
        def body(n, carry):
            hf, hb = carry
            af, bf, rf = chunk(n, 0)
            ab, bk, rb = chunk(nc - 1 - n, 1)
            if finish:
                glf = jax.nn.gelu(gt_ref[0, pl.ds(rf, RC), :])
                glb = jax.nn.gelu(gt_ref[0, pl.ds(rb, RC), :])
                of = y_ref[0, pl.ds(rf, RC), :]
                ob = y_ref[0, pl.ds(rb, RC), :]
            for t in range(CT):
                u = CT - 1 - t
                hf = af[t * B:(t + 1) * B] * hf + bf[t * B:(t + 1) * B]
                hb = ab[u * B:(u + 1) * B] * hb + bk[u * B:(u + 1) * B]
                rows_f = pl.ds(rf + t * B, B)
                rows_b = pl.ds(rb + u * B, B)
                if finish:
                    y_ref[0, rows_f, :] = (of[t * B:(t + 1) * B] + hf) * glf[t * B:(t + 1) * B]
                    y_ref[0, rows_b, :] = (ob[u * B:(u + 1) * B] + hb) * glb[u * B:(u + 1) * B]
                else:
                    y_ref[0, rows_f, :] = hf
                    y_ref[0, rows_b, :] = hb
            return hf, hb
        return body

    carry = lax.fori_loop(0, nc // 2, make_body(False), (h0_ref[0], h0_ref[1]))
    hf, hb = lax.fori_loop(nc // 2, nc, make_body(True), carry)
    fin_ref[0] = hf
    fin_ref[1] = hb


def _rglru(proj, cw, cb, wg, bg, lam, h0, B, T, name):
    R = T * B
    kern = functools.partial(_rglru_kernel, B=B, T=T, CT=512 // B)
    return pl.pallas_call(
        kern, grid=(RG_SLABS,),
        in_specs=[pl.BlockSpec((1, R, LANES), lambda s: (s, 0, 0)),
                  pl.BlockSpec((1, R, LANES), lambda s: (s + RG_SLABS, 0, 0)),
                  pl.BlockSpec((4, LANES), lambda s: (0, s)),
                  pl.BlockSpec((1, LANES), lambda s: (0, s)),
                  pl.BlockSpec((1, 2, LANES, 2 * LANES), lambda s: (s, 0, 0, 0)),
                  pl.BlockSpec((1, 2, 1, 2 * LANES), lambda s: (s, 0, 0, 0)),
                  pl.BlockSpec((2, LANES), lambda s: (0, s)),
                  pl.BlockSpec((2, B, LANES), lambda s: (0, 0, s))],
        out_specs=[pl.BlockSpec((1, R, LANES), lambda s: (s, 0, 0)),
                   pl.BlockSpec((2, B, LANES), lambda s: (0, 0, s))],
        out_shape=[jax.ShapeDtypeStruct((RG_SLABS, R, LANES), F32),
                   jax.ShapeDtypeStruct((2, B, D_RG), F32)],
        compiler_params=_cparams(58), name=name)(proj, proj, cw, cb, wg, bg, lam, h0)


def _cpow(pr, pi, n):
    wr = jnp.ones(n.shape, F32)
    wi = jnp.zeros(n.shape, F32)
    for k in range(4):
        bit = (n & (1 << k)) != 0
        wr, wi = jnp.where(bit, wr * pr - wi * pi, wr), jnp.where(bit, wr * pi + wi * pr, wi)
        pr, pi = pr * pr - pi * pi, 2.0 * pr * pi
    return wr, wi


def _s5prep_kernel(ar_ref, ai_ref, ldt_ref, arc_ref, aic_ref, ldtc_ref, btr_ref, bti_ref, ctr_ref, cti_ref,
                   k_ref, p_ref, q_ref, l_ref):
    ar = ar_ref[0]
    ai = ai_ref[0]
    dt = jnp.exp(ldt_ref[0])
    mag = jnp.exp(dt * ar)
    lr = mag * jnp.cos(dt * ai)
    li = mag * jnp.sin(dt * ai)
    den = ar * ar + ai * ai
    nre = lr - 1.0
    qre = (nre * ar + li * ai) / den
    qim = (li * ar - nre * ai) / den
    btr = btr_ref[0]
    bti = bti_ref[0]
    bbr = qre * btr - qim * bti
    bbi = qre * bti + qim * btr
    rowblk = lax.broadcasted_iota(jnp.int32, (256, LANES), 0) // CHUNK
    isf = lax.broadcasted_iota(jnp.int32, (256, LANES), 1) < 64

    def powers(n):
        return _cpow(lr, li, n)

    ctr = ctr_ref[0]
    cti = cti_ref[0]

    wr, wi = powers(rowblk)
    zr = wr * bbr - wi * bbi
    zi = wr * bbi + wi * bbr
    zero = jnp.zeros_like(zr)

    def mm(a, b):
        return jnp.dot(a, b, precision=HIGHEST, preferred_element_type=F32)

    mall_f = mm(jnp.where(isf, zr, zero), ctr) - mm(jnp.where(isf, zi, zero), cti)
    mall_b = mm(jnp.where(isf, zero, zr), ctr) - mm(jnp.where(isf, zero, zi), cti)
    blocks = [mall_f[(15 - q) * CHUNK:(16 - q) * CHUNK] for q in range(15)]
    blocks.append(mall_f[0:CHUNK] + mall_b[0:CHUNK])
    blocks += [mall_b[l * CHUNK:(l + 1) * CHUNK] for l in range(1, 16)]
    blocks.append(jnp.zeros((CHUNK, 256), F32))
    gw = jnp.concatenate(blocks, axis=0)
    colblk = lax.broadcasted_iota(jnp.int32, (256, 256), 1) // CHUNK
    kmat = jnp.zeros((256, 256), F32)
    for tp in range(CHUNK):
        kmat = jnp.where(colblk == tp, gw[(15 - tp) * CHUNK:(15 - tp) * CHUNK + 256], kmat)
    k_ref[0] = kmat.astype(BF16)

    wr, wi = powers(jnp.where(isf, 15 - rowblk, rowblk))
    p_ref[0] = jnp.concatenate([wr * bbr - wi * bbi, wr * bbi + wi * bbr], axis=1).astype(BF16)

    arc = arc_ref[0]
    aic = aic_ref[0]
    dtc = jnp.exp(ldtc_ref[0])
    laneblk = lax.broadcasted_iota(jnp.int32, (LANES, 256), 1) // CHUNK
    rowf = lax.broadcasted_iota(jnp.int32, (LANES, 256), 0) < 64
    magc = jnp.exp(dtc * arc)
    lrc = magc * jnp.cos(dtc * aic)
    lic = magc * jnp.sin(dtc * aic)
    vr, vi = _cpow(lrc, lic, jnp.where(rowf, laneblk, 15 - laneblk))
    wr = vr * lrc - vi * lic
    wi = vr * lic + vi * lrc
    q_ref[0] = jnp.concatenate([ctr * wr - cti * wi, -(ctr * wi + cti * wr)], axis=0).astype(BF16)

    pr, pi = lr, li
    for _ in range(4):
        pr, pi = pr * pr - pi * pi, 2.0 * pr * pi
    l_ref[0] = jnp.concatenate([pr, pi], axis=1)


def _s5prep(a_re, a_im, log_dt, b_re, b_im, c_re, c_im):
    G = S5_GROUPS

    def rows(x):
        return jnp.concatenate([x[0], x[1]], axis=-1).reshape(G, 1, LANES)

    ldt = jnp.broadcast_to(log_dt[:, :, None], (2, G, 64))
    ar, ai, ld = rows(a_re), rows(a_im), rows(ldt)
    arc, aic, ldc = (v.reshape(G, LANES, 1) for v in (ar, ai, ld))

    def btile(b):
        bt = jnp.tile(jnp.swapaxes(b, 2, 3), (1, 1, CHUNK, 1))
        return jnp.concatenate([bt[0], bt[1]], axis=-1)

    def ctile(c):
        ct = jnp.tile(jnp.swapaxes(c, 2, 3), (1, 1, 1, CHUNK))
        return jnp.concatenate([ct[0], ct[1]], axis=1)

    vec = pl.BlockSpec((1, 1, LANES), lambda g: (g, 0, 0))
    col = pl.BlockSpec((1, LANES, 1), lambda g: (g, 0, 0))
    bsp = pl.BlockSpec((1, 256, LANES), lambda g: (g, 0, 0))
    csp = pl.BlockSpec((1, LANES, 256), lambda g: (g, 0, 0))
    msp = pl.BlockSpec((1, 256, 256), lambda g: (g, 0, 0))
    return pl.pallas_call(
        _s5prep_kernel, grid=(G,),
        in_specs=[vec, vec, vec, col, col, col, bsp, bsp, csp, csp],
        out_specs=[msp, msp, msp, pl.BlockSpec((1, 1, 256), lambda g: (g, 0, 0))],
        out_shape=[jax.ShapeDtypeStruct((G, 256, 256), BF16)] * 3 + [jax.ShapeDtypeStruct((G, 1, 256), F32)],
        compiler_params=_cparams(32), name="s5prep")(
            ar, ai, ld, arc, aic, ldc, btile(b_re), btile(b_im), ctile(c_re), ctile(c_im))


def _seg_transpose8(vs, seg):
    vs = list(vs)
    for d in (4, 2, 1):
        hi = (seg & d) != 0
        new = list(vs)
        for i in range(8):
            if i & d:
                continue
            j = i + d
            new[i] = jnp.where(hi, pltpu.roll(vs[j], d * CHUNK, axis=1), vs[i])
            new[j] = jnp.where(hi, vs[j], pltpu.roll(vs[i], LANES - d * CHUNK, axis=1))
        vs = new
    return vs


def _s5_kernel(u_ref, k_ref, p_ref, q_ref, l_ref, h0_ref, y_ref, fin_ref, ebuf, ybuf, *, A, Bq, B, lat):
    RP = A * Bq
    nchunk = RP // B
    G = GROUPS_PER_SLAB
    nr = 256
    qn = min(Bq, nr)
    tiles = [(r0, [((r0 + i * qn) // Bq, (r0 + i * qn) % Bq) for i in range(nr // qn)])
             for r0 in range(0, RP, nr)]
    lane_blk = lax.broadcasted_iota(jnp.int32, (nr, LANES), 1) // CHUNK

    for r0, pieces in tiles:
        xs = [jnp.concatenate([u_ref[0, a, tau, q0:q0 + qn, :] for a, q0 in pieces], axis=0)
              for tau in range(CHUNK)]
        halves = [_seg_transpose8(xs[8 * h:8 * h + 8], lane_blk) for h in range(2)]
        for g in range(G):
            ug = jnp.concatenate([halves[0][g], halves[1][g]], axis=1).astype(BF16)
            ybuf[g, r0:r0 + nr, :] = _dot(ug, k_ref[g])
            ebuf[g, r0:r0 + nr, :] = _dot(ug, p_ref[g])

    def off(m):
        if lat:
            return ((m & 1) * GRID_W + (m >> 1)) * B
        return m * B

    fmask = (lax.broadcasted_iota(jnp.int32, (B, 2 * LANES), 1) % LANES) < 64
    fmask1 = lax.broadcasted_iota(jnp.int32, (B, LANES), 1) < 64
    lam = [l_ref[g] for g in range(G)]

    def body(i, carry):
        of = pl.multiple_of(off(i), B)
        ob = pl.multiple_of(off(nchunk - 1 - i), B)
        out = []
        for g in range(G):
            re, im = carry[2 * g], carry[2 * g + 1]
            ef = ebuf[g, pl.ds(of, B), :]
            eb = ebuf[g, pl.ds(ob, B), :]
            s = jnp.concatenate([re, im], axis=1)
            ebuf[g, pl.ds(of, B), :] = jnp.where(fmask, s, ef)
            ebuf[g, pl.ds(ob, B), :] = jnp.where(fmask, eb, s)
            lr = lam[g][:, :LANES]
            li = lam[g][:, LANES:]
            e_re = jnp.where(fmask1, ef[:, :LANES], eb[:, :LANES])
            e_im = jnp.where(fmask1, ef[:, LANES:], eb[:, LANES:])
            out.append(lr * re - li * im + e_re)
            out.append(lr * im + li * re + e_im)
        return tuple(out)

    init = []
    for g in range(G):
        h0 = h0_ref[g]
        init += [h0[:, :LANES], h0[:, LANES:]]
    fin = lax.fori_loop(0, nchunk, body, tuple(init))
    for g in range(G):
        fin_ref[g] = jnp.concatenate([fin[2 * g], fin[2 * g + 1]], axis=1)

    for g in range(G):
        ybuf[g] = ybuf[g] + _dot(ebuf[g].astype(BF16), q_ref[g])
    for r0, pieces in tiles:
        for h in range(2):
            yt = _seg_transpose8([ybuf[g, r0:r0 + nr, h * LANES:(h + 1) * LANES] for g in range(G)], lane_blk)
            for tp in range(8):
                for k, (a, q0) in enumerate(pieces):
                    y_ref[0, a, 8 * h + tp, q0:q0 + qn, :] = yt[tp][k * qn:(k + 1) * qn]


def _s5(proj5, kmat, pmat, qmat, lam16, h0, A, Bq, B, lat, name):
    RP = A * Bq
    G = GROUPS_PER_SLAB
    kern = functools.partial(_s5_kernel, A=A, Bq=Bq, B=B, lat=lat)
    msp = pl.BlockSpec((G, 256, 256), lambda s: (s, 0, 0))
    return pl.pallas_call(
        kern, grid=(S5_SLABS,),
        in_specs=[pl.BlockSpec((1, A, CHUNK, Bq, LANES), lambda s: (s + 2 * RG_SLABS, 0, 0, 0, 0)),
                  msp, msp, msp,
                  pl.BlockSpec((G, 1, 256), lambda s: (s, 0, 0)),
                  pl.BlockSpec((G, B, 256), lambda s: (s, 0, 0))],
        out_specs=[pl.BlockSpec((1, A, CHUNK, Bq, LANES), lambda s: (s, 0, 0, 0, 0)),
                   pl.BlockSpec((G, B, 256), lambda s: (s, 0, 0))],
        out_shape=[jax.ShapeDtypeStruct((S5_SLABS, A, CHUNK, Bq, LANES), F32),
                   jax.ShapeDtypeStruct((S5_GROUPS, B, 256), F32)],
        scratch_shapes=[pltpu.VMEM((G, RP, 256), F32), pltpu.VMEM((G, RP, 256), F32)],
        compiler_params=_cparams(58), name=name)(proj5, kmat, pmat, qmat, lam16, h0)


def _post_kernel(yr_ref, ys_ref, u_ref, x_ref, mod_ref, wo_ref, wg_ref, bgl_ref, d_ref, grg_ref, gs5_ref,
                 o_ref, ybuf, *, B, tT):
    yr = jnp.concatenate([yr_ref[s] for s in range(RG_SLABS)], axis=1)
    nr = (_rms(yr) * grg_ref[...]).astype(BF16)
    ys = jnp.concatenate([ys_ref[s] for s in range(S5_SLABS)], axis=1)
    u = jnp.concatenate([u_ref[s] for s in range(S5_SLABS)], axis=1)
    ys = jax.nn.gelu(ys + d_ref[...] * u)
    ys = ys * jax.nn.sigmoid(_dot(ys.astype(BF16), wg_ref[...]) + bgl_ref[...])
    ns = (_rms(ys) * gs5_ref[...]).astype(BF16)
    y = _dot(nr, wo_ref[0:D_RG, :]) + _dot(ns, wo_ref[D_RG:D_MODEL, :])
    nj = D_MODEL // LANES
    for j in range(nj):
        ybuf[j] = y[:, j * LANES:(j + 1) * LANES]
    for b in range(B):
        yb = jnp.concatenate([ybuf[j, pl.ds(b, tT, stride=B), :] for j in range(nj)], axis=1)
        o_ref[b] = x_ref[b] + mod_ref[b, 2:3, :] * yb


def _post(y_rg, y_ssm, proj, x, mod, w_out_bf16, w_glu_bf16, b_glu, s5_d, g_rg, g_s5, name):
    B, T, _ = x.shape
    tT = 512 // B
    rows = tT * B
    kern = functools.partial(_post_kernel, B=B, tT=tT)
    const = lambda i: (0, 0)
    return pl.pallas_call(
        kern, grid=(T // tT,),
        in_specs=[pl.BlockSpec((RG_SLABS, rows, LANES), lambda i: (0, i, 0)),
                  pl.BlockSpec((S5_SLABS, rows, LANES), lambda i: (0, i, 0)),
                  pl.BlockSpec((S5_SLABS, rows, LANES), lambda i: (2 * RG_SLABS // S5_SLABS, i, 0)),
                  pl.BlockSpec((B, tT, D_MODEL), lambda i: (0, i, 0)),
                  pl.BlockSpec((B, N_MOD, D_MODEL), lambda i: (0, 0, 0)),
                  pl.BlockSpec((D_MODEL, D_MODEL), const, pipeline_mode=pl.Buffered(1)),
                  pl.BlockSpec((D_S5, D_S5), const),
                  pl.BlockSpec((1, D_S5), const),
                  pl.BlockSpec((1, D_S5), const),
                  pl.BlockSpec((1, D_RG), const),
                  pl.BlockSpec((1, D_S5), const)],
        out_specs=pl.BlockSpec((B, tT, D_MODEL), lambda i: (0, i, 0)),
        out_shape=jax.ShapeDtypeStruct((B, T, D_MODEL), F32),
        scratch_shapes=[pltpu.VMEM((D_MODEL // LANES, rows, LANES), F32)],
        compiler_params=_cparams(56), name=name)(
            y_rg, y_ssm, proj, x, mod, w_out_bf16, w_glu_bf16, b_glu, s5_d, g_rg, g_s5)


def _premoe_kernel(x_ref, mod_ref, g_ref, wr_ref, h_ref, a_ref):
    h = _rms(x_ref[...]) * g_ref[...] * (1.0 + mod_ref[0, 4:5, :]) + mod_ref[0, 3:4, :]
    hhi = h.astype(BF16)
    h_ref[...] = hhi
    hlo = (h - hhi.astype(F32)).astype(BF16)
    w = wr_ref[...]
    whi = w.astype(BF16)
    wlo = (w - whi.astype(F32)).astype(BF16)
    nt = lambda a, b: lax.dot_general(a, b, (((1,), (1,)), ((), ())), preferred_element_type=F32)
    lg = nt(whi, hhi) + nt(wlo, hhi) + nt(whi, hlo)
    e = jnp.exp(lg - jnp.max(lg, axis=0, keepdims=True))
    a_ref[...] = e / jnp.sum(e, axis=0, keepdims=True)


def _premoe(x1, mod, g2, w_router_t, T, name):
    n = x1.shape[0]
    rows = 512
    per_req = T // rows if mod.shape[0] > 1 else n
    return pl.pallas_call(
        _premoe_kernel, grid=(n // rows,),
        in_specs=[pl.BlockSpec((rows, D_MODEL), lambda i: (i, 0)),
                  pl.BlockSpec((1, N_MOD, D_MODEL), lambda i: (i // per_req, 0, 0)),
                  pl.BlockSpec((1, D_MODEL), lambda i: (0, 0)),
                  pl.BlockSpec((N_EXPERTS, D_MODEL), lambda i: (0, 0))],
        out_specs=[pl.BlockSpec((rows, D_MODEL), lambda i: (i, 0)),
                   pl.BlockSpec((N_EXPERTS, rows), lambda i: (0, i))],
        out_shape=[jax.ShapeDtypeStruct((n, D_MODEL), BF16),
                   jax.ShapeDtypeStruct((N_EXPERTS, n), F32)],
        compiler_params=_cparams(40), name=name)(x1, mod, g2, w_router_t)


def _topk_kernel(a_ref, s_ref, st_ref, c_ref, *, cap, T, nreq):
    capf = float(cap)

    def count_ge(r, v):
        return jnp.sum((a_ref[:, r * T:(r + 1) * T] >= v).astype(F32), axis=1, keepdims=True)

    def bit_step(i, thrs):
        bit = jnp.int32(1) << (30 - i)
        out = []
        for r in range(nreq):
            cand = thrs[r] | bit
            ok = count_ge(r, pltpu.bitcast(cand, F32)) >= capf
            out.append(jnp.where(ok, cand, thrs[r]))
        return tuple(out)

    zero = jnp.zeros((N_EXPERTS, 1), jnp.int32)
    thrs = lax.fori_loop(0, 31, bit_step, (zero,) * nreq)

    def mid_step(i, c):
        out = []
        for r in range(nreq):
            lo, hi = c[2 * r], c[2 * r + 1]
            mid = 0.5 * (lo + hi)
            ok = count_ge(r, mid) >= capf
            out += [jnp.where(ok, mid, lo), jnp.where(ok, hi, mid)]
        return tuple(out)

    init = []
    for r in range(nreq):
        init += [pltpu.bitcast(thrs[r], F32), pltpu.bitcast(thrs[r] + 1, F32)]
    bounds = lax.fori_loop(0, 24, mid_step, tuple(init))

    u = (lax.broadcasted_iota(jnp.int32, (T, T), 0)
         <= lax.broadcasted_iota(jnp.int32, (T, T), 1)).astype(F32).astype(BF16)
    nblk = T // TOKEN_BLOCK
    cl = lax.broadcasted_iota(jnp.int32, c_ref.shape, 1)
    cnt = jnp.zeros(c_ref.shape, F32)
    for r in range(nreq):
        aff = a_ref[:, r * T:(r + 1) * T]
        lo, hi = bounds[2 * r], bounds[2 * r + 1]
        gt = aff >= hi
        eq = (aff >= lo) & jnp.logical_not(gt)
        need = capf - jnp.sum(gt.astype(F32), axis=1, keepdims=True)
        pe = _dot(eq.astype(F32).astype(BF16), u)
        sel = gt | (eq & (pe <= need))
        ps = _dot(sel.astype(F32).astype(BF16), u)
        slot = jnp.where(sel, ps - 1.0, -1.0)
        s_ref[:, r * T:(r + 1) * T] = slot
        st_ref[r * T:(r + 1) * T, :] = slot.T
        for k in range(nblk):
            end = (k + 1) * TOKEN_BLOCK
            cnt = jnp.where(cl == r * nblk + k, ps[:, end - 1:end], cnt)
    c_ref[...] = cnt


def _topk(aff_t, T, name):
    n = aff_t.shape[1]
    nreq = n // T
    ncnt = nreq * (T // TOKEN_BLOCK)
    kern = functools.partial(_topk_kernel, cap=T // 8, T=T, nreq=nreq)
    return pl.pallas_call(
        kern, grid=(1,),
        in_specs=[pl.BlockSpec((N_EXPERTS, n), lambda i: (0, 0))],
        out_specs=[pl.BlockSpec((N_EXPERTS, n), lambda i: (0, 0)),
                   pl.BlockSpec((n, N_EXPERTS), lambda i: (0, 0)),
                   pl.BlockSpec((N_EXPERTS, ncnt), lambda i: (0, 0))],
        out_shape=[jax.ShapeDtypeStruct((N_EXPERTS, n), F32),
                   jax.ShapeDtypeStruct((n, N_EXPERTS), F32),
                   jax.ShapeDtypeStruct((N_EXPERTS, ncnt), F32)],
        compiler_params=_cparams(48), name=name)(aff_t)


def _window_starts(start_ref, nblk):
    base = (pl.program_id(0) * nblk + pl.program_id(1)) * N_EXPERTS
    return [pl.multiple_of(start_ref[base + e], 16) for e in range(N_EXPERTS)]


def _gather_kernel(start_ref, ok_ref, h_ref, s_ref, a_ref, xg_ref, gt_ref, *, cap, W, nblk):
    @pl.when(pl.program_id(1) == 0)
    def _():
        xg_ref[...] = jnp.zeros_like(xg_ref)
        gt_ref[...] = jnp.zeros_like(gt_ref)

    h = h_ref[...]
    nch = 512

    def step(w, starts):
        j = lax.broadcasted_iota(jnp.int32, (w, TOKEN_BLOCK), 0).astype(F32)
        ohs = []
        for e in range(N_EXPERTS):
            rel = s_ref[e:e + 1, :]
            if starts is not None:
                rel = rel - starts[e].astype(F32)
            oh = rel == j
            rows = pl.ds(starts[e], w) if starts is not None else slice(0, w)
            gt_ref[e, rows, :] += jnp.sum(jnp.where(oh, a_ref[e:e + 1, :], 0.0), axis=1, keepdims=True)
            ohs.append(oh.astype(F32).astype(BF16))
        ohm = jnp.concatenate(ohs, axis=0)
        for c in range(0, D_MODEL, nch):
            res = _dot(ohm, h[:, c:c + nch])
            for e in range(N_EXPERTS):
                rows = pl.ds(starts[e], w) if starts is not None else slice(0, w)
                xg_ref[e, rows, c:c + nch] += res[e * w:(e + 1) * w].astype(BF16)

    if W == cap:
        step(cap, None)
    else:
        ok = ok_ref[pl.program_id(0)] > 0

        @pl.when(ok)
        def _():
            step(W, _window_starts(start_ref, nblk))

        @pl.when(jnp.logical_not(ok))
        def _():
            step(cap, None)


def _gather(h2, slot, aff_t, starts, ok, T, name):
    n = h2.shape[0]
    cap = T // 8
    nreq = n // T
    nblk = T // TOKEN_BLOCK
    kern = functools.partial(_gather_kernel, cap=cap, W=min(SLOT_WINDOW, cap), nblk=nblk)
    tok = lambda b, k, *_: (0, b * nblk + k)
    req = lambda b, k, *_: (0, b, 0)
    return pl.pallas_call(
        kern,
        grid_spec=pltpu.PrefetchScalarGridSpec(
            num_scalar_prefetch=2, grid=(nreq, nblk),
            in_specs=[pl.BlockSpec((TOKEN_BLOCK, D_MODEL), lambda b, k, *_: (b * nblk + k, 0)),
                      pl.BlockSpec((N_EXPERTS, TOKEN_BLOCK), tok),
                      pl.BlockSpec((N_EXPERTS, TOKEN_BLOCK), tok)],
            out_specs=[pl.BlockSpec((N_EXPERTS, cap, D_MODEL), req),
                       pl.BlockSpec((N_EXPERTS, cap, 1), req)]),
        out_shape=[jax.ShapeDtypeStruct((N_EXPERTS, nreq * cap, D_MODEL), BF16),
                   jax.ShapeDtypeStruct((N_EXPERTS, nreq * cap, 1), F32)],
        compiler_params=_cparams(56), name=name)(starts, ok, h2, slot, aff_t)


def _ffn_kernel(xc_ref, xl_ref, gc_ref, gl_ref, w1_ref, w3_ref, w2_ref, yc_ref, yl_ref, xbuf, acc, *, rc, nf, rsub):
    f = pl.program_id(2)

    @pl.when(f == 0)
    def _():
        xbuf[0:rc, :] = xc_ref[0]
        xbuf[rc:, :] = xl_ref[0]

    @pl.when(f == 0)
    def _():
        acc[...] = jnp.zeros_like(acc)

    w1 = w1_ref[0].astype(BF16)
    w3 = w3_ref[0].astype(BF16)
    w2 = w2_ref[0].astype(BF16)
    rows = xbuf.shape[0]
    for r0 in range(0, rows, rsub):
        x = xbuf[r0:r0 + rsub, :]
        hid = jax.nn.silu(_dot(x, w1)) * _dot(x, w3)
        acc[r0:r0 + rsub, :] += _dot(hid.astype(BF16), w2)

    @pl.when(f == nf - 1)
    def _():
        yc_ref[0] = (acc[0:rc, :] * gc_ref[0]).astype(BF16)
        yl_ref[0] = (acc[rc:, :] * gl_ref[0]).astype(BF16)


def _ffn(xg_c, xg_l, gt_c, gt_l, w1, w3, w2):
    rc = xg_c.shape[1] // 2
    rl = xg_l.shape[1] // 2
    bf = 256
    nf = D_MODEL // bf
    kern = functools.partial(_ffn_kernel, rc=rc, nf=nf, rsub=(rc + rl) // 2)
    rowsp = lambda r, w: pl.BlockSpec((1, r, w), lambda e, h, f: (e, h, 0))
    return pl.pallas_call(
        kern, grid=(N_EXPERTS, 2, nf),
        in_specs=[rowsp(rc, D_MODEL), rowsp(rl, D_MODEL), rowsp(rc, 1), rowsp(rl, 1),
                  pl.BlockSpec((1, D_MODEL, bf), lambda e, h, f: (e, 0, f)),
                  pl.BlockSpec((1, D_MODEL, bf), lambda e, h, f: (e, 0, f)),
                  pl.BlockSpec((1, bf, D_MODEL), lambda e, h, f: (e, f, 0))],
        out_specs=[rowsp(rc, D_MODEL), rowsp(rl, D_MODEL)],
        out_shape=[jax.ShapeDtypeStruct(xg_c.shape, BF16), jax.ShapeDtypeStruct(xg_l.shape, BF16)],
        scratch_shapes=[pltpu.VMEM((rc + rl, D_MODEL), BF16), pltpu.VMEM((rc + rl, D_MODEL), F32)],
        compiler_params=_cparams(56), name="ffn")(xg_c, xg_l, gt_c, gt_l, w1, w3, w2)


def _expand_consts(w):
    lane = jnp.arange(N_EXPERTS * w)
    expand = (jnp.arange(N_EXPERTS)[:, None] == (lane // w)[None, :]).astype(BF16)
    return expand, (lane % w).astype(F32)[None, :]


def _combine_kernel(start_ref, ok_ref, x_ref, y_ref, st_ref, sv_ref, mod_ref, fg_ref, exw_ref, jlw_ref,
                    exd_ref, jld_ref, o_ref, stage, *, cap, W, nblk):
    st = st_ref[...]

    def finish(moe):
        x2 = x_ref[...] + mod_ref[0, 5:6, :] * moe
        o_ref[...] = _rms(x2) * fg_ref[...]

    def dense():
        rep = _dot(st.astype(BF16), exd_ref[...])
        oh = (rep == jld_ref[...]).astype(F32).astype(BF16)
        finish(_dot(oh, y_ref[...].reshape(N_EXPERTS * cap, D_MODEL)))

    if W == cap:
        dense()
        return
    ok = ok_ref[pl.program_id(0)] > 0

    @pl.when(ok)
    def _():
        starts = _window_starts(start_ref, nblk)
        for e in range(N_EXPERTS):
            stage[e * W:(e + 1) * W, :] = y_ref[e, pl.ds(starts[e], W), :]
        rep = _dot((st - sv_ref[0]).astype(BF16), exw_ref[...])
        oh = (rep == jlw_ref[...]).astype(F32).astype(BF16)
        finish(_dot(oh, stage[...]))

    @pl.when(jnp.logical_not(ok))
    def _():
        dense()


def _combine(x1, ybuf, slot_t, starts, ok, starts_f, mod, final_g, T, name):
    n = x1.shape[0]
    cap = T // 8
    nblk = T // TOKEN_BLOCK
    W = min(SLOT_WINDOW, cap)
    multi = mod.shape[0] > 1
    exw, jlw = _expand_consts(W)
    exd, jld = _expand_consts(cap)
    kern = functools.partial(_combine_kernel, cap=cap, W=W, nblk=nblk)
    tok = lambda b, k, *_: (b * nblk + k, 0)
    const = lambda b, k, *_: (0, 0)
    return pl.pallas_call(
        kern,
        grid_spec=pltpu.PrefetchScalarGridSpec(
            num_scalar_prefetch=2, grid=(n // T, nblk),
            in_specs=[pl.BlockSpec((TOKEN_BLOCK, D_MODEL), tok),
                      pl.BlockSpec((N_EXPERTS, cap, D_MODEL), lambda b, k, *_: (0, b, 0)),
                      pl.BlockSpec((TOKEN_BLOCK, N_EXPERTS), tok),
                      pl.BlockSpec((1, 1, N_EXPERTS), lambda b, k, *_: (b * nblk + k, 0, 0)),
                      pl.BlockSpec((1, N_MOD, D_MODEL), lambda b, k, *_: (b if multi else 0, 0, 0)),
                      pl.BlockSpec((1, D_MODEL), const),
                      pl.BlockSpec(exw.shape, const), pl.BlockSpec(jlw.shape, const),
                      pl.BlockSpec(exd.shape, const), pl.BlockSpec(jld.shape, const)],
            out_specs=pl.BlockSpec((TOKEN_BLOCK, D_MODEL), tok),
            scratch_shapes=[pltpu.VMEM((N_EXPERTS * W, D_MODEL), BF16)]),
        out_shape=jax.ShapeDtypeStruct((n, D_MODEL), F32),
        compiler_params=_cparams(56), name=name)(starts, ok, x1, ybuf, slot_t, starts_f, mod, final_g,
                                                 exw, jlw, exd, jld)


def _window_tables(cnt, T):
    cap = T // 8
    nblk = T // TOKEN_BLOCK
    nreq = cnt.shape[1] // nblk
    W = min(SLOT_WINDOW, cap)
    end = cnt.astype(jnp.int32).reshape(N_EXPERTS, nreq, nblk)
    first = jnp.concatenate([jnp.zeros_like(end[..., :1]), end[..., :-1]], axis=-1)
    start = jnp.minimum((first // 16) * 16, cap - W)
    ok = jnp.all(end - start <= W, axis=(0, 2)).astype(jnp.int32)
    start = start.transpose(1, 2, 0)
    return start.reshape(-1), ok, start.astype(F32).reshape(nreq * nblk, 1, N_EXPERTS)


def _blockdiag_pairs(w):
    w = w.reshape(2, RG_SLABS, 2, 64, 64)
    bd = jnp.einsum('dshij,hk->dshikj', w, jnp.eye(2, dtype=w.dtype))
    return bd.reshape(2, RG_SLABS, LANES, LANES).transpose(1, 0, 2, 3)


def kernel(x_prompt, x_sample, state_rglru, state_s5_re, state_s5_im, c, c_ctx, norm1_g, norm2_g, w_ada, b_ada, w_in, rg_conv_w, rg_conv_b, rg_wa, rg_ba, rg_wx, rg_bx, rg_lam, s5_a_re, s5_a_im, s5_log_dt, s5_b_re, s5_b_im, s5_c_re, s5_c_im, s5_d, s5_w_glu, s5_b_glu, gnorm_rg, gnorm_s5, w_out, w_router, w1, w3, w2, final_g):
    l = 0
    Bp, Tp, _ = x_prompt.shape
    Bs, Ts, _ = x_sample.shape

    c16 = jnp.concatenate([c, c_ctx[None], jnp.zeros((16 - Bs - 1, D_MODEL), F32)], axis=0)
    mod = _adaln(c16, w_ada[l], b_ada[l]).reshape(16, N_MOD, D_MODEL)
    mod_lat = mod[:Bs]
    mod_ctx1 = mod[Bs:Bs + 1]
    mod_ctx = jnp.broadcast_to(mod_ctx1, (Bp, N_MOD, D_MODEL))

    w_in_b = w_in[l].astype(BF16)
    w_out_b = w_out[l].astype(BF16)
    w_glu_b = s5_w_glu[l].astype(BF16)
    row = lambda v: v.reshape(1, -1)

    wg = jnp.concatenate([_blockdiag_pairs(rg_wa[l]), _blockdiag_pairs(rg_wx[l])], axis=-1).astype(BF16)
    bg = jnp.concatenate([rg_ba[l].reshape(2, RG_SLABS, LANES), rg_bx[l].reshape(2, RG_SLABS, LANES)],
                         axis=-1).transpose(1, 0, 2).reshape(RG_SLABS, 2, 1, 2 * LANES)
    kmat, pmat, qmat, lam16 = _s5prep(s5_a_re[l], s5_a_im[l], s5_log_dt[l], s5_b_re[l], s5_b_im[l],
                                       s5_c_re[l], s5_c_im[l])

    def mixer(x, modp, rg_h0, s5_h0, lat, tag):
        B, T, _ = x.shape
        R = T * B
        proj = _inproj(x, modp, row(norm1_g[l]), w_in_b, "inproj_" + tag)
        y_rg, fin_rg = _rglru(proj, rg_conv_w[l], row(rg_conv_b[l]), wg, bg, rg_lam[l], rg_h0, B, T,
                              "rglru_" + tag)
        if lat:
            A, Bq = T // GRID_W // CHUNK, GRID_W * B
        else:
            A, Bq = T // CHUNK, B
        y_ssm, fin_s5 = _s5(proj.reshape(N_SLAB, A, CHUNK, Bq, LANES), kmat, pmat, qmat, lam16, s5_h0,
                            A, Bq, B, lat, "s5_" + tag)
        x1 = _post(y_rg, y_ssm.reshape(S5_SLABS, R, LANES), proj, x, modp, w_out_b, w_glu_b,
                   row(s5_b_glu[l]), row(s5_d[l]), row(gnorm_rg[l]), row(gnorm_s5[l]), "post_" + tag)
        return x1.reshape(R, D_MODEL), fin_rg, fin_s5

    z_rg = jnp.zeros((2, Bp, D_RG), F32)
    z_s5 = jnp.zeros((S5_GROUPS, Bp, 256), F32)
    x1_c, fin_rg, fin_s5 = mixer(x_prompt, mod_ctx, z_rg, z_s5, False, "ctx")
    rg_h0 = state_rglru[:, l].transpose(1, 0, 2)
    s5_h0 = jnp.stack([state_s5_re[:, l, 0], state_s5_re[:, l, 1], state_s5_im[:, l, 0], state_s5_im[:, l, 1]],
                      axis=0)
    s5_h0 = s5_h0.transpose(2, 1, 0, 3).reshape(S5_GROUPS, Bs, 256)
    x1_s, _, _ = mixer(x_sample, mod_lat, rg_h0, s5_h0, True, "lat")

    w_router_t = w_router[l].T
    g2 = row(norm2_g[l])

    def route(x1, modp, T, tag):
        h2, aff_t = _premoe(x1, modp, g2, w_router_t, T, "premoe_" + tag)
        slot, slot_t, cnt = _topk(aff_t, T, "topk_" + tag)
        win = _window_tables(cnt, T)
        xg, gates = _gather(h2, slot, aff_t, win[0], win[1], T, "gather_" + tag)
        return xg, gates, slot_t, win

    xg_c, gt_c, st_c, win_c = route(x1_c, mod_ctx1, Tp, "ctx")
    xg_s, gt_s, st_s, win_s = route(x1_s, mod_lat, Ts, "lat")
    y_c, y_s = _ffn(xg_c, xg_s, gt_c, gt_s, w1[l], w3[l], w2[l])
    fg = row(final_g)
    y_prompt = _combine(x1_c, y_c, st_c, *win_c, mod_ctx1, fg, Tp, "combine_ctx").reshape(Bp, Tp, D_MODEL)
    y_sample = _combine(x1_s, y_s, st_s, *win_s, mod_lat, fg, Ts, "combine_lat").reshape(Bs, Ts, D_MODEL)

    new_rg = fin_rg.transpose(1, 0, 2)[:, None]
    f4 = fin_s5.reshape(S5_GROUPS, Bp, 4, 64).transpose(1, 2, 0, 3)
    new_re = f4[:, None, 0:2]
    new_im = f4[:, None, 2:4]
    return (y_prompt, y_sample, new_rg, new_re, new_im)
```

```python
import functools

import jax
import jax.numpy as jnp
from jax import lax
from jax.experimental import pallas as pl
from jax.experimental.pallas import tpu as pltpu

F32 = jnp.float32
BF16 = jnp.bfloat16
HIGHEST = lax.Precision.HIGHEST

LANES = 128
D_MODEL = 2048
D_RG = 1536
D_S5 = 512
D_IN = 2 * D_RG + D_S5
N_SLAB = D_IN // LANES
RG_SLABS = D_RG // LANES
S5_SLABS = D_S5 // LANES
S5_GROUPS = 32
GROUPS_PER_SLAB = 8
CHUNK = 16
N_EXPERTS = 16
N_MOD = 6
EPS = 1e-6
RG_C = 8.0
LOG2_E = 1.4426950408889634
GRID_W = 64
TOKEN_BLOCK = 256
SLOT_WINDOW = 80
MIB = 1024 * 1024


def _cparams(vmem_mib):
    return pltpu.CompilerParams(vmem_limit_bytes=vmem_mib * MIB)


def _dot(a, b):
    return jnp.dot(a, b, preferred_element_type=F32)


def _rms(x):
    return x * lax.rsqrt(jnp.mean(x * x, axis=-1, keepdims=True) + EPS)


def _adaln_kernel(c_ref, w_ref, b_ref, o_ref):
    s = jax.nn.silu(c_ref[...]).astype(BF16)
    o_ref[...] = _dot(s, w_ref[...].astype(BF16)) + b_ref[...]


def _adaln(c16, w_ada, b_ada):
    n = w_ada.shape[1]
    bn = 1024
    return pl.pallas_call(
        _adaln_kernel, grid=(n // bn,),
        in_specs=[pl.BlockSpec((16, D_MODEL), lambda j: (0, 0)),
                  pl.BlockSpec((D_MODEL, bn), lambda j: (0, j)),
                  pl.BlockSpec((1, bn), lambda j: (0, j))],
        out_specs=pl.BlockSpec((16, bn), lambda j: (0, j)),
        out_shape=jax.ShapeDtypeStruct((16, n), F32),
        compiler_params=_cparams(40), name="adaln")(c16, w_ada, b_ada.reshape(1, n))


def _inproj_kernel(x_ref, mod_ref, g_ref, w_ref, o_ref, *, B, tT, nch):
    x = x_ref[...]
    shift = mod_ref[:, 0:1, :]
    scale = mod_ref[:, 1:2, :]
    h = _rms(x) * g_ref[...] * (1.0 + scale) + shift
    hb = h.reshape(B * tT, D_MODEL).astype(BF16)
    for c0 in range(0, D_IN, nch):
        p = _dot(hb, w_ref[:, c0:c0 + nch])
        for b in range(B):
            for j in range(nch // LANES):
                o_ref[c0 // LANES + j, pl.ds(b, tT, stride=B), :] = (
                    p[b * tT:(b + 1) * tT, j * LANES:(j + 1) * LANES])


def _inproj(x, mod, g1, w_in_bf16, name):
    B, T, _ = x.shape
    tT = 512 // B
    kern = functools.partial(_inproj_kernel, B=B, tT=tT, nch=512)
    return pl.pallas_call(
        kern, grid=(T // tT,),
        in_specs=[pl.BlockSpec((B, tT, D_MODEL), lambda i: (0, i, 0)),
                  pl.BlockSpec((B, N_MOD, D_MODEL), lambda i: (0, 0, 0)),
                  pl.BlockSpec((1, D_MODEL), lambda i: (0, 0)),
                  pl.BlockSpec((D_MODEL, D_IN), lambda i: (0, 0), pipeline_mode=pl.Buffered(1))],
        out_specs=pl.BlockSpec((N_SLAB, tT * B, LANES), lambda i: (0, i, 0)),
        out_shape=jax.ShapeDtypeStruct((N_SLAB, T * B, LANES), F32),
        compiler_params=_cparams(56), name=name)(x, mod, g1, w_in_bf16)


def _rglru_kernel(xb_ref, gt_ref, cw_ref, cb_ref, wg_ref, bg_ref, lam_ref, h0_ref, y_ref, fin_ref,
                  *, B, T, CT):
    RC = CT * B
    nc = T // CT
    cw = [cw_ref[k:k + 1, :] for k in range(4)]
    cb = cb_ref[...]
    hl = [(-0.5 * RG_C * LOG2_E) * jax.nn.softplus(-lam_ref[d:d + 1, :]) for d in range(2)]

    def chunk(n, d):
        t0 = n * CT
        row0 = pl.multiple_of(t0 * B, RC)
        cur = xb_ref[0, pl.ds(row0, RC), :]
        p0 = pl.multiple_of(jnp.maximum(t0 - 1, 0) * B, B)
        prev = xb_ref[0, pl.ds(p0, B), :] * jnp.where(t0 > 0, 1.0, 0.0)
        n0 = pl.multiple_of(jnp.minimum(t0 + CT, T - 2) * B, B)
        nxt = xb_ref[0, pl.ds(n0, 2 * B), :] * jnp.where(t0 + CT < T, 1.0, 0.0)
        ext = jnp.concatenate([prev, cur, nxt], axis=0)
        xc = (cw[0] * ext[0:RC] + cw[1] * ext[B:B + RC] + cw[2] * ext[2 * B:2 * B + RC]
              + cw[3] * ext[3 * B:3 * B + RC] + cb)
        th = jnp.tanh(_dot(xc.astype(BF16), wg_ref[0, d]) + bg_ref[0, d])
        i = 0.5 + 0.5 * th[:, LANES:]
        a = jnp.exp2(hl[d] + hl[d] * th[:, :LANES])
        om = 1.0 - a * a
        bb = jnp.where(om > 0.0, om * lax.rsqrt(om), 0.0) * (i * xc)
        return a, bb, row0

    def make_body(finish):
        def body(n, carry):
            hf, hb = carry
            af, bf, rf = chunk(n, 0)
            ab, bk, rb = chunk(nc - 1 - n, 1)
            if finish:
                glf = jax.nn.gelu(gt_ref[0, pl.ds(rf, RC), :])
                glb = jax.nn.gelu(gt_ref[0, pl.ds(rb, RC), :])
                of = y_ref[0, pl.ds(rf, RC), :]
                ob = y_ref[0, pl.ds(rb, RC), :]
            for t in range(CT):
                u = CT - 1 - t
                hf = af[t * B:(t + 1) * B] * hf + bf[t * B:(t + 1) * B]
                hb = ab[u * B:(u + 1) * B] * hb + bk[u * B:(u + 1) * B]
                rows_f = pl.ds(rf + t * B, B)
                rows_b = pl.ds(rb + u * B, B)
                if finish:
                    y_ref[0, rows_f, :] = (of[t * B:(t + 1) * B] + hf) * glf[t * B:(t + 1) * B]
                    y_ref[0, rows_b, :] = (ob[u * B:(u + 1) * B] + hb) * glb[u * B:(u + 1) * B]
                else:
                    y_ref[0, rows_f, :] = hf
                    y_ref[0, rows_b, :] = hb
            return hf, hb
        return body

    carry = lax.fori_loop(0, nc // 2, make_body(False), (h0_ref[0], h0_ref[1]))
    hf, hb = lax.fori_loop(nc // 2, nc, make_body(True), carry)
    fin_ref[0] = hf
    fin_ref[1] = hb


def _rglru(proj, cw, cb, wg, bg, lam, h0, B, T, name):
    R = T * B
    kern = functools.partial(_rglru_kernel, B=B, T=T, CT=512 // B)
    return pl.pallas_call(
        kern, grid=(RG_SLABS,),
        in_specs=[pl.BlockSpec((1, R, LANES), lambda s: (s, 0, 0)),
                  pl.BlockSpec((1, R, LANES), lambda s: (s + RG_SLABS, 0, 0)),
                  pl.BlockSpec((4, LANES), lambda s: (0, s)),
                  pl.BlockSpec((1, LANES), lambda s: (0, s)),
                  pl.BlockSpec((1, 2, LANES, 2 * LANES), lambda s: (s, 0, 0, 0)),
                  pl.BlockSpec((1, 2, 1, 2 * LANES), lambda s: (s, 0, 0, 0)),
                  pl.BlockSpec((2, LANES), lambda s: (0, s)),
                  pl.BlockSpec((2, B, LANES), lambda s: (0, 0, s))],
        out_specs=[pl.BlockSpec((1, R, LANES), lambda s: (s, 0, 0)),
                   pl.BlockSpec((2, B, LANES), lambda s: (0, 0, s))],
        out_shape=[jax.ShapeDtypeStruct((RG_SLABS, R, LANES), F32),
                   jax.ShapeDtypeStruct((2, B, D_RG), F32)],
        compiler_params=_cparams(58), name=name)(proj, proj, cw, cb, wg, bg, lam, h0)


def _cpow(pr, pi, n):
    wr = jnp.ones(n.shape, F32)
    wi = jnp.zeros(n.shape, F32)
    for k in range(4):
        bit = (n & (1 << k)) != 0
        wr, wi = jnp.where(bit, wr * pr - wi * pi, wr), jnp.where(bit, wr * pi + wi * pr, wi)
        pr, pi = pr * pr - pi * pi, 2.0 * pr * pi
    return wr, wi


def _s5prep_kernel(ar_ref, ai_ref, ldt_ref, arc_ref, aic_ref, ldtc_ref, btr_ref, bti_ref, ctr_ref, cti_ref,
                   k_ref, p_ref, q_ref, l_ref):
    ar = ar_ref[0]
    ai = ai_ref[0]
    dt = jnp.exp(ldt_ref[0])
    mag = jnp.exp(dt * ar)
    lr = mag * jnp.cos(dt * ai)
    li = mag * jnp.sin(dt * ai)
    den = ar * ar + ai * ai
    nre = lr - 1.0
    qre = (nre * ar + li * ai) / den
    qim = (li * ar - nre * ai) / den
    btr = btr_ref[0]
    bti = bti_ref[0]
    bbr = qre * btr - qim * bti
    bbi = qre * bti + qim * btr
    rowblk = lax.broadcasted_iota(jnp.int32, (256, LANES), 0) // CHUNK
    isf = lax.broadcasted_iota(jnp.int32, (256, LANES), 1) < 64

    def powers(n):
        return _cpow(lr, li, n)

    ctr = ctr_ref[0]
    cti = cti_ref[0]

    wr, wi = powers(rowblk)
    zr = wr * bbr - wi * bbi
    zi = wr * bbi + wi * bbr
    zero = jnp.zeros_like(zr)

    def mm(a, b):
        return jnp.dot(a, b, precision=HIGHEST, preferred_element_type=F32)

    mall_f = mm(jnp.where(isf, zr, zero), ctr) - mm(jnp.where(isf, zi, zero), cti)
    mall_b = mm(jnp.where(isf, zero, zr), ctr) - mm(jnp.where(isf, zero, zi), cti)
    blocks = [mall_f[(15 - q) * CHUNK:(16 - q) * CHUNK] for q in range(15)]
    blocks.append(mall_f[0:CHUNK] + mall_b[0:CHUNK])
    blocks += [mall_b[l * CHUNK:(l + 1) * CHUNK] for l in range(1, 16)]
    blocks.append(jnp.zeros((CHUNK, 256), F32))
    gw = jnp.concatenate(blocks, axis=0)
    colblk = lax.broadcasted_iota(jnp.int32, (256, 256), 1) // CHUNK
    kmat = jnp.zeros((256, 256), F32)
    for tp in range(CHUNK):
        kmat = jnp.where(colblk == tp, gw[(15 - tp) * CHUNK:(15 - tp) * CHUNK + 256], kmat)
    k_ref[0] = kmat.astype(BF16)

    wr, wi = powers(jnp.where(isf, 15 - rowblk, rowblk))
    p_ref[0] = jnp.concatenate([wr * bbr - wi * bbi, wr * bbi + wi * bbr], axis=1).astype(BF16)

    arc = arc_ref[0]
    aic = aic_ref[0]
    dtc = jnp.exp(ldtc_ref[0])
    laneblk = lax.broadcasted_iota(jnp.int32, (LANES, 256), 1) // CHUNK
    rowf = lax.broadcasted_iota(jnp.int32, (LANES, 256), 0) < 64
    magc = jnp.exp(dtc * arc)
    lrc = magc * jnp.cos(dtc * aic)
    lic = magc * jnp.sin(dtc * aic)
    vr, vi = _cpow(lrc, lic, jnp.where(rowf, laneblk, 15 - laneblk))
    wr = vr * lrc - vi * lic
    wi = vr * lic + vi * lrc
    q_ref[0] = jnp.concatenate([ctr * wr - cti * wi, -(ctr * wi + cti * wr)], axis=0).astype(BF16)

    pr, pi = lr, li
    for _ in range(4):
        pr, pi = pr * pr - pi * pi, 2.0 * pr * pi
    l_ref[0] = jnp.concatenate([pr, pi], axis=1)


def _s5prep(a_re, a_im, log_dt, b_re, b_im, c_re, c_im):
    G = S5_GROUPS

    def rows(x):
        return jnp.concatenate([x[0], x[1]], axis=-1).reshape(G, 1, LANES)

    ldt = jnp.broadcast_to(log_dt[:, :, None], (2, G, 64))
    ar, ai, ld = rows(a_re), rows(a_im), rows(ldt)
    arc, aic, ldc = (v.reshape(G, LANES, 1) for v in (ar, ai, ld))

    def btile(b):
        bt = jnp.tile(jnp.swapaxes(b, 2, 3), (1, 1, CHUNK, 1))
        return jnp.concatenate([bt[0], bt[1]], axis=-1)

    def ctile(c):
        ct = jnp.tile(jnp.swapaxes(c, 2, 3), (1, 1, 1, CHUNK))
        return jnp.concatenate([ct[0], ct[1]], axis=1)

    vec = pl.BlockSpec((1, 1, LANES), lambda g: (g, 0, 0))
    col = pl.BlockSpec((1, LANES, 1), lambda g: (g, 0, 0))
    bsp = pl.BlockSpec((1, 256, LANES), lambda g: (g, 0, 0))
    csp = pl.BlockSpec((1, LANES, 256), lambda g: (g, 0, 0))
    msp = pl.BlockSpec((1, 256, 256), lambda g: (g, 0, 0))
    return pl.pallas_call(
        _s5prep_kernel, grid=(G,),
        in_specs=[vec, vec, vec, col, col, col, bsp, bsp, csp, csp],
        out_specs=[msp, msp, msp, pl.BlockSpec((1, 1, 256), lambda g: (g, 0, 0))],
        out_shape=[jax.ShapeDtypeStruct((G, 256, 256), BF16)] * 3 + [jax.ShapeDtypeStruct((G, 1, 256), F32)],
        compiler_params=_cparams(32), name="s5prep")(
            ar, ai, ld, arc, aic, ldc, btile(b_re), btile(b_im), ctile(c_re), ctile(c_im))


def _seg_transpose8(vs, seg):
    vs = list(vs)
    for d in (4, 2, 1):
        hi = (seg & d) != 0
        new = list(vs)
        for i in range(8):
            if i & d:
                continue
            j = i + d
            new[i] = jnp.where(hi, pltpu.roll(vs[j], d * CHUNK, axis=1), vs[i])
            new[j] = jnp.where(hi, vs[j], pltpu.roll(vs[i], LANES - d * CHUNK, axis=1))
        vs = new
    return vs


def _s5_kernel(u_ref, k_ref, p_ref, q_ref, l_ref, h0_ref, y_ref, fin_ref, ebuf, ybuf, *, A, Bq, B, lat):
    RP = A * Bq
    nchunk = RP // B
    G = GROUPS_PER_SLAB
    nr = 256
    qn = min(Bq, nr)
    tiles = [(r0, [((r0 + i * qn) // Bq, (r0 + i * qn) % Bq) for i in range(nr // qn)])
             for r0 in range(0, RP, nr)]
    lane_blk = lax.broadcasted_iota(jnp.int32, (nr, LANES), 1) // CHUNK

    for r0, pieces in tiles:
        xs = [jnp.concatenate([u_ref[0, a, tau, q0:q0 + qn, :] for a, q0 in pieces], axis=0)
              for tau in range(CHUNK)]
        halves = [_seg_transpose8(xs[8 * h:8 * h + 8], lane_blk) for h in range(2)]
        for g in range(G):
            ug = jnp.concatenate([halves[0][g], halves[1][g]], axis=1).astype(BF16)
            ybuf[g, r0:r0 + nr, :] = _dot(ug, k_ref[g])
            ebuf[g, r0:r0 + nr, :] = _dot(ug, p_ref[g])

    def off(m):
        if lat:
            return ((m & 1) * GRID_W + (m >> 1)) * B
        return m * B

    fmask = (lax.broadcasted_iota(jnp.int32, (B, 2 * LANES), 1) % LANES) < 64
    fmask1 = lax.broadcasted_iota(jnp.int32, (B, LANES), 1) < 64
    lam = [l_ref[g] for g in range(G)]

    def body(i, carry):
        of = pl.multiple_of(off(i), B)
        ob = pl.multiple_of(off(nchunk - 1 - i), B)
        out = []
        for g in range(G):
            re, im = carry[2 * g], carry[2 * g + 1]
            ef = ebuf[g, pl.ds(of, B), :]
            eb = ebuf[g, pl.ds(ob, B), :]
            s = jnp.concatenate([re, im], axis=1)
            ebuf[g, pl.ds(of, B), :] = jnp.where(fmask, s, ef)
            ebuf[g, pl.ds(ob, B), :] = jnp.where(fmask, eb, s)
            lr = lam[g][:, :LANES]
            li = lam[g][:, LANES:]
            e_re = jnp.where(fmask1, ef[:, :LANES], eb[:, :LANES])
            e_im = jnp.where(fmask1, ef[:, LANES:], eb[:, LANES:])
            out.append(lr * re - li * im + e_re)
            out.append(lr * im + li * re + e_im)
        return tuple(out)

    init = []
    for g in range(G):
        h0 = h0_ref[g]
        init += [h0[:, :LANES], h0[:, LANES:]]
    fin = lax.fori_loop(0, nchunk, body, tuple(init))
    for g in range(G):
        fin_ref[g] = jnp.concatenate([fin[2 * g], fin[2 * g + 1]], axis=1)

    for g in range(G):
        ybuf[g] = ybuf[g] + _dot(ebuf[g].astype(BF16), q_ref[g])
    for r0, pieces in tiles:
        for h in range(2):
            yt = _seg_transpose8([ybuf[g, r0:r0 + nr, h * LANES:(h + 1) * LANES] for g in range(G)], lane_blk)
            for tp in range(8):
                for k, (a, q0) in enumerate(pieces):
                    y_ref[0, a, 8 * h + tp, q0:q0 + qn, :] = yt[tp][k * qn:(k + 1) * qn]


def _s5(proj5, kmat, pmat, qmat, lam16, h0, A, Bq, B, lat, name):
    RP = A * Bq
    G = GROUPS_PER_SLAB
    kern = functools.partial(_s5_kernel, A=A, Bq=Bq, B=B, lat=lat)
    msp = pl.BlockSpec((G, 256, 256), lambda s: (s, 0, 0))
    return pl.pallas_call(
        kern, grid=(S5_SLABS,),
        in_specs=[pl.BlockSpec((1, A, CHUNK, Bq, LANES), lambda s: (s + 2 * RG_SLABS, 0, 0, 0, 0)),
                  msp, msp, msp,
                  pl.BlockSpec((G, 1, 256), lambda s: (s, 0, 0)),
                  pl.BlockSpec((G, B, 256), lambda s: (s, 0, 0))],
        out_specs=[pl.BlockSpec((1, A, CHUNK, Bq, LANES), lambda s: (s, 0, 0, 0, 0)),
                   pl.BlockSpec((G, B, 256), lambda s: (s, 0, 0))],
        out_shape=[jax.ShapeDtypeStruct((S5_SLABS, A, CHUNK, Bq, LANES), F32),
                   jax.ShapeDtypeStruct((S5_GROUPS, B, 256), F32)],
        scratch_shapes=[pltpu.VMEM((G, RP, 256), F32), pltpu.VMEM((G, RP, 256), F32)],
        compiler_params=_cparams(58), name=name)(proj5, kmat, pmat, qmat, lam16, h0)


def _post_kernel(yr_ref, ys_ref, u_ref, x_ref, mod_ref, wo_ref, wg_ref, bgl_ref, d_ref, grg_ref, gs5_ref,
                 o_ref, ybuf, *, B, tT):
    yr = jnp.concatenate([yr_ref[s] for s in range(RG_SLABS)], axis=1)
    nr = (_rms(yr) * grg_ref[...]).astype(BF16)
    ys = jnp.concatenate([ys_ref[s] for s in range(S5_SLABS)], axis=1)
    u = jnp.concatenate([u_ref[s] for s in range(S5_SLABS)], axis=1)
    ys = jax.nn.gelu(ys + d_ref[...] * u)
    ys = ys * jax.nn.sigmoid(_dot(ys.astype(BF16), wg_ref[...]) + bgl_ref[...])
    ns = (_rms(ys) * gs5_ref[...]).astype(BF16)
    y = _dot(nr, wo_ref[0:D_RG, :]) + _dot(ns, wo_ref[D_RG:D_MODEL, :])
    nj = D_MODEL // LANES
    for j in range(nj):
        ybuf[j] = y[:, j * LANES:(j + 1) * LANES]
    for b in range(B):
        yb = jnp.concatenate([ybuf[j, pl.ds(b, tT, stride=B), :] for j in range(nj)], axis=1)
        o_ref[b] = x_ref[b] + mod_ref[b, 2:3, :] * yb


def _post(y_rg, y_ssm, proj, x, mod, w_out_bf16, w_glu_bf16, b_glu, s5_d, g_rg, g_s5, name):
    B, T, _ = x.shape
    tT = 512 // B
    rows = tT * B
    kern = functools.partial(_post_kernel, B=B, tT=tT)
    const = lambda i: (0, 0)
    return pl.pallas_call(
        kern, grid=(T // tT,),
        in_specs=[pl.BlockSpec((RG_SLABS, rows, LANES), lambda i: (0, i, 0)),
                  pl.BlockSpec((S5_SLABS, rows, LANES), lambda i: (0, i, 0)),
                  pl.BlockSpec((S5_SLABS, rows, LANES), lambda i: (2 * RG_SLABS // S5_SLABS, i, 0)),
                  pl.BlockSpec((B, tT, D_MODEL), lambda i: (0, i, 0)),
                  pl.BlockSpec((B, N_MOD, D_MODEL), lambda i: (0, 0, 0)),
                  pl.BlockSpec((D_MODEL, D_MODEL), const, pipeline_mode=pl.Buffered(1)),
                  pl.BlockSpec((D_S5, D_S5), const),
                  pl.BlockSpec((1, D_S5), const),
                  pl.BlockSpec((1, D_S5), const),
                  pl.BlockSpec((1, D_RG), const),
                  pl.BlockSpec((1, D_S5), const)],
        out_specs=pl.BlockSpec((B, tT, D_MODEL), lambda i: (0, i, 0)),
        out_shape=jax.ShapeDtypeStruct((B, T, D_MODEL), F32),
        scratch_shapes=[pltpu.VMEM((D_MODEL // LANES, rows, LANES), F32)],
        compiler_params=_cparams(56), name=name)(
            y_rg, y_ssm, proj, x, mod, w_out_bf16, w_glu_bf16, b_glu, s5_d, g_rg, g_s5)


def _premoe_kernel(x_ref, mod_ref, g_ref, wr_ref, h_ref, a_ref):
    h = _rms(x_ref[...]) * g_ref[...] * (1.0 + mod_ref[0, 4:5, :]) + mod_ref[0, 3:4, :]
    hhi = h.astype(BF16)
    h_ref[...] = hhi
    hlo = (h - hhi.astype(F32)).astype(BF16)
    w = wr_ref[...]
    whi = w.astype(BF16)
    wlo = (w - whi.astype(F32)).astype(BF16)
    nt = lambda a, b: lax.dot_general(a, b, (((1,), (1,)), ((), ())), preferred_element_type=F32)
    lg = nt(whi, hhi) + nt(wlo, hhi) + nt(whi, hlo)
    e = jnp.exp(lg - jnp.max(lg, axis=0, keepdims=True))
    a_ref[...] = e / jnp.sum(e, axis=0, keepdims=True)


def _premoe(x1, mod, g2, w_router_t, T, name):
    n = x1.shape[0]
    rows = 512
    per_req = T // rows if mod.shape[0] > 1 else n
    return pl.pallas_call(
        _premoe_kernel, grid=(n // rows,),
        in_specs=[pl.BlockSpec((rows, D_MODEL), lambda i: (i, 0)),
                  pl.BlockSpec((1, N_MOD, D_MODEL), lambda i: (i // per_req, 0, 0)),
                  pl.BlockSpec((1, D_MODEL), lambda i: (0, 0)),
                  pl.BlockSpec((N_EXPERTS, D_MODEL), lambda i: (0, 0))],
        out_specs=[pl.BlockSpec((rows, D_MODEL), lambda i: (i, 0)),
                   pl.BlockSpec((N_EXPERTS, rows), lambda i: (0, i))],
        out_shape=[jax.ShapeDtypeStruct((n, D_MODEL), BF16),
                   jax.ShapeDtypeStruct((N_EXPERTS, n), F32)],
        compiler_params=_cparams(40), name=name)(x1, mod, g2, w_router_t)


def _topk_kernel(a_ref, s_ref, st_ref, c_ref, *, cap, T, nreq):
    capf = float(cap)

    def count_ge(r, v):
        return jnp.sum((a_ref[:, r * T:(r + 1) * T] >= v).astype(F32), axis=1, keepdims=True)

    def bit_step(i, thrs):
        bit = jnp.int32(1) << (30 - i)
        out = []
        for r in range(nreq):
            cand = thrs[r] | bit
            ok = count_ge(r, pltpu.bitcast(cand, F32)) >= capf
            out.append(jnp.where(ok, cand, thrs[r]))
        return tuple(out)

    zero = jnp.zeros((N_EXPERTS, 1), jnp.int32)
    thrs = lax.fori_loop(0, 31, bit_step, (zero,) * nreq)

    def mid_step(i, c):
        out = []
        for r in range(nreq):
            lo, hi = c[2 * r], c[2 * r + 1]
            mid = 0.5 * (lo + hi)
            ok = count_ge(r, mid) >= capf
            out += [jnp.where(ok, mid, lo), jnp.where(ok, hi, mid)]
        return tuple(out)

    init = []
    for r in range(nreq):
        init += [pltpu.bitcast(thrs[r], F32), pltpu.bitcast(thrs[r] + 1, F32)]
    bounds = lax.fori_loop(0, 24, mid_step, tuple(init))

    u = (lax.broadcasted_iota(jnp.int32, (T, T), 0)
         <= lax.broadcasted_iota(jnp.int32, (T, T), 1)).astype(F32).astype(BF16)
    nblk = T // TOKEN_BLOCK
    cl = lax.broadcasted_iota(jnp.int32, c_ref.shape, 1)
    cnt = jnp.zeros(c_ref.shape, F32)
    for r in range(nreq):
        aff = a_ref[:, r * T:(r + 1) * T]
        lo, hi = bounds[2 * r], bounds[2 * r + 1]
        gt = aff >= hi
        eq = (aff >= lo) & jnp.logical_not(gt)
        need = capf - jnp.sum(gt.astype(F32), axis=1, keepdims=True)
        pe = _dot(eq.astype(F32).astype(BF16), u)
        sel = gt | (eq & (pe <= need))
        ps = _dot(sel.astype(F32).astype(BF16), u)
        slot = jnp.where(sel, ps - 1.0, -1.0)
        s_ref[:, r * T:(r + 1) * T] = slot
        st_ref[r * T:(r + 1) * T, :] = slot.T
        for k in range(nblk):
            end = (k + 1) * TOKEN_BLOCK
            cnt = jnp.where(cl == r * nblk + k, ps[:, end - 1:end], cnt)
    c_ref[...] = cnt


def _topk(aff_t, T, name):
    n = aff_t.shape[1]
    nreq = n // T
    ncnt = nreq * (T // TOKEN_BLOCK)
    kern = functools.partial(_topk_kernel, cap=T // 8, T=T, nreq=nreq)
    return pl.pallas_call(
        kern, grid=(1,),
        in_specs=[pl.BlockSpec((N_EXPERTS, n), lambda i: (0, 0))],
        out_specs=[pl.BlockSpec((N_EXPERTS, n), lambda i: (0, 0)),
                   pl.BlockSpec((n, N_EXPERTS), lambda i: (0, 0)),
                   pl.BlockSpec((N_EXPERTS, ncnt), lambda i: (0, 0))],
        out_shape=[jax.ShapeDtypeStruct((N_EXPERTS, n), F32),
                   jax.ShapeDtypeStruct((n, N_EXPERTS), F32),
                   jax.ShapeDtypeStruct((N_EXPERTS, ncnt), F32)],
        compiler_params=_cparams(48), name=name)(aff_t)


def _window_starts(start_ref, nblk):
    base = (pl.program_id(0) * nblk + pl.program_id(1)) * N_EXPERTS
    return [pl.multiple_of(start_ref[base + e], 16) for e in range(N_EXPERTS)]


def _gather_kernel(start_ref, ok_ref, h_ref, s_ref, a_ref, xg_ref, gt_ref, *, cap, W, nblk):
    @pl.when(pl.program_id(1) == 0)
    def _():
        xg_ref[...] = jnp.zeros_like(xg_ref)
        gt_ref[...] = jnp.zeros_like(gt_ref)

    h = h_ref[...]
    nch = 512

    def step(w, starts):
        j = lax.broadcasted_iota(jnp.int32, (w, TOKEN_BLOCK), 0).astype(F32)
        ohs = []
        for e in range(N_EXPERTS):
            rel = s_ref[e:e + 1, :]
            if starts is not None:
                rel = rel - starts[e].astype(F32)
            oh = rel == j
            rows = pl.ds(starts[e], w) if starts is not None else slice(0, w)
            gt_ref[e, rows, :] += jnp.sum(jnp.where(oh, a_ref[e:e + 1, :], 0.0), axis=1, keepdims=True)
            ohs.append(oh.astype(F32).astype(BF16))
        ohm = jnp.concatenate(ohs, axis=0)
        for c in range(0, D_MODEL, nch):
            res = _dot(ohm, h[:, c:c + nch])
            for e in range(N_EXPERTS):
                rows = pl.ds(starts[e], w) if starts is not None else slice(0, w)
                xg_ref[e, rows, c:c + nch] += res[e * w:(e + 1) * w].astype(BF16)

    if W == cap:
        step(cap, None)
    else:
        ok = ok_ref[pl.program_id(0)] > 0

        @pl.when(ok)
        def _():
            step(W, _window_starts(start_ref, nblk))

        @pl.when(jnp.logical_not(ok))
        def _():
            step(cap, None)


def _gather(h2, slot, aff_t, starts, ok, T, name):
    n = h2.shape[0]
    cap = T // 8
    nreq = n // T
    nblk = T // TOKEN_BLOCK
    kern = functools.partial(_gather_kernel, cap=cap, W=min(SLOT_WINDOW, cap), nblk=nblk)
    tok = lambda b, k, *_: (0, b * nblk + k)
    req = lambda b, k, *_: (0, b, 0)
    return pl.pallas_call(
        kern,
        grid_spec=pltpu.PrefetchScalarGridSpec(
            num_scalar_prefetch=2, grid=(nreq, nblk),
            in_specs=[pl.BlockSpec((TOKEN_BLOCK, D_MODEL), lambda b, k, *_: (b * nblk + k, 0)),
                      pl.BlockSpec((N_EXPERTS, TOKEN_BLOCK), tok),
                      pl.BlockSpec((N_EXPERTS, TOKEN_BLOCK), tok)],
            out_specs=[pl.BlockSpec((N_EXPERTS, cap, D_MODEL), req),
                       pl.BlockSpec((N_EXPERTS, cap, 1), req)]),
        out_shape=[jax.ShapeDtypeStruct((N_EXPERTS, nreq * cap, D_MODEL), BF16),
                   jax.ShapeDtypeStruct((N_EXPERTS, nreq * cap, 1), F32)],
        compiler_params=_cparams(56), name=name)(starts, ok, h2, slot, aff_t)


def _ffn_kernel(xc_ref, xl_ref, gc_ref, gl_ref, w1_ref, w3_ref, w2_ref, yc_ref, yl_ref, xbuf, acc, *, rc, nf, rsub):
    f = pl.program_id(2)

    @pl.when(f == 0)
    def _():
        xbuf[0:rc, :] = xc_ref[0]
        xbuf[rc:, :] = xl_ref[0]

    @pl.when(f == 0)
    def _():
        acc[...] = jnp.zeros_like(acc)

    w1 = w1_ref[0].astype(BF16)
    w3 = w3_ref[0].astype(BF16)
    w2 = w2_ref[0].astype(BF16)
    rows = xbuf.shape[0]
    for r0 in range(0, rows, rsub):
        x = xbuf[r0:r0 + rsub, :]
        hid = jax.nn.silu(_dot(x, w1)) * _dot(x, w3)
        acc[r0:r0 + rsub, :] += _dot(hid.astype(BF16), w2)

    @pl.when(f == nf - 1)
    def _():
        yc_ref[0] = (acc[0:rc, :] * gc_ref[0]).astype(BF16)
        yl_ref[0] = (acc[rc:, :] * gl_ref[0]).astype(BF16)


def _ffn(xg_c, xg_l, gt_c, gt_l, w1, w3, w2):
    rc = xg_c.shape[1] // 2
    rl = xg_l.shape[1] // 2
    bf = 256
    nf = D_MODEL // bf
    kern = functools.partial(_ffn_kernel, rc=rc, nf=nf, rsub=(rc + rl) // 2)
    rowsp = lambda r, w: pl.BlockSpec((1, r, w), lambda e, h, f: (e, h, 0))
    return pl.pallas_call(
        kern, grid=(N_EXPERTS, 2, nf),
        in_specs=[rowsp(rc, D_MODEL), rowsp(rl, D_MODEL), rowsp(rc, 1), rowsp(rl, 1),
                  pl.BlockSpec((1, D_MODEL, bf), lambda e, h, f: (e, 0, f)),
                  pl.BlockSpec((1, D_MODEL, bf), lambda e, h, f: (e, 0, f)),
                  pl.BlockSpec((1, bf, D_MODEL), lambda e, h, f: (e, f, 0))],
        out_specs=[rowsp(rc, D_MODEL), rowsp(rl, D_MODEL)],
        out_shape=[jax.ShapeDtypeStruct(xg_c.shape, BF16), jax.ShapeDtypeStruct(xg_l.shape, BF16)],
        scratch_shapes=[pltpu.VMEM((rc + rl, D_MODEL), BF16), pltpu.VMEM((rc + rl, D_MODEL), F32)],
        compiler_params=_cparams(56), name="ffn")(xg_c, xg_l, gt_c, gt_l, w1, w3, w2)


def _expand_consts(w):
    lane = jnp.arange(N_EXPERTS * w)
    expand = (jnp.arange(N_EXPERTS)[:, None] == (lane // w)[None, :]).astype(BF16)
    return expand, (lane % w).astype(F32)[None, :]


def _combine_kernel(start_ref, ok_ref, x_ref, y_ref, st_ref, sv_ref, mod_ref, fg_ref, exw_ref, jlw_ref,
                    exd_ref, jld_ref, o_ref, stage, *, cap, W, nblk):
    st = st_ref[...]

    def finish(moe):
        x2 = x_ref[...] + mod_ref[0, 5:6, :] * moe
        o_ref[...] = _rms(x2) * fg_ref[...]

    def dense():
        rep = _dot(st.astype(BF16), exd_ref[...])
        oh = (rep == jld_ref[...]).astype(F32).astype(BF16)
        finish(_dot(oh, y_ref[...].reshape(N_EXPERTS * cap, D_MODEL)))

    if W == cap:
        dense()
        return
    ok = ok_ref[pl.program_id(0)] > 0

    @pl.when(ok)
    def _():
        starts = _window_starts(start_ref, nblk)
        for e in range(N_EXPERTS):
            stage[e * W:(e + 1) * W, :] = y_ref[e, pl.ds(starts[e], W), :]
        rep = _dot((st - sv_ref[0]).astype(BF16), exw_ref[...])
        oh = (rep == jlw_ref[...]).astype(F32).astype(BF16)
        finish(_dot(oh, stage[...]))

    @pl.when(jnp.logical_not(ok))
    def _():
        dense()


def _combine(x1, ybuf, slot_t, starts, ok, starts_f, mod, final_g, T, name):
    n = x1.shape[0]
    cap = T // 8
    nblk = T // TOKEN_BLOCK
    W = min(SLOT_WINDOW, cap)
    multi = mod.shape[0] > 1
    exw, jlw = _expand_consts(W)
    exd, jld = _expand_consts(cap)
    kern = functools.partial(_combine_kernel, cap=cap, W=W, nblk=nblk)
    tok = lambda b, k, *_: (b * nblk + k, 0)
    const = lambda b, k, *_: (0, 0)
    return pl.pallas_call(
        kern,
        grid_spec=pltpu.PrefetchScalarGridSpec(
            num_scalar_prefetch=2, grid=(n // T, nblk),
            in_specs=[pl.BlockSpec((TOKEN_BLOCK, D_MODEL), tok),
                      pl.BlockSpec((N_EXPERTS, cap, D_MODEL), lambda b, k, *_: (0, b, 0)),
                      pl.BlockSpec((TOKEN_BLOCK, N_EXPERTS), tok),
                      pl.BlockSpec((1, 1, N_EXPERTS), lambda b, k, *_: (b * nblk + k, 0, 0)),
                      pl.BlockSpec((1, N_MOD, D_MODEL), lambda b, k, *_: (b if multi else 0, 0, 0)),
                      pl.BlockSpec((1, D_MODEL), const),
                      pl.BlockSpec(exw.shape, const), pl.BlockSpec(jlw.shape, const),
                      pl.BlockSpec(exd.shape, const), pl.BlockSpec(jld.shape, const)],
            out_specs=pl.BlockSpec((TOKEN_BLOCK, D_MODEL), tok),
            scratch_shapes=[pltpu.VMEM((N_EXPERTS * W, D_MODEL), BF16)]),
        out_shape=jax.ShapeDtypeStruct((n, D_MODEL), F32),
        compiler_params=_cparams(56), name=name)(starts, ok, x1, ybuf, slot_t, starts_f, mod, final_g,
                                                 exw, jlw, exd, jld)


def _window_tables(cnt, T):
    cap = T // 8
    nblk = T // TOKEN_BLOCK
    nreq = cnt.shape[1] // nblk
    W = min(SLOT_WINDOW, cap)
    end = cnt.astype(jnp.int32).reshape(N_EXPERTS, nreq, nblk)
    first = jnp.concatenate([jnp.zeros_like(end[..., :1]), end[..., :-1]], axis=-1)
    start = jnp.minimum((first // 16) * 16, cap - W)
    ok = jnp.all(end - start <= W, axis=(0, 2)).astype(jnp.int32)
    start = start.transpose(1, 2, 0)
    return start.reshape(-1), ok, start.astype(F32).reshape(nreq * nblk, 1, N_EXPERTS)


def _blockdiag_pairs(w):
    w = w.reshape(2, RG_SLABS, 2, 64, 64)
    bd = jnp.einsum('dshij,hk->dshikj', w, jnp.eye(2, dtype=w.dtype))
    return bd.reshape(2, RG_SLABS, LANES, LANES).transpose(1, 0, 2, 3)


def kernel(x_prompt, x_sample, state_rglru, state_s5_re, state_s5_im, c, c_ctx, norm1_g, norm2_g, w_ada, b_ada, w_in, rg_conv_w, rg_conv_b, rg_wa, rg_ba, rg_wx, rg_bx, rg_lam, s5_a_re, s5_a_im, s5_log_dt, s5_b_re, s5_b_im, s5_c_re, s5_c_im, s5_d, s5_w_glu, s5_b_glu, gnorm_rg, gnorm_s5, w_out, w_router, w1, w3, w2, final_g):
    l = 0
    Bp, Tp, _ = x_prompt.shape
    Bs, Ts, _ = x_sample.shape

    c16 = jnp.concatenate([c, c_ctx[None], jnp.zeros((16 - Bs - 1, D_MODEL), F32)], axis=0)
    mod = _adaln(c16, w_ada[l], b_ada[l]).reshape(16, N_MOD, D_MODEL)
    mod_lat = mod[:Bs]
    mod_ctx1 = mod[Bs:Bs + 1]
    mod_ctx = jnp.broadcast_to(mod_ctx1, (Bp, N_MOD, D_MODEL))

    w_in_b = w_in[l].astype(BF16)
    w_out_b = w_out[l].astype(BF16)
    w_glu_b = s5_w_glu[l].astype(BF16)
    row = lambda v: v.reshape(1, -1)

    wg = (0.5 * jnp.concatenate([_blockdiag_pairs(rg_wa[l]), _blockdiag_pairs(rg_wx[l])], axis=-1)).astype(BF16)
    bg = 0.5 * jnp.concatenate([rg_ba[l].reshape(2, RG_SLABS, LANES), rg_bx[l].reshape(2, RG_SLABS, LANES)],
                               axis=-1).transpose(1, 0, 2).reshape(RG_SLABS, 2, 1, 2 * LANES)
    kmat, pmat, qmat, lam16 = _s5prep(s5_a_re[l], s5_a_im[l], s5_log_dt[l], s5_b_re[l], s5_b_im[l],
                                       s5_c_re[l], s5_c_im[l])

    def mixer(x, modp, rg_h0, s5_h0, lat, tag):
        B, T, _ = x.shape
        R = T * B
        proj = _inproj(x, modp, row(norm1_g[l]), w_in_b, "inproj_" + tag)
        y_rg, fin_rg = _rglru(proj, rg_conv_w[l], row(rg_conv_b[l]), wg, bg, rg_lam[l], rg_h0, B, T,
                              "rglru_" + tag)
        if lat:
            A, Bq = T // GRID_W // CHUNK, GRID_W * B
        else:
            A, Bq = T // CHUNK, B
        y_ssm, fin_s5 = _s5(proj.reshape(N_SLAB, A, CHUNK, Bq, LANES), kmat, pmat, qmat, lam16, s5_h0,
                            A, Bq, B, lat, "s5_" + tag)
        x1 = _post(y_rg, y_ssm.reshape(S5_SLABS, R, LANES), proj, x, modp, w_out_b, w_glu_b,
                   row(s5_b_glu[l]), row(s5_d[l]), row(gnorm_rg[l]), row(gnorm_s5[l]), "post_" + tag)
        return x1.reshape(R, D_MODEL), fin_rg, fin_s5

    z_rg = jnp.zeros((2, Bp, D_RG), F32)
    z_s5 = jnp.zeros((S5_GROUPS, Bp, 256), F32)
    x1_c, fin_rg, fin_s5 = mixer(x_prompt, mod_ctx, z_rg, z_s5, False, "ctx")
    rg_h0 = state_rglru[:, l].transpose(1, 0, 2)
    s5_h0 = jnp.stack([state_s5_re[:, l, 0], state_s5_re[:, l, 1], state_s5_im[:, l, 0], state_s5_im[:, l, 1]],
                      axis=0)
    s5_h0 = s5_h0.transpose(2, 1, 0, 3).reshape(S5_GROUPS, Bs, 256)
    x1_s, _, _ = mixer(x_sample, mod_lat, rg_h0, s5_h0, True, "lat")

    w_router_t = w_router[l].T
    g2 = row(norm2_g[l])

    def route(x1, modp, T, tag):
        h2, aff_t = _premoe(x1, modp, g2, w_router_t, T, "premoe_" + tag)
        slot, slot_t, cnt = _topk(aff_t, T, "topk_" + tag)
        win = _window_tables(cnt, T)
        xg, gates = _gather(h2, slot, aff_t, win[0], win[1], T, "gather_" + tag)
        return xg, gates, slot_t, win

    xg_c, gt_c, st_c, win_c = route(x1_c, mod_ctx1, Tp, "ctx")
    xg_s, gt_s, st_s, win_s = route(x1_s, mod_lat, Ts, "lat")
    y_c, y_s = _ffn(xg_c, xg_s, gt_c, gt_s, w1[l], w3[l], w2[l])
    fg = row(final_g)
    y_prompt = _combine(x1_c, y_c, st_c, *win_c, mod_ctx1, fg, Tp, "combine_ctx").reshape(Bp, Tp, D_MODEL)
    y_sample = _combine(x1_s, y_s, st_s, *win_s, mod_lat, fg, Ts, "combine_lat").reshape(Bs, Ts, D_MODEL)

    new_rg = fin_rg.transpose(1, 0, 2)[:, None]
    f4 = fin_s5.reshape(S5_GROUPS, Bp, 4, 64).transpose(1, 2, 0, 3)
    new_re = f4[:, None, 0:2]
    new_im = f4[:, None, 2:4]
    return (y_prompt, y_sample, new_rg, new_re, new_im)
```

```python
import functools

import jax
import jax.numpy as jnp
from jax import lax
from jax.experimental import pallas as pl
from jax.experimental.pallas import tpu as pltpu

F32 = jnp.float32
BF16 = jnp.bfloat16
HIGHEST = lax.Precision.HIGHEST

LANES = 128
D_MODEL = 2048
D_RG = 1536
D_S5 = 512
D_IN = 2 * D_RG + D_S5
N_SLAB = D_IN // LANES
RG_SLABS = D_RG // LANES
S5_SLABS = D_S5 // LANES
S5_GROUPS = 32
GROUPS_PER_SLAB = 8
CHUNK = 16
N_EXPERTS = 16
N_MOD = 6
EPS = 1e-6
RG_C = 8.0
LOG2_E = 1.4426950408889634
GRID_W = 64
TOKEN_BLOCK = 256
SLOT_WINDOW = 80
MIB = 1024 * 1024


def _cparams(vmem_mib):
    return pltpu.CompilerParams(vmem_limit_bytes=vmem_mib * MIB)


def _dot(a, b):
    return jnp.dot(a, b, preferred_element_type=F32)


def _rms(x):
    return x * lax.rsqrt(jnp.mean(x * x, axis=-1, keepdims=True) + EPS)


def _adaln_kernel(c_ref, w_ref, b_ref, o_ref):
    s = jax.nn.silu(c_ref[...]).astype(BF16)
    o_ref[...] = _dot(s, w_ref[...].astype(BF16)) + b_ref[...]


def _adaln(c16, w_ada, b_ada):
    n = w_ada.shape[1]
    bn = 1024
    return pl.pallas_call(
        _adaln_kernel, grid=(n // bn,),
        in_specs=[pl.BlockSpec((16, D_MODEL), lambda j: (0, 0)),
                  pl.BlockSpec((D_MODEL, bn), lambda j: (0, j)),
                  pl.BlockSpec((1, bn), lambda j: (0, j))],
        out_specs=pl.BlockSpec((16, bn), lambda j: (0, j)),
        out_shape=jax.ShapeDtypeStruct((16, n), F32),
        compiler_params=_cparams(40), name="adaln")(c16, w_ada, b_ada.reshape(1, n))


def _sublane_transpose8(vs, sub):
    vs = list(vs)
    for d in (4, 2, 1):
        hi = (sub & d) != 0
        new = list(vs)
        for i in range(8):
            if i & d:
                continue
            new[i] = jnp.where(hi, pltpu.roll(vs[i + d], d, axis=1), vs[i])
            new[i + d] = jnp.where(hi, vs[i + d], pltpu.roll(vs[i], 8 - d, axis=1))
        vs = new
    return vs


def _rows_bt_to_tb(p, B, tT):
    sub = lax.broadcasted_iota(jnp.int32, (tT // 8, 8, LANES), 1)
    groups = []
    for g in range(B // 8):
        vs = [p[(8 * g + i) * tT:(8 * g + i + 1) * tT].reshape(tT // 8, 8, LANES) for i in range(8)]
        groups.append(_sublane_transpose8(vs, sub))
    out = jnp.stack([jnp.stack([grp[a] for grp in groups], axis=1) for a in range(8)], axis=1)
    return out.reshape(tT * B, LANES)


def _rows_tb_to_bt(y, B, tT):
    ng = B // 8
    y4 = y.reshape(tT // 8, 8 * ng, 8, LANES)
    sub = lax.broadcasted_iota(jnp.int32, (tT // 8, 8, LANES), 1)
    out = []
    for g in range(ng):
        out += _sublane_transpose8([y4[:, a * ng + g] for a in range(8)], sub)
    return jnp.concatenate([v.reshape(tT, LANES) for v in out], axis=0)


def _inproj_kernel(x_ref, mod_ref, g_ref, w_ref, o_ref, *, B, tT, nch):
    x = x_ref[...]
    shift = mod_ref[:, 0:1, :]
    scale = mod_ref[:, 1:2, :]
    h = _rms(x) * g_ref[...] * (1.0 + scale) + shift
    hb = h.reshape(B * tT, D_MODEL).astype(BF16)
    for c0 in range(0, D_IN, nch):
        p = _dot(hb, w_ref[:, c0:c0 + nch])
        for j in range(nch // LANES):
            o_ref[c0 // LANES + j] = _rows_bt_to_tb(p[:, j * LANES:(j + 1) * LANES], B, tT)


def _inproj(x, mod, g1, w_in_bf16, name):
    B, T, _ = x.shape
    tT = 512 // B
    kern = functools.partial(_inproj_kernel, B=B, tT=tT, nch=512)
    return pl.pallas_call(
        kern, grid=(T // tT,),
        in_specs=[pl.BlockSpec((B, tT, D_MODEL), lambda i: (0, i, 0)),
                  pl.BlockSpec((B, N_MOD, D_MODEL), lambda i: (0, 0, 0)),
                  pl.BlockSpec((1, D_MODEL), lambda i: (0, 0)),
                  pl.BlockSpec((D_MODEL, D_IN), lambda i: (0, 0), pipeline_mode=pl.Buffered(1))],
        out_specs=pl.BlockSpec((N_SLAB, tT * B, LANES), lambda i: (0, i, 0)),
        out_shape=jax.ShapeDtypeStruct((N_SLAB, T * B, LANES), F32),
        compiler_params=_cparams(56), name=name)(x, mod, g1, w_in_bf16)


def _rglru_kernel(xb_ref, gt_ref, cw_ref, cb_ref, wg_ref, bg_ref, lam_ref, h0_ref, y_ref, fin_ref,
                  *, B, T, CT):
    RC = CT * B
    nc = T // CT
    cw = [cw_ref[k:k + 1, :] for k in range(4)]
    cb = cb_ref[...]
    hl = [(-0.5 * RG_C * LOG2_E) * jax.nn.softplus(-lam_ref[d:d + 1, :]) for d in range(2)]

    def chunk(n, d):
        t0 = n * CT
        row0 = pl.multiple_of(t0 * B, RC)
        cur = xb_ref[0, pl.ds(row0, RC), :]
        p0 = pl.multiple_of(jnp.maximum(t0 - 1, 0) * B, B)
        prev = xb_ref[0, pl.ds(p0, B), :] * jnp.where(t0 > 0, 1.0, 0.0)
        n0 = pl.multiple_of(jnp.minimum(t0 + CT, T - 2) * B, B)
        nxt = xb_ref[0, pl.ds(n0, 2 * B), :] * jnp.where(t0 + CT < T, 1.0, 0.0)
        ext = jnp.concatenate([prev, cur, nxt], axis=0)
        xc = (cw[0] * ext[0:RC] + cw[1] * ext[B:B + RC] + cw[2] * ext[2 * B:2 * B + RC]
              + cw[3] * ext[3 * B:3 * B + RC] + cb)
        th = jnp.tanh(_dot(xc.astype(BF16), wg_ref[0, d]) + bg_ref[0, d])
        i = 0.5 + 0.5 * th[:, LANES:]
        a = jnp.exp2(hl[d] + hl[d] * th[:, :LANES])
        om = 1.0 - a * a
        bb = jnp.where(om > 0.0, om * lax.rsqrt(om), 0.0) * (i * xc)
        return a, bb, row0

    def make_body(finish):
        def body(n, carry):
            hf, hb = carry
            af, bf, rf = chunk(n, 0)
            ab, bk, rb = chunk(nc - 1 - n, 1)
            if finish:
                glf = jax.nn.gelu(gt_ref[0, pl.ds(rf, RC), :])
                glb = jax.nn.gelu(gt_ref[0, pl.ds(rb, RC), :])
                of = y_ref[0, pl.ds(rf, RC), :]
                ob = y_ref[0, pl.ds(rb, RC), :]
            for t in range(CT):
                u = CT - 1 - t
                hf = af[t * B:(t + 1) * B] * hf + bf[t * B:(t + 1) * B]
                hb = ab[u * B:(u + 1) * B] * hb + bk[u * B:(u + 1) * B]
                rows_f = pl.ds(rf + t * B, B)
                rows_b = pl.ds(rb + u * B, B)
                if finish:
                    y_ref[0, rows_f, :] = (of[t * B:(t + 1) * B] + hf) * glf[t * B:(t + 1) * B]
                    y_ref[0, rows_b, :] = (ob[u * B:(u + 1) * B] + hb) * glb[u * B:(u + 1) * B]
                else:
                    y_ref[0, rows_f, :] = hf
                    y_ref[0, rows_b, :] = hb
            return hf, hb
        return body

    carry = lax.fori_loop(0, nc // 2, make_body(False), (h0_ref[0], h0_ref[1]))
    hf, hb = lax.fori_loop(nc // 2, nc, make_body(True), carry)
    fin_ref[0] = hf
    fin_ref[1] = hb


def _rglru(proj, cw, cb, wg, bg, lam, h0, B, T, name):
    R = T * B
    kern = functools.partial(_rglru_kernel, B=B, T=T, CT=512 // B)
    return pl.pallas_call(
        kern, grid=(RG_SLABS,),
        in_specs=[pl.BlockSpec((1, R, LANES), lambda s: (s, 0, 0)),
                  pl.BlockSpec((1, R, LANES), lambda s: (s + RG_SLABS, 0, 0)),
                  pl.BlockSpec((4, LANES), lambda s: (0, s)),
                  pl.BlockSpec((1, LANES), lambda s: (0, s)),
                  pl.BlockSpec((1, 2, LANES, 2 * LANES), lambda s: (s, 0, 0, 0)),
                  pl.BlockSpec((1, 2, 1, 2 * LANES), lambda s: (s, 0, 0, 0)),
                  pl.BlockSpec((2, LANES), lambda s: (0, s)),
                  pl.BlockSpec((2, B, LANES), lambda s: (0, 0, s))],
        out_specs=[pl.BlockSpec((1, R, LANES), lambda s: (s, 0, 0)),
                   pl.BlockSpec((2, B, LANES), lambda s: (0, 0, s))],
        out_shape=[jax.ShapeDtypeStruct((RG_SLABS, R, LANES), F32),
                   jax.ShapeDtypeStruct((2, B, D_RG), F32)],
        compiler_params=_cparams(58), name=name)(proj, proj, cw, cb, wg, bg, lam, h0)


def _cpow(pr, pi, n):
    wr = jnp.ones(n.shape, F32)
    wi = jnp.zeros(n.shape, F32)
    for k in range(4):
        bit = (n & (1 << k)) != 0
        wr, wi = jnp.where(bit, wr * pr - wi * pi, wr), jnp.where(bit, wr * pi + wi * pr, wi)
        pr, pi = pr * pr - pi * pi, 2.0 * pr * pi
    return wr, wi


def _s5prep_kernel(ar_ref, ai_ref, ldt_ref, arc_ref, aic_ref, ldtc_ref, btr_ref, bti_ref, ctr_ref, cti_ref,
                   k_ref, p_ref, q_ref, l_ref):
    ar = ar_ref[0]
    ai = ai_ref[0]
    dt = jnp.exp(ldt_ref[0])
    mag = jnp.exp(dt * ar)
    lr = mag * jnp.cos(dt * ai)
    li = mag * jnp.sin(dt * ai)
    den = ar * ar + ai * ai
    nre = lr - 1.0
    qre = (nre * ar + li * ai) / den
    qim = (li * ar - nre * ai) / den
    btr = btr_ref[0]
    bti = bti_ref[0]
    bbr = qre * btr - qim * bti
    bbi = qre * bti + qim * btr
    rowblk = lax.broadcasted_iota(jnp.int32, (256, LANES), 0) // CHUNK
    isf = lax.broadcasted_iota(jnp.int32, (256, LANES), 1) < 64

    def powers(n):
        return _cpow(lr, li, n)

    ctr = ctr_ref[0]
    cti = cti_ref[0]

    wr, wi = powers(rowblk)
    zr = wr * bbr - wi * bbi
    zi = wr * bbi + wi * bbr
    zero = jnp.zeros_like(zr)

    def mm(a, b):
        return jnp.dot(a, b, precision=HIGHEST, preferred_element_type=F32)

    mall_f = mm(jnp.where(isf, zr, zero), ctr) - mm(jnp.where(isf, zi, zero), cti)
    mall_b = mm(jnp.where(isf, zero, zr), ctr) - mm(jnp.where(isf, zero, zi), cti)
    blocks = [mall_f[(15 - q) * CHUNK:(16 - q) * CHUNK] for q in range(15)]
    blocks.append(mall_f[0:CHUNK] + mall_b[0:CHUNK])
    blocks += [mall_b[l * CHUNK:(l + 1) * CHUNK] for l in range(1, 16)]
    blocks.append(jnp.zeros((CHUNK, 256), F32))
    gw = jnp.concatenate(blocks, axis=0)
    colblk = lax.broadcasted_iota(jnp.int32, (256, 256), 1) // CHUNK
    kmat = jnp.zeros((256, 256), F32)
    for tp in range(CHUNK):
        kmat = jnp.where(colblk == tp, gw[(15 - tp) * CHUNK:(15 - tp) * CHUNK + 256], kmat)
    k_ref[0] = kmat.astype(BF16)

    wr, wi = powers(jnp.where(isf, 15 - rowblk, rowblk))
    p_ref[0] = jnp.concatenate([wr * bbr - wi * bbi, wr * bbi + wi * bbr], axis=1).astype(BF16)

    arc = arc_ref[0]
    aic = aic_ref[0]
    dtc = jnp.exp(ldtc_ref[0])
    laneblk = lax.broadcasted_iota(jnp.int32, (LANES, 256), 1) // CHUNK
    rowf = lax.broadcasted_iota(jnp.int32, (LANES, 256), 0) < 64
    magc = jnp.exp(dtc * arc)
    lrc = magc * jnp.cos(dtc * aic)
    lic = magc * jnp.sin(dtc * aic)
    vr, vi = _cpow(lrc, lic, jnp.where(rowf, laneblk, 15 - laneblk))
    wr = vr * lrc - vi * lic
    wi = vr * lic + vi * lrc
    q_ref[0] = jnp.concatenate([ctr * wr - cti * wi, -(ctr * wi + cti * wr)], axis=0).astype(BF16)

    pr, pi = lr, li
    for _ in range(4):
        pr, pi = pr * pr - pi * pi, 2.0 * pr * pi
    l_ref[0] = jnp.concatenate([pr, pi], axis=1)


def _s5prep(a_re, a_im, log_dt, b_re, b_im, c_re, c_im):
    G = S5_GROUPS

    def rows(x):
        return jnp.concatenate([x[0], x[1]], axis=-1).reshape(G, 1, LANES)

    ldt = jnp.broadcast_to(log_dt[:, :, None], (2, G, 64))
    ar, ai, ld = rows(a_re), rows(a_im), rows(ldt)
    arc, aic, ldc = (v.reshape(G, LANES, 1) for v in (ar, ai, ld))

    def btile(b):
        bt = jnp.tile(jnp.swapaxes(b, 2, 3), (1, 1, CHUNK, 1))
        return jnp.concatenate([bt[0], bt[1]], axis=-1)

    def ctile(c):
        ct = jnp.tile(jnp.swapaxes(c, 2, 3), (1, 1, 1, CHUNK))
        return jnp.concatenate([ct[0], ct[1]], axis=1)

    vec = pl.BlockSpec((1, 1, LANES), lambda g: (g, 0, 0))
    col = pl.BlockSpec((1, LANES, 1), lambda g: (g, 0, 0))
    bsp = pl.BlockSpec((1, 256, LANES), lambda g: (g, 0, 0))
    csp = pl.BlockSpec((1, LANES, 256), lambda g: (g, 0, 0))
    msp = pl.BlockSpec((1, 256, 256), lambda g: (g, 0, 0))
    return pl.pallas_call(
        _s5prep_kernel, grid=(G,),
        in_specs=[vec, vec, vec, col, col, col, bsp, bsp, csp, csp],
        out_specs=[msp, msp, msp, pl.BlockSpec((1, 1, 256), lambda g: (g, 0, 0))],
        out_shape=[jax.ShapeDtypeStruct((G, 256, 256), BF16)] * 3 + [jax.ShapeDtypeStruct((G, 1, 256), F32)],
        compiler_params=_cparams(32), name="s5prep")(
            ar, ai, ld, arc, aic, ldc, btile(b_re), btile(b_im), ctile(c_re), ctile(c_im))


def _seg_transpose8(vs, seg):
    vs = list(vs)
    for d in (4, 2, 1):
        hi = (seg & d) != 0
        new = list(vs)
        for i in range(8):
            if i & d:
                continue
            j = i + d
            new[i] = jnp.where(hi, pltpu.roll(vs[j], d * CHUNK, axis=1), vs[i])
            new[j] = jnp.where(hi, vs[j], pltpu.roll(vs[i], LANES - d * CHUNK, axis=1))
        vs = new
    return vs


def _s5_kernel(u_ref, k_ref, p_ref, q_ref, l_ref, h0_ref, y_ref, fin_ref, ebuf, ybuf, *, A, Bq, B, lat):
    RP = A * Bq
    nchunk = RP // B
    G = GROUPS_PER_SLAB
    nr = 256
    qn = min(Bq, nr)
    tiles = [(r0, [((r0 + i * qn) // Bq, (r0 + i * qn) % Bq) for i in range(nr // qn)])
             for r0 in range(0, RP, nr)]
    lane_blk = lax.broadcasted_iota(jnp.int32, (nr, LANES), 1) // CHUNK

    for r0, pieces in tiles:
        xs = [jnp.concatenate([u_ref[0, a, tau, q0:q0 + qn, :] for a, q0 in pieces], axis=0)
              for tau in range(CHUNK)]
        halves = [_seg_transpose8(xs[8 * h:8 * h + 8], lane_blk) for h in range(2)]
        for g in range(G):
            ug = jnp.concatenate([halves[0][g], halves[1][g]], axis=1).astype(BF16)
            ybuf[g, r0:r0 + nr, :] = _dot(ug, k_ref[g])
            ebuf[g, r0:r0 + nr, :] = _dot(ug, p_ref[g])

    def off(m):
        if lat:
            return ((m & 1) * GRID_W + (m >> 1)) * B
        return m * B

    fmask = (lax.broadcasted_iota(jnp.int32, (B, 2 * LANES), 1) % LANES) < 64
    fmask1 = lax.broadcasted_iota(jnp.int32, (B, LANES), 1) < 64
    lam = [l_ref[g] for g in range(G)]

    def body(i, carry):
        of = pl.multiple_of(off(i), B)
        ob = pl.multiple_of(off(nchunk - 1 - i), B)
        out = []
        for g in range(G):
            re, im = carry[2 * g], carry[2 * g + 1]
            ef = ebuf[g, pl.ds(of, B), :]
            eb = ebuf[g, pl.ds(ob, B), :]
            s = jnp.concatenate([re, im], axis=1)
            ebuf[g, pl.ds(of, B), :] = jnp.where(fmask, s, ef)
            ebuf[g, pl.ds(ob, B), :] = jnp.where(fmask, eb, s)
            lr = lam[g][:, :LANES]
            li = lam[g][:, LANES:]
            e_re = jnp.where(fmask1, ef[:, :LANES], eb[:, :LANES])
            e_im = jnp.where(fmask1, ef[:, LANES:], eb[:, LANES:])
            out.append(lr * re - li * im + e_re)
            out.append(lr * im + li * re + e_im)
        return tuple(out)

    init = []
    for g in range(G):
        h0 = h0_ref[g]
        init += [h0[:, :LANES], h0[:, LANES:]]
    fin = lax.fori_loop(0, nchunk, body, tuple(init))
    for g in range(G):
        fin_ref[g] = jnp.concatenate([fin[2 * g], fin[2 * g + 1]], axis=1)

    for g in range(G):
        ybuf[g] = ybuf[g] + _dot(ebuf[g].astype(BF16), q_ref[g])
    for r0, pieces in tiles:
        for h in range(2):
            yt = _seg_transpose8([ybuf[g, r0:r0 + nr, h * LANES:(h + 1) * LANES] for g in range(G)], lane_blk)
            for tp in range(8):
                for k, (a, q0) in enumerate(pieces):
                    y_ref[0, a, 8 * h + tp, q0:q0 + qn, :] = yt[tp][k * qn:(k + 1) * qn]


def _s5(proj5, kmat, pmat, qmat, lam16, h0, A, Bq, B, lat, name):
    RP = A * Bq
    G = GROUPS_PER_SLAB
    kern = functools.partial(_s5_kernel, A=A, Bq=Bq, B=B, lat=lat)
    msp = pl.BlockSpec((G, 256, 256), lambda s: (s, 0, 0))
    return pl.pallas_call(
        kern, grid=(S5_SLABS,),
        in_specs=[pl.BlockSpec((1, A, CHUNK, Bq, LANES), lambda s: (s + 2 * RG_SLABS, 0, 0, 0, 0)),
                  msp, msp, msp,
                  pl.BlockSpec((G, 1, 256), lambda s: (s, 0, 0)),
                  pl.BlockSpec((G, B, 256), lambda s: (s, 0, 0))],
        out_specs=[pl.BlockSpec((1, A, CHUNK, Bq, LANES), lambda s: (s, 0, 0, 0, 0)),
                   pl.BlockSpec((G, B, 256), lambda s: (s, 0, 0))],
        out_shape=[jax.ShapeDtypeStruct((S5_SLABS, A, CHUNK, Bq, LANES), F32),
                   jax.ShapeDtypeStruct((S5_GROUPS, B, 256), F32)],
        scratch_shapes=[pltpu.VMEM((G, RP, 256), F32), pltpu.VMEM((G, RP, 256), F32)],
        compiler_params=_cparams(58), name=name)(proj5, kmat, pmat, qmat, lam16, h0)


def _post_kernel(yr_ref, ys_ref, u_ref, x_ref, mod_ref, wo_ref, wg_ref, bgl_ref, d_ref, grg_ref, gs5_ref,
                 o_ref, *, B, tT):
    yr = jnp.concatenate([yr_ref[s] for s in range(RG_SLABS)], axis=1)
    nr = (_rms(yr) * grg_ref[...]).astype(BF16)
    ys = jnp.concatenate([ys_ref[s] for s in range(S5_SLABS)], axis=1)
    u = jnp.concatenate([u_ref[s] for s in range(S5_SLABS)], axis=1)
    ys = jax.nn.gelu(ys + d_ref[...] * u)
    ys = ys * jax.nn.sigmoid(_dot(ys.astype(BF16), wg_ref[...]) + bgl_ref[...])
    ns = (_rms(ys) * gs5_ref[...]).astype(BF16)
    y = _dot(nr, wo_ref[0:D_RG, :]) + _dot(ns, wo_ref[D_RG:D_MODEL, :])
    nj = D_MODEL // LANES
    cols = [_rows_tb_to_bt(y[:, j * LANES:(j + 1) * LANES], B, tT) for j in range(nj)]
    for b in range(B):
        yb = jnp.concatenate([c[b * tT:(b + 1) * tT] for c in cols], axis=1)
        o_ref[b] = x_ref[b] + mod_ref[b, 2:3, :] * yb


def _post(y_rg, y_ssm, proj, x, mod, w_out_bf16, w_glu_bf16, b_glu, s5_d, g_rg, g_s5, name):
    B, T, _ = x.shape
    tT = 512 // B
    rows = tT * B
    kern = functools.partial(_post_kernel, B=B, tT=tT)
    const = lambda i: (0, 0)
    return pl.pallas_call(
        kern, grid=(T // tT,),
        in_specs=[pl.BlockSpec((RG_SLABS, rows, LANES), lambda i: (0, i, 0)),
                  pl.BlockSpec((S5_SLABS, rows, LANES), lambda i: (0, i, 0)),
                  pl.BlockSpec((S5_SLABS, rows, LANES), lambda i: (2 * RG_SLABS // S5_SLABS, i, 0)),
                  pl.BlockSpec((B, tT, D_MODEL), lambda i: (0, i, 0)),
                  pl.BlockSpec((B, N_MOD, D_MODEL), lambda i: (0, 0, 0)),
                  pl.BlockSpec((D_MODEL, D_MODEL), const, pipeline_mode=pl.Buffered(1)),
                  pl.BlockSpec((D_S5, D_S5), const),
                  pl.BlockSpec((1, D_S5), const),
                  pl.BlockSpec((1, D_S5), const),
                  pl.BlockSpec((1, D_RG), const),
                  pl.BlockSpec((1, D_S5), const)],
        out_specs=pl.BlockSpec((B, tT, D_MODEL), lambda i: (0, i, 0)),
        out_shape=jax.ShapeDtypeStruct((B, T, D_MODEL), F32),
        compiler_params=_cparams(56), name=name)(
            y_rg, y_ssm, proj, x, mod, w_out_bf16, w_glu_bf16, b_glu, s5_d, g_rg, g_s5)


def _premoe_kernel(x_ref, mod_ref, g_ref, wr_ref, h_ref, a_ref):
    h = _rms(x_ref[...]) * g_ref[...] * (1.0 + mod_ref[0, 4:5, :]) + mod_ref[0, 3:4, :]
    hhi = h.astype(BF16)
    h_ref[...] = hhi
    hlo = (h - hhi.astype(F32)).astype(BF16)
    w = wr_ref[...]
    whi = w.astype(BF16)
    wlo = (w - whi.astype(F32)).astype(BF16)
    nt = lambda a, b: lax.dot_general(a, b, (((1,), (1,)), ((), ())), preferred_element_type=F32)
    lg = nt(whi, hhi) + nt(wlo, hhi) + nt(whi, hlo)
    e = jnp.exp(lg - jnp.max(lg, axis=0, keepdims=True))
    a_ref[...] = e / jnp.sum(e, axis=0, keepdims=True)


def _premoe(x1, mod, g2, w_router_t, T, name):
    n = x1.shape[0]
    rows = 512
    per_req = T // rows if mod.shape[0] > 1 else n
    return pl.pallas_call(
        _premoe_kernel, grid=(n // rows,),
        in_specs=[pl.BlockSpec((rows, D_MODEL), lambda i: (i, 0)),
                  pl.BlockSpec((1, N_MOD, D_MODEL), lambda i: (i // per_req, 0, 0)),
                  pl.BlockSpec((1, D_MODEL), lambda i: (0, 0)),
                  pl.BlockSpec((N_EXPERTS, D_MODEL), lambda i: (0, 0))],
        out_specs=[pl.BlockSpec((rows, D_MODEL), lambda i: (i, 0)),
                   pl.BlockSpec((N_EXPERTS, rows), lambda i: (0, i))],
        out_shape=[jax.ShapeDtypeStruct((n, D_MODEL), BF16),
                   jax.ShapeDtypeStruct((N_EXPERTS, n), F32)],
        compiler_params=_cparams(40), name=name)(x1, mod, g2, w_router_t)


def _topk_kernel(a_ref, s_ref, st_ref, c_ref, *, cap, T, nreq):
    capf = float(cap)

    def count_ge(r, v):
        return jnp.sum((a_ref[:, r * T:(r + 1) * T] >= v).astype(F32), axis=1, keepdims=True)

    def bit_step(i, thrs):
        bit = jnp.int32(1) << (30 - i)
        out = []
        for r in range(nreq):
            cand = thrs[r] | bit
            ok = count_ge(r, pltpu.bitcast(cand, F32)) >= capf
            out.append(jnp.where(ok, cand, thrs[r]))
        return tuple(out)

    zero = jnp.zeros((N_EXPERTS, 1), jnp.int32)
    thrs = lax.fori_loop(0, 31, bit_step, (zero,) * nreq)

    def mid_step(i, c):
        out = []
        for r in range(nreq):
            lo, hi = c[2 * r], c[2 * r + 1]
            mid = 0.5 * (lo + hi)
            ok = count_ge(r, mid) >= capf
            out += [jnp.where(ok, mid, lo), jnp.where(ok, hi, mid)]
        return tuple(out)

    init = []
    for r in range(nreq):
        init += [pltpu.bitcast(thrs[r], F32), pltpu.bitcast(thrs[r] + 1, F32)]
    bounds = lax.fori_loop(0, 24, mid_step, tuple(init))

    u = (lax.broadcasted_iota(jnp.int32, (T, T), 0)
         <= lax.broadcasted_iota(jnp.int32, (T, T), 1)).astype(F32).astype(BF16)
    nblk = T // TOKEN_BLOCK
    cl = lax.broadcasted_iota(jnp.int32, c_ref.shape, 1)
    cnt = jnp.zeros(c_ref.shape, F32)
    for r in range(nreq):
        aff = a_ref[:, r * T:(r + 1) * T]
        lo, hi = bounds[2 * r], bounds[2 * r + 1]
        gt = aff >= hi
        eq = (aff >= lo) & jnp.logical_not(gt)
        need = capf - jnp.sum(gt.astype(F32), axis=1, keepdims=True)
        pe = _dot(eq.astype(F32).astype(BF16), u)
        sel = gt | (eq & (pe <= need))
        ps = _dot(sel.astype(F32).astype(BF16), u)
        slot = jnp.where(sel, ps - 1.0, -1.0)
        s_ref[:, r * T:(r + 1) * T] = slot
        st_ref[r * T:(r + 1) * T, :] = slot.T
        for k in range(nblk):
            end = (k + 1) * TOKEN_BLOCK
            cnt = jnp.where(cl == r * nblk + k, ps[:, end - 1:end], cnt)
    c_ref[...] = cnt


def _topk(aff_t, T, name):
    n = aff_t.shape[1]
    nreq = n // T
    ncnt = nreq * (T // TOKEN_BLOCK)
    kern = functools.partial(_topk_kernel, cap=T // 8, T=T, nreq=nreq)
    return pl.pallas_call(
        kern, grid=(1,),
        in_specs=[pl.BlockSpec((N_EXPERTS, n), lambda i: (0, 0))],
        out_specs=[pl.BlockSpec((N_EXPERTS, n), lambda i: (0, 0)),
                   pl.BlockSpec((n, N_EXPERTS), lambda i: (0, 0)),
                   pl.BlockSpec((N_EXPERTS, ncnt), lambda i: (0, 0))],
        out_shape=[jax.ShapeDtypeStruct((N_EXPERTS, n), F32),
                   jax.ShapeDtypeStruct((n, N_EXPERTS), F32),
                   jax.ShapeDtypeStruct((N_EXPERTS, ncnt), F32)],
        compiler_params=_cparams(48), name=name)(aff_t)


def _window_starts(start_ref, nblk):
    base = (pl.program_id(0) * nblk + pl.program_id(1)) * N_EXPERTS
    return [pl.multiple_of(start_ref[base + e], 16) for e in range(N_EXPERTS)]


def _gather_kernel(start_ref, ok_ref, h_ref, s_ref, a_ref, xg_ref, gt_ref, *, cap, W, nblk):
    @pl.when(pl.program_id(1) == 0)
    def _():
        xg_ref[...] = jnp.zeros_like(xg_ref)
        gt_ref[...] = jnp.zeros_like(gt_ref)

    h = h_ref[...]
    nch = 512

    def step(w, starts):
        j = lax.broadcasted_iota(jnp.int32, (w, TOKEN_BLOCK), 0).astype(F32)
        ohs = []
        for e in range(N_EXPERTS):
            rel = s_ref[e:e + 1, :]
            if starts is not None:
                rel = rel - starts[e].astype(F32)
            oh = rel == j
            rows = pl.ds(starts[e], w) if starts is not None else slice(0, w)
            gt_ref[e, rows, :] += jnp.sum(jnp.where(oh, a_ref[e:e + 1, :], 0.0), axis=1, keepdims=True)
            ohs.append(oh.astype(F32).astype(BF16))
        ohm = jnp.concatenate(ohs, axis=0)
        for c in range(0, D_MODEL, nch):
            res = _dot(ohm, h[:, c:c + nch])
            for e in range(N_EXPERTS):
                rows = pl.ds(starts[e], w) if starts is not None else slice(0, w)
                xg_ref[e, rows, c:c + nch] += res[e * w:(e + 1) * w].astype(BF16)

    if W == cap:
        step(cap, None)
    else:
        ok = ok_ref[pl.program_id(0)] > 0

        @pl.when(ok)
        def _():
            step(W, _window_starts(start_ref, nblk))

        @pl.when(jnp.logical_not(ok))
        def _():
            step(cap, None)


def _gather(h2, slot, aff_t, starts, ok, T, name):
    n = h2.shape[0]
    cap = T // 8
    nreq = n // T
    nblk = T // TOKEN_BLOCK
    kern = functools.partial(_gather_kernel, cap=cap, W=min(SLOT_WINDOW, cap), nblk=nblk)
    tok = lambda b, k, *_: (0, b * nblk + k)
    req = lambda b, k, *_: (0, b, 0)
    return pl.pallas_call(
        kern,
        grid_spec=pltpu.PrefetchScalarGridSpec(
            num_scalar_prefetch=2, grid=(nreq, nblk),
            in_specs=[pl.BlockSpec((TOKEN_BLOCK, D_MODEL), lambda b, k, *_: (b * nblk + k, 0)),
                      pl.BlockSpec((N_EXPERTS, TOKEN_BLOCK), tok),
                      pl.BlockSpec((N_EXPERTS, TOKEN_BLOCK), tok)],
            out_specs=[pl.BlockSpec((N_EXPERTS, cap, D_MODEL), req),
                       pl.BlockSpec((N_EXPERTS, cap, 1), req)]),
        out_shape=[jax.ShapeDtypeStruct((N_EXPERTS, nreq * cap, D_MODEL), BF16),
                   jax.ShapeDtypeStruct((N_EXPERTS, nreq * cap, 1), F32)],
        compiler_params=_cparams(56), name=name)(starts, ok, h2, slot, aff_t)


def _ffn_kernel(xc_ref, xl_ref, gc_ref, gl_ref, w1_ref, w3_ref, w2_ref, yc_ref, yl_ref, xbuf, acc, *, rc, nf, rsub):
    f = pl.program_id(2)

    @pl.when(f == 0)
    def _():
        xbuf[0:rc, :] = xc_ref[0]
        xbuf[rc:, :] = xl_ref[0]

    @pl.when(f == 0)
    def _():
        acc[...] = jnp.zeros_like(acc)

    w1 = w1_ref[0].astype(BF16)
    w3 = w3_ref[0].astype(BF16)
    w2 = w2_ref[0].astype(BF16)
    rows = xbuf.shape[0]
    for r0 in range(0, rows, rsub):
        x = xbuf[r0:r0 + rsub, :]
        hid = jax.nn.silu(_dot(x, w1)) * _dot(x, w3)
        acc[r0:r0 + rsub, :] += _dot(hid.astype(BF16), w2)

    @pl.when(f == nf - 1)
    def _():
        yc_ref[0] = (acc[0:rc, :] * gc_ref[0]).astype(BF16)
        yl_ref[0] = (acc[rc:, :] * gl_ref[0]).astype(BF16)


def _ffn(xg_c, xg_l, gt_c, gt_l, w1, w3, w2):
    rc = xg_c.shape[1] // 2
    rl = xg_l.shape[1] // 2
    bf = 256
    nf = D_MODEL // bf
    kern = functools.partial(_ffn_kernel, rc=rc, nf=nf, rsub=(rc + rl) // 2)
    rowsp = lambda r, w: pl.BlockSpec((1, r, w), lambda e, h, f: (e, h, 0))
    return pl.pallas_call(
        kern, grid=(N_EXPERTS, 2, nf),
        in_specs=[rowsp(rc, D_MODEL), rowsp(rl, D_MODEL), rowsp(rc, 1), rowsp(rl, 1),
                  pl.BlockSpec((1, D_MODEL, bf), lambda e, h, f: (e, 0, f)),
                  pl.BlockSpec((1, D_MODEL, bf), lambda e, h, f: (e, 0, f)),
                  pl.BlockSpec((1, bf, D_MODEL), lambda e, h, f: (e, f, 0))],
        out_specs=[rowsp(rc, D_MODEL), rowsp(rl, D_MODEL)],
        out_shape=[jax.ShapeDtypeStruct(xg_c.shape, BF16), jax.ShapeDtypeStruct(xg_l.shape, BF16)],
        scratch_shapes=[pltpu.VMEM((rc + rl, D_MODEL), BF16), pltpu.VMEM((rc + rl, D_MODEL), F32)],
        compiler_params=_cparams(56), name="ffn")(xg_c, xg_l, gt_c, gt_l, w1, w3, w2)


def _expand_consts(w):
    lane = jnp.arange(N_EXPERTS * w)
    expand = (jnp.arange(N_EXPERTS)[:, None] == (lane // w)[None, :]).astype(BF16)
    return expand, (lane % w).astype(F32)[None, :]


def _combine_kernel(start_ref, ok_ref, x_ref, y_ref, st_ref, sv_ref, mod_ref, fg_ref, exw_ref, jlw_ref,
                    exd_ref, jld_ref, o_ref, stage, *, cap, W, nblk):
    st = st_ref[...]

    def finish(moe):
        x2 = x_ref[...] + mod_ref[0, 5:6, :] * moe
        o_ref[...] = _rms(x2) * fg_ref[...]

    def dense():
        rep = _dot(st.astype(BF16), exd_ref[...])
        oh = (rep == jld_ref[...]).astype(F32).astype(BF16)
        finish(_dot(oh, y_ref[...].reshape(N_EXPERTS * cap, D_MODEL)))

    if W == cap:
        dense()
        return
    ok = ok_ref[pl.program_id(0)] > 0

    @pl.when(ok)
    def _():
        starts = _window_starts(start_ref, nblk)
        for e in range(N_EXPERTS):
            stage[e * W:(e + 1) * W, :] = y_ref[e, pl.ds(starts[e], W), :]
        rep = _dot((st - sv_ref[0]).astype(BF16), exw_ref[...])
        oh = (rep == jlw_ref[...]).astype(F32).astype(BF16)
        finish(_dot(oh, stage[...]))

    @pl.when(jnp.logical_not(ok))
    def _():
        dense()


def _combine(x1, ybuf, slot_t, starts, ok, starts_f, mod, final_g, T, name):
    n = x1.shape[0]
    cap = T // 8
    nblk = T // TOKEN_BLOCK
    W = min(SLOT_WINDOW, cap)
    multi = mod.shape[0] > 1
    exw, jlw = _expand_consts(W)
    exd, jld = _expand_consts(cap)
    kern = functools.partial(_combine_kernel, cap=cap, W=W, nblk=nblk)
    tok = lambda b, k, *_: (b * nblk + k, 0)
    const = lambda b, k, *_: (0, 0)
    return pl.pallas_call(
        kern,
        grid_spec=pltpu.PrefetchScalarGridSpec(
            num_scalar_prefetch=2, grid=(n // T, nblk),
            in_specs=[pl.BlockSpec((TOKEN_BLOCK, D_MODEL), tok),
                      pl.BlockSpec((N_EXPERTS, cap, D_MODEL), lambda b, k, *_: (0, b, 0)),
                      pl.BlockSpec((TOKEN_BLOCK, N_EXPERTS), tok),
                      pl.BlockSpec((1, 1, N_EXPERTS), lambda b, k, *_: (b * nblk + k, 0, 0)),
                      pl.BlockSpec((1, N_MOD, D_MODEL), lambda b, k, *_: (b if multi else 0, 0, 0)),
                      pl.BlockSpec((1, D_MODEL), const),
                      pl.BlockSpec(exw.shape, const), pl.BlockSpec(jlw.shape, const),
                      pl.BlockSpec(exd.shape, const), pl.BlockSpec(jld.shape, const)],
            out_specs=pl.BlockSpec((TOKEN_BLOCK, D_MODEL), tok),
            scratch_shapes=[pltpu.VMEM((N_EXPERTS * W, D_MODEL), BF16)]),
        out_shape=jax.ShapeDtypeStruct((n, D_MODEL), F32),
        compiler_params=_cparams(56), name=name)(starts, ok, x1, ybuf, slot_t, starts_f, mod, final_g,
                                                 exw, jlw, exd, jld)


def _window_tables(cnt, T):
    cap = T // 8
    nblk = T // TOKEN_BLOCK
    nreq = cnt.shape[1] // nblk
    W = min(SLOT_WINDOW, cap)
    end = cnt.astype(jnp.int32).reshape(N_EXPERTS, nreq, nblk)
    first = jnp.concatenate([jnp.zeros_like(end[..., :1]), end[..., :-1]], axis=-1)
    start = jnp.minimum((first // 16) * 16, cap - W)
    ok = jnp.all(end - start <= W, axis=(0, 2)).astype(jnp.int32)
    start = start.transpose(1, 2, 0)
    return start.reshape(-1), ok, start.astype(F32).reshape(nreq * nblk, 1, N_EXPERTS)


def _blockdiag_pairs(w):
    w = w.reshape(2, RG_SLABS, 2, 64, 64)
    bd = jnp.einsum('dshij,hk->dshikj', w, jnp.eye(2, dtype=w.dtype))
    return bd.reshape(2, RG_SLABS, LANES, LANES).transpose(1, 0, 2, 3)


def kernel(x_prompt, x_sample, state_rglru, state_s5_re, state_s5_im, c, c_ctx, norm1_g, norm2_g, w_ada, b_ada, w_in, rg_conv_w, rg_conv_b, rg_wa, rg_ba, rg_wx, rg_bx, rg_lam, s5_a_re, s5_a_im, s5_log_dt, s5_b_re, s5_b_im, s5_c_re, s5_c_im, s5_d, s5_w_glu, s5_b_glu, gnorm_rg, gnorm_s5, w_out, w_router, w1, w3, w2, final_g):
    l = 0
    Bp, Tp, _ = x_prompt.shape
    Bs, Ts, _ = x_sample.shape

    c16 = jnp.concatenate([c, c_ctx[None], jnp.zeros((16 - Bs - 1, D_MODEL), F32)], axis=0)
    mod = _adaln(c16, w_ada[l], b_ada[l]).reshape(16, N_MOD, D_MODEL)
    mod_lat = mod[:Bs]
    mod_ctx1 = mod[Bs:Bs + 1]
    mod_ctx = jnp.broadcast_to(mod_ctx1, (Bp, N_MOD, D_MODEL))

    w_in_b = w_in[l].astype(BF16)
    w_out_b = w_out[l].astype(BF16)
    w_glu_b = s5_w_glu[l].astype(BF16)
    row = lambda v: v.reshape(1, -1)

    wg = (0.5 * jnp.concatenate([_blockdiag_pairs(rg_wa[l]), _blockdiag_pairs(rg_wx[l])], axis=-1)).astype(BF16)
    bg = 0.5 * jnp.concatenate([rg_ba[l].reshape(2, RG_SLABS, LANES), rg_bx[l].reshape(2, RG_SLABS, LANES)],
                               axis=-1).transpose(1, 0, 2).reshape(RG_SLABS, 2, 1, 2 * LANES)
    kmat, pmat, qmat, lam16 = _s5prep(s5_a_re[l], s5_a_im[l], s5_log_dt[l], s5_b_re[l], s5_b_im[l],
                                       s5_c_re[l], s5_c_im[l])

    def mixer(x, modp, rg_h0, s5_h0, lat, tag):
        B, T, _ = x.shape
        R = T * B
        proj = _inproj(x, modp, row(norm1_g[l]), w_in_b, "inproj_" + tag)
        y_rg, fin_rg = _rglru(proj, rg_conv_w[l], row(rg_conv_b[l]), wg, bg, rg_lam[l], rg_h0, B, T,
                              "rglru_" + tag)
        if lat:
            A, Bq = T // GRID_W // CHUNK, GRID_W * B
        else:
            A, Bq = T // CHUNK, B
        y_ssm, fin_s5 = _s5(proj.reshape(N_SLAB, A, CHUNK, Bq, LANES), kmat, pmat, qmat, lam16, s5_h0,
                            A, Bq, B, lat, "s5_" + tag)
        x1 = _post(y_rg, y_ssm.reshape(S5_SLABS, R, LANES), proj, x, modp, w_out_b, w_glu_b,
                   row(s5_b_glu[l]), row(s5_d[l]), row(gnorm_rg[l]), row(gnorm_s5[l]), "post_" + tag)
        return x1.reshape(R, D_MODEL), fin_rg, fin_s5

    z_rg = jnp.zeros((2, Bp, D_RG), F32)
    z_s5 = jnp.zeros((S5_GROUPS, Bp, 256), F32)
    x1_c, fin_rg, fin_s5 = mixer(x_prompt, mod_ctx, z_rg, z_s5, False, "ctx")
    rg_h0 = state_rglru[:, l].transpose(1, 0, 2)
    s5_h0 = jnp.stack([state_s5_re[:, l, 0], state_s5_re[:, l, 1], state_s5_im[:, l, 0], state_s5_im[:, l, 1]],
                      axis=0)
    s5_h0 = s5_h0.transpose(2, 1, 0, 3).reshape(S5_GROUPS, Bs, 256)
    x1_s, _, _ = mixer(x_sample, mod_lat, rg_h0, s5_h0, True, "lat")

    w_router_t = w_router[l].T
    g2 = row(norm2_g[l])

    def route(x1, modp, T, tag):
        h2, aff_t = _premoe(x1, modp, g2, w_router_t, T, "premoe_" + tag)
        slot, slot_t, cnt = _topk(aff_t, T, "topk_" + tag)
        win = _window_tables(cnt, T)
        xg, gates = _gather(h2, slot, aff_t, win[0], win[1], T, "gather_" + tag)
        return xg, gates, slot_t, win

    xg_c, gt_c, st_c, win_c = route(x1_c, mod_ctx1, Tp, "ctx")
    xg_s, gt_s, st_s, win_s = route(x1_s, mod_lat, Ts, "lat")
    y_c, y_s = _ffn(xg_c, xg_s, gt_c, gt_s, w1[l], w3[l], w2[l])
    fg = row(final_g)
    y_prompt = _combine(x1_c, y_c, st_c, *win_c, mod_ctx1, fg, Tp, "combine_ctx").reshape(Bp, Tp, D_MODEL)
    y_sample = _combine(x1_s, y_s, st_s, *win_s, mod_lat, fg, Ts, "combine_lat").reshape(Bs, Ts, D_MODEL)

    new_rg = fin_rg.transpose(1, 0, 2)[:, None]
    f4 = fin_s5.reshape(S5_GROUPS, Bp, 4, 64).transpose(1, 2, 0, 3)
    new_re = f4[:, None, 0:2]
    new_im = f4[:, None, 2:4]
    return (y_prompt, y_sample, new_rg, new_re, new_im)
```

```python
import functools

import jax
import jax.numpy as jnp
from jax import lax
from jax.experimental import pallas as pl
from jax.experimental.pallas import tpu as pltpu

F32 = jnp.float32
BF16 = jnp.bfloat16
HIGHEST = lax.Precision.HIGHEST

LANES = 128
D_MODEL = 2048
D_RG = 1536
D_S5 = 512
D_IN = 2 * D_RG + D_S5
N_SLAB = D_IN // LANES
RG_SLABS = D_RG // LANES
S5_SLABS = D_S5 // LANES
S5_GROUPS = 32
GROUPS_PER_SLAB = 8
CHUNK = 16
N_EXPERTS = 16
N_MOD = 6
EPS = 1e-6
RG_C = 8.0
LOG2_E = 1.4426950408889634
GRID_W = 64
TOKEN_BLOCK = 256
SLOT_WINDOW = 80
MIB = 1024 * 1024


def _cparams(vmem_mib):
    return pltpu.CompilerParams(vmem_limit_bytes=vmem_mib * MIB)


def _dot(a, b):
    return jnp.dot(a, b, preferred_element_type=F32)


def _rms(x):
    return x * lax.rsqrt(jnp.mean(x * x, axis=-1, keepdims=True) + EPS)


def _adaln_kernel(c_ref, w_ref, b_ref, o_ref):
    s = jax.nn.silu(c_ref[...]).astype(BF16)
    o_ref[...] = _dot(s, w_ref[...].astype(BF16)) + b_ref[...]


def _adaln(c16, w_ada, b_ada):
    n = w_ada.shape[1]
    bn = 1024
    return pl.pallas_call(
        _adaln_kernel, grid=(n // bn,),
        in_specs=[pl.BlockSpec((16, D_MODEL), lambda j: (0, 0)),
                  pl.BlockSpec((D_MODEL, bn), lambda j: (0, j)),
                  pl.BlockSpec((1, bn), lambda j: (0, j))],
        out_specs=pl.BlockSpec((16, bn), lambda j: (0, j)),
        out_shape=jax.ShapeDtypeStruct((16, n), F32),
        compiler_params=_cparams(40), name="adaln")(c16, w_ada, b_ada.reshape(1, n))


def _sublane_transpose8(vs, sub):
    vs = list(vs)
    for d in (4, 2, 1):
        hi = (sub & d) != 0
        new = list(vs)
        for i in range(8):
            if i & d:
                continue
            new[i] = jnp.where(hi, pltpu.roll(vs[i + d], d, axis=1), vs[i])
            new[i + d] = jnp.where(hi, vs[i + d], pltpu.roll(vs[i], 8 - d, axis=1))
        vs = new
    return vs


def _rows_bt_to_tb(p, B, tT):
    sub = lax.broadcasted_iota(jnp.int32, (tT // 8, 8, LANES), 1)
    groups = []
    for g in range(B // 8):
        vs = [p[(8 * g + i) * tT:(8 * g + i + 1) * tT].reshape(tT // 8, 8, LANES) for i in range(8)]
        groups.append(_sublane_transpose8(vs, sub))
    out = jnp.stack([jnp.stack([grp[a] for grp in groups], axis=1) for a in range(8)], axis=1)
    return out.reshape(tT * B, LANES)


def _rows_tb_to_bt(y, B, tT):
    ng = B // 8
    y4 = y.reshape(tT // 8, 8 * ng, 8, LANES)
    sub = lax.broadcasted_iota(jnp.int32, (tT // 8, 8, LANES), 1)
    out = []
    for g in range(ng):
        out += _sublane_transpose8([y4[:, a * ng + g] for a in range(8)], sub)
    return jnp.concatenate([v.reshape(tT, LANES) for v in out], axis=0)


def _inproj_kernel(x_ref, mod_ref, g_ref, w_ref, o_ref, *, B, tT, nch):
    x = x_ref[...]
    shift = mod_ref[:, 0:1, :]
    scale = mod_ref[:, 1:2, :]
    h = _rms(x) * g_ref[...] * (1.0 + scale) + shift
    hb = h.reshape(B * tT, D_MODEL).astype(BF16)
    for c0 in range(0, D_IN, nch):
        p = _dot(hb, w_ref[:, c0:c0 + nch])
        if D_RG <= c0 < 2 * D_RG:
            p = jax.nn.gelu(p)
        for j in range(nch // LANES):
            o_ref[c0 // LANES + j] = _rows_bt_to_tb(p[:, j * LANES:(j + 1) * LANES], B, tT)


def _inproj(x, mod, g1, w_in_bf16, name):
    B, T, _ = x.shape
    tT = 512 // B
    kern = functools.partial(_inproj_kernel, B=B, tT=tT, nch=512)
    return pl.pallas_call(
        kern, grid=(T // tT,),
        in_specs=[pl.BlockSpec((B, tT, D_MODEL), lambda i: (0, i, 0)),
                  pl.BlockSpec((B, N_MOD, D_MODEL), lambda i: (0, 0, 0)),
                  pl.BlockSpec((1, D_MODEL), lambda i: (0, 0)),
                  pl.BlockSpec((D_MODEL, D_IN), lambda i: (0, 0), pipeline_mode=pl.Buffered(1))],
        out_specs=pl.BlockSpec((N_SLAB, tT * B, LANES), lambda i: (0, i, 0)),
        out_shape=jax.ShapeDtypeStruct((N_SLAB, T * B, LANES), F32),
        compiler_params=_cparams(56), name=name)(x, mod, g1, w_in_bf16)


def _rglru_kernel(xb_ref, gt_ref, cw_ref, cb_ref, wg_ref, bg_ref, lam_ref, h0_ref, y_ref, fin_ref,
                  *, B, T, CT):
    RC = CT * B
    nc = T // CT
    cw = [cw_ref[k:k + 1, :] for k in range(4)]
    cb = cb_ref[...]
    hl = [(-0.5 * RG_C * LOG2_E) * jax.nn.softplus(-lam_ref[d:d + 1, :]) for d in range(2)]

    def chunk(n, d):
        t0 = n * CT
        row0 = pl.multiple_of(t0 * B, RC)
        cur = xb_ref[0, pl.ds(row0, RC), :]
        p0 = pl.multiple_of(jnp.maximum(t0 - 1, 0) * B, B)
        prev = xb_ref[0, pl.ds(p0, B), :] * jnp.where(t0 > 0, 1.0, 0.0)
        n0 = pl.multiple_of(jnp.minimum(t0 + CT, T - 2) * B, B)
        nxt = xb_ref[0, pl.ds(n0, 2 * B), :] * jnp.where(t0 + CT < T, 1.0, 0.0)
        ext = jnp.concatenate([prev, cur, nxt], axis=0)
        xc = (cw[0] * ext[0:RC] + cw[1] * ext[B:B + RC] + cw[2] * ext[2 * B:2 * B + RC]
              + cw[3] * ext[3 * B:3 * B + RC] + cb)
        th = jnp.tanh(_dot(xc.astype(BF16), wg_ref[0, d]) + bg_ref[0, d])
        i = 0.5 + 0.5 * th[:, LANES:]
        a = jnp.exp2(hl[d] + hl[d] * th[:, :LANES])
        om = 1.0 - a * a
        bb = jnp.where(om > 0.0, om * lax.rsqrt(om), 0.0) * (i * xc)
        return a, bb, row0

    def make_body(finish):
        def body(n, carry):
            hf, hb = carry
            af, bf, rf = chunk(n, 0)
            ab, bk, rb = chunk(nc - 1 - n, 1)
            if finish:
                glf = gt_ref[0, pl.ds(rf, RC), :]
                glb = gt_ref[0, pl.ds(rb, RC), :]
                of = y_ref[0, pl.ds(rf, RC), :]
                ob = y_ref[0, pl.ds(rb, RC), :]
            for t in range(CT):
                u = CT - 1 - t
                hf = af[t * B:(t + 1) * B] * hf + bf[t * B:(t + 1) * B]
                hb = ab[u * B:(u + 1) * B] * hb + bk[u * B:(u + 1) * B]
                rows_f = pl.ds(rf + t * B, B)
                rows_b = pl.ds(rb + u * B, B)
                if finish:
                    y_ref[0, rows_f, :] = (of[t * B:(t + 1) * B] + hf) * glf[t * B:(t + 1) * B]
                    y_ref[0, rows_b, :] = (ob[u * B:(u + 1) * B] + hb) * glb[u * B:(u + 1) * B]
                else:
                    y_ref[0, rows_f, :] = hf
                    y_ref[0, rows_b, :] = hb
            return hf, hb
        return body

    carry = lax.fori_loop(0, nc // 2, make_body(False), (h0_ref[0], h0_ref[1]))
    hf, hb = lax.fori_loop(nc // 2, nc, make_body(True), carry)
    fin_ref[0] = hf
    fin_ref[1] = hb


def _rglru(proj, cw, cb, wg, bg, lam, h0, B, T, name):
    R = T * B
    kern = functools.partial(_rglru_kernel, B=B, T=T, CT=512 // B)
    return pl.pallas_call(
        kern, grid=(RG_SLABS,),
        in_specs=[pl.BlockSpec((1, R, LANES), lambda s: (s, 0, 0)),
                  pl.BlockSpec((1, R, LANES), lambda s: (s + RG_SLABS, 0, 0)),
                  pl.BlockSpec((4, LANES), lambda s: (0, s)),
                  pl.BlockSpec((1, LANES), lambda s: (0, s)),
                  pl.BlockSpec((1, 2, LANES, 2 * LANES), lambda s: (s, 0, 0, 0)),
                  pl.BlockSpec((1, 2, 1, 2 * LANES), lambda s: (s, 0, 0, 0)),
                  pl.BlockSpec((2, LANES), lambda s: (0, s)),
                  pl.BlockSpec((2, B, LANES), lambda s: (0, 0, s))],
        out_specs=[pl.BlockSpec((1, R, LANES), lambda s: (s, 0, 0)),
                   pl.BlockSpec((2, B, LANES), lambda s: (0, 0, s))],
        out_shape=[jax.ShapeDtypeStruct((RG_SLABS, R, LANES), F32),
                   jax.ShapeDtypeStruct((2, B, D_RG), F32)],
        compiler_params=_cparams(58), name=name)(proj, proj, cw, cb, wg, bg, lam, h0)


def _cpow(pr, pi, n):
    wr = jnp.ones(n.shape, F32)
    wi = jnp.zeros(n.shape, F32)
    for k in range(4):
        bit = (n & (1 << k)) != 0
        wr, wi = jnp.where(bit, wr * pr - wi * pi, wr), jnp.where(bit, wr * pi + wi * pr, wi)
        pr, pi = pr * pr - pi * pi, 2.0 * pr * pi
    return wr, wi


def _s5prep_kernel(ar_ref, ai_ref, ldt_ref, arc_ref, aic_ref, ldtc_ref, btr_ref, bti_ref, ctr_ref, cti_ref,
                   k_ref, p_ref, q_ref, l_ref):
    ar = ar_ref[0]
    ai = ai_ref[0]
    dt = jnp.exp(ldt_ref[0])
    mag = jnp.exp(dt * ar)
    lr = mag * jnp.cos(dt * ai)
    li = mag * jnp.sin(dt * ai)
    den = ar * ar + ai * ai
    nre = lr - 1.0
    qre = (nre * ar + li * ai) / den
    qim = (li * ar - nre * ai) / den
    btr = btr_ref[0]
    bti = bti_ref[0]
    bbr = qre * btr - qim * bti
    bbi = qre * bti + qim * btr
    rowblk = lax.broadcasted_iota(jnp.int32, (256, LANES), 0) // CHUNK
    isf = lax.broadcasted_iota(jnp.int32, (256, LANES), 1) < 64

    def powers(n):
        return _cpow(lr, li, n)

    ctr = ctr_ref[0]
    cti = cti_ref[0]

    wr, wi = powers(rowblk)
    zr = wr * bbr - wi * bbi
    zi = wr * bbi + wi * bbr
    zero = jnp.zeros_like(zr)

    def mm(a, b):
        return jnp.dot(a, b, precision=HIGHEST, preferred_element_type=F32)

    mall_f = mm(jnp.where(isf, zr, zero), ctr) - mm(jnp.where(isf, zi, zero), cti)
    mall_b = mm(jnp.where(isf, zero, zr), ctr) - mm(jnp.where(isf, zero, zi), cti)
    blocks = [mall_f[(15 - q) * CHUNK:(16 - q) * CHUNK] for q in range(15)]
    blocks.append(mall_f[0:CHUNK] + mall_b[0:CHUNK])
    blocks += [mall_b[l * CHUNK:(l + 1) * CHUNK] for l in range(1, 16)]
    blocks.append(jnp.zeros((CHUNK, 256), F32))
    gw = jnp.concatenate(blocks, axis=0)
    colblk = lax.broadcasted_iota(jnp.int32, (256, 256), 1) // CHUNK
    kmat = jnp.zeros((256, 256), F32)
    for tp in range(CHUNK):
        kmat = jnp.where(colblk == tp, gw[(15 - tp) * CHUNK:(15 - tp) * CHUNK + 256], kmat)
    k_ref[0] = kmat.astype(BF16)

    wr, wi = powers(jnp.where(isf, 15 - rowblk, rowblk))
    p_ref[0] = jnp.concatenate([wr * bbr - wi * bbi, wr * bbi + wi * bbr], axis=1).astype(BF16)

    arc = arc_ref[0]
    aic = aic_ref[0]
    dtc = jnp.exp(ldtc_ref[0])
    laneblk = lax.broadcasted_iota(jnp.int32, (LANES, 256), 1) // CHUNK
    rowf = lax.broadcasted_iota(jnp.int32, (LANES, 256), 0) < 64
    magc = jnp.exp(dtc * arc)
    lrc = magc * jnp.cos(dtc * aic)
    lic = magc * jnp.sin(dtc * aic)
    vr, vi = _cpow(lrc, lic, jnp.where(rowf, laneblk, 15 - laneblk))
    wr = vr * lrc - vi * lic
    wi = vr * lic + vi * lrc
    q_ref[0] = jnp.concatenate([ctr * wr - cti * wi, -(ctr * wi + cti * wr)], axis=0).astype(BF16)

    pr, pi = lr, li
    for _ in range(4):
        pr, pi = pr * pr - pi * pi, 2.0 * pr * pi
    l_ref[0] = jnp.concatenate([pr, pi], axis=1)


def _s5prep(a_re, a_im, log_dt, b_re, b_im, c_re, c_im):
    G = S5_GROUPS

    def rows(x):
        return jnp.concatenate([x[0], x[1]], axis=-1).reshape(G, 1, LANES)

    ldt = jnp.broadcast_to(log_dt[:, :, None], (2, G, 64))
    ar, ai, ld = rows(a_re), rows(a_im), rows(ldt)
    arc, aic, ldc = (v.reshape(G, LANES, 1) for v in (ar, ai, ld))

    def btile(b):
        bt = jnp.tile(jnp.swapaxes(b, 2, 3), (1, 1, CHUNK, 1))
        return jnp.concatenate([bt[0], bt[1]], axis=-1)

    def ctile(c):
        ct = jnp.tile(jnp.swapaxes(c, 2, 3), (1, 1, 1, CHUNK))
        return jnp.concatenate([ct[0], ct[1]], axis=1)

    vec = pl.BlockSpec((1, 1, LANES), lambda g: (g, 0, 0))
    col = pl.BlockSpec((1, LANES, 1), lambda g: (g, 0, 0))
    bsp = pl.BlockSpec((1, 256, LANES), lambda g: (g, 0, 0))
    csp = pl.BlockSpec((1, LANES, 256), lambda g: (g, 0, 0))
    msp = pl.BlockSpec((1, 256, 256), lambda g: (g, 0, 0))
    return pl.pallas_call(
        _s5prep_kernel, grid=(G,),
        in_specs=[vec, vec, vec, col, col, col, bsp, bsp, csp, csp],
        out_specs=[msp, msp, msp, pl.BlockSpec((1, 1, 256), lambda g: (g, 0, 0))],
        out_shape=[jax.ShapeDtypeStruct((G, 256, 256), BF16)] * 3 + [jax.ShapeDtypeStruct((G, 1, 256), F32)],
        compiler_params=_cparams(32), name="s5prep")(
            ar, ai, ld, arc, aic, ldc, btile(b_re), btile(b_im), ctile(c_re), ctile(c_im))


def _seg_transpose8(vs, seg):
    vs = list(vs)
    for d in (4, 2, 1):
        hi = (seg & d) != 0
        new = list(vs)
        for i in range(8):
            if i & d:
                continue
            j = i + d
            new[i] = jnp.where(hi, pltpu.roll(vs[j], d * CHUNK, axis=1), vs[i])
            new[j] = jnp.where(hi, vs[j], pltpu.roll(vs[i], LANES - d * CHUNK, axis=1))
        vs = new
    return vs


def _s5_kernel(u_ref, k_ref, p_ref, q_ref, l_ref, h0_ref, y_ref, fin_ref, ebuf, ybuf, *, A, Bq, B, lat):
    RP = A * Bq
    nchunk = RP // B
    G = GROUPS_PER_SLAB
    nr = 256
    qn = min(Bq, nr)
    tiles = [(r0, [((r0 + i * qn) // Bq, (r0 + i * qn) % Bq) for i in range(nr // qn)])
             for r0 in range(0, RP, nr)]
    lane_blk = lax.broadcasted_iota(jnp.int32, (nr, LANES), 1) // CHUNK

    for r0, pieces in tiles:
        xs = [jnp.concatenate([u_ref[0, a, tau, q0:q0 + qn, :] for a, q0 in pieces], axis=0)
              for tau in range(CHUNK)]
        halves = [_seg_transpose8(xs[8 * h:8 * h + 8], lane_blk) for h in range(2)]
        for g in range(G):
            ug = jnp.concatenate([halves[0][g], halves[1][g]], axis=1).astype(BF16)
            ybuf[g, r0:r0 + nr, :] = _dot(ug, k_ref[g])
            ebuf[g, r0:r0 + nr, :] = _dot(ug, p_ref[g])

    def off(m):
        if lat:
            return ((m & 1) * GRID_W + (m >> 1)) * B
        return m * B

    fmask = (lax.broadcasted_iota(jnp.int32, (B, 2 * LANES), 1) % LANES) < 64
    fmask1 = lax.broadcasted_iota(jnp.int32, (B, LANES), 1) < 64
    lam = [l_ref[g] for g in range(G)]

    def body(i, carry):
        of = pl.multiple_of(off(i), B)
        ob = pl.multiple_of(off(nchunk - 1 - i), B)
        out = []
        for g in range(G):
            re, im = carry[2 * g], carry[2 * g + 1]
            ef = ebuf[g, pl.ds(of, B), :]
            eb = ebuf[g, pl.ds(ob, B), :]
            s = jnp.concatenate([re, im], axis=1)
            ebuf[g, pl.ds(of, B), :] = jnp.where(fmask, s, ef)
            ebuf[g, pl.ds(ob, B), :] = jnp.where(fmask, eb, s)
            lr = lam[g][:, :LANES]
            li = lam[g][:, LANES:]
            e_re = jnp.where(fmask1, ef[:, :LANES], eb[:, :LANES])
            e_im = jnp.where(fmask1, ef[:, LANES:], eb[:, LANES:])
            out.append(lr * re - li * im + e_re)
            out.append(lr * im + li * re + e_im)
        return tuple(out)

    init = []
    for g in range(G):
        h0 = h0_ref[g]
        init += [h0[:, :LANES], h0[:, LANES:]]
    fin = lax.fori_loop(0, nchunk, body, tuple(init))
    for g in range(G):
        fin_ref[g] = jnp.concatenate([fin[2 * g], fin[2 * g + 1]], axis=1)

    for g in range(G):
        ybuf[g] = ybuf[g] + _dot(ebuf[g].astype(BF16), q_ref[g])
    for r0, pieces in tiles:
        for h in range(2):
            yt = _seg_transpose8([ybuf[g, r0:r0 + nr, h * LANES:(h + 1) * LANES] for g in range(G)], lane_blk)
            for tp in range(8):
                for k, (a, q0) in enumerate(pieces):
                    y_ref[0, a, 8 * h + tp, q0:q0 + qn, :] = yt[tp][k * qn:(k + 1) * qn]


def _s5(proj5, kmat, pmat, qmat, lam16, h0, A, Bq, B, lat, name):
    RP = A * Bq
    G = GROUPS_PER_SLAB
    kern = functools.partial(_s5_kernel, A=A, Bq=Bq, B=B, lat=lat)
    msp = pl.BlockSpec((G, 256, 256), lambda s: (s, 0, 0))
    return pl.pallas_call(
        kern, grid=(S5_SLABS,),
        in_specs=[pl.BlockSpec((1, A, CHUNK, Bq, LANES), lambda s: (s + 2 * RG_SLABS, 0, 0, 0, 0)),
                  msp, msp, msp,
                  pl.BlockSpec((G, 1, 256), lambda s: (s, 0, 0)),
                  pl.BlockSpec((G, B, 256), lambda s: (s, 0, 0))],
        out_specs=[pl.BlockSpec((1, A, CHUNK, Bq, LANES), lambda s: (s, 0, 0, 0, 0)),
                   pl.BlockSpec((G, B, 256), lambda s: (s, 0, 0))],
        out_shape=[jax.ShapeDtypeStruct((S5_SLABS, A, CHUNK, Bq, LANES), F32),
                   jax.ShapeDtypeStruct((S5_GROUPS, B, 256), F32)],
        scratch_shapes=[pltpu.VMEM((G, RP, 256), F32), pltpu.VMEM((G, RP, 256), F32)],
        compiler_params=_cparams(58), name=name)(proj5, kmat, pmat, qmat, lam16, h0)


def _post_kernel(yr_ref, ys_ref, u_ref, x_ref, mod_ref, wo_ref, wg_ref, bgl_ref, d_ref, grg_ref, gs5_ref,
                 o_ref, *, B, tT):
    yr = jnp.concatenate([yr_ref[s] for s in range(RG_SLABS)], axis=1)
    nr = (_rms(yr) * grg_ref[...]).astype(BF16)
    ys = jnp.concatenate([ys_ref[s] for s in range(S5_SLABS)], axis=1)
    u = jnp.concatenate([u_ref[s] for s in range(S5_SLABS)], axis=1)
    ys = jax.nn.gelu(ys + d_ref[...] * u)
    ys = ys * jax.nn.sigmoid(_dot(ys.astype(BF16), wg_ref[...]) + bgl_ref[...])
    ns = (_rms(ys) * gs5_ref[...]).astype(BF16)
    nch = 512
    for c0 in range(0, D_MODEL, nch):
        cs = slice(c0, c0 + nch)
        y = _dot(nr, wo_ref[0:D_RG, cs]) + _dot(ns, wo_ref[D_RG:D_MODEL, cs])
        cols = [_rows_tb_to_bt(y[:, j * LANES:(j + 1) * LANES], B, tT) for j in range(nch // LANES)]
        for b in range(B):
            yb = jnp.concatenate([c[b * tT:(b + 1) * tT] for c in cols], axis=1)
            o_ref[b, :, cs] = x_ref[b, :, cs] + mod_ref[b, 2:3, cs] * yb


def _post(y_rg, y_ssm, proj, x, mod, w_out_bf16, w_glu_bf16, b_glu, s5_d, g_rg, g_s5, name):
    B, T, _ = x.shape
    tT = 512 // B
    rows = tT * B
    kern = functools.partial(_post_kernel, B=B, tT=tT)
    const = lambda i: (0, 0)
    return pl.pallas_call(
        kern, grid=(T // tT,),
        in_specs=[pl.BlockSpec((RG_SLABS, rows, LANES), lambda i: (0, i, 0)),
                  pl.BlockSpec((S5_SLABS, rows, LANES), lambda i: (0, i, 0)),
                  pl.BlockSpec((S5_SLABS, rows, LANES), lambda i: (2 * RG_SLABS // S5_SLABS, i, 0)),
                  pl.BlockSpec((B, tT, D_MODEL), lambda i: (0, i, 0)),
                  pl.BlockSpec((B, N_MOD, D_MODEL), lambda i: (0, 0, 0)),
                  pl.BlockSpec((D_MODEL, D_MODEL), const, pipeline_mode=pl.Buffered(1)),
                  pl.BlockSpec((D_S5, D_S5), const),
                  pl.BlockSpec((1, D_S5), const),
                  pl.BlockSpec((1, D_S5), const),
                  pl.BlockSpec((1, D_RG), const),
                  pl.BlockSpec((1, D_S5), const)],
        out_specs=pl.BlockSpec((B, tT, D_MODEL), lambda i: (0, i, 0)),
        out_shape=jax.ShapeDtypeStruct((B, T, D_MODEL), F32),
        compiler_params=_cparams(56), name=name)(
            y_rg, y_ssm, proj, x, mod, w_out_bf16, w_glu_bf16, b_glu, s5_d, g_rg, g_s5)


def _premoe_kernel(x_ref, mod_ref, g_ref, wr_ref, h_ref, a_ref):
    h = _rms(x_ref[...]) * g_ref[...] * (1.0 + mod_ref[0, 4:5, :]) + mod_ref[0, 3:4, :]
    hhi = h.astype(BF16)
    h_ref[...] = hhi
    hlo = (h - hhi.astype(F32)).astype(BF16)
    w = wr_ref[...]
    whi = w.astype(BF16)
    wlo = (w - whi.astype(F32)).astype(BF16)
    nt = lambda a, b: lax.dot_general(a, b, (((1,), (1,)), ((), ())), preferred_element_type=F32)
    lg = nt(whi, hhi) + nt(wlo, hhi) + nt(whi, hlo)
    e = jnp.exp(lg - jnp.max(lg, axis=0, keepdims=True))
    a_ref[...] = e / jnp.sum(e, axis=0, keepdims=True)


def _premoe(x1, mod, g2, w_router_t, T, name):
    n = x1.shape[0]
    rows = 512
    per_req = T // rows if mod.shape[0] > 1 else n
    return pl.pallas_call(
        _premoe_kernel, grid=(n // rows,),
        in_specs=[pl.BlockSpec((rows, D_MODEL), lambda i: (i, 0)),
                  pl.BlockSpec((1, N_MOD, D_MODEL), lambda i: (i // per_req, 0, 0)),
                  pl.BlockSpec((1, D_MODEL), lambda i: (0, 0)),
                  pl.BlockSpec((N_EXPERTS, D_MODEL), lambda i: (0, 0))],
        out_specs=[pl.BlockSpec((rows, D_MODEL), lambda i: (i, 0)),
                   pl.BlockSpec((N_EXPERTS, rows), lambda i: (0, i))],
        out_shape=[jax.ShapeDtypeStruct((n, D_MODEL), BF16),
                   jax.ShapeDtypeStruct((N_EXPERTS, n), F32)],
        compiler_params=_cparams(40), name=name)(x1, mod, g2, w_router_t)


def _topk_kernel(a_ref, s_ref, st_ref, c_ref, *, cap, T, nreq):
    capf = float(cap)

    def count_ge(r, v):
        return jnp.sum((a_ref[:, r * T:(r + 1) * T] >= v).astype(F32), axis=1, keepdims=True)

    def bit_step(i, thrs):
        bit = jnp.int32(1) << (30 - i)
        out = []
        for r in range(nreq):
            cand = thrs[r] | bit
            ok = count_ge(r, pltpu.bitcast(cand, F32)) >= capf
            out.append(jnp.where(ok, cand, thrs[r]))
        return tuple(out)

    zero = jnp.zeros((N_EXPERTS, 1), jnp.int32)
    thrs = lax.fori_loop(0, 31, bit_step, (zero,) * nreq)

    def mid_step(i, c):
        out = []
        for r in range(nreq):
            lo, hi = c[2 * r], c[2 * r + 1]
            mid = 0.5 * (lo + hi)
            ok = count_ge(r, mid) >= capf
            out += [jnp.where(ok, mid, lo), jnp.where(ok, hi, mid)]
        return tuple(out)

    init = []
    for r in range(nreq):
        init += [pltpu.bitcast(thrs[r], F32), pltpu.bitcast(thrs[r] + 1, F32)]
    bounds = lax.fori_loop(0, 24, mid_step, tuple(init))

    u = (lax.broadcasted_iota(jnp.int32, (T, T), 0)
         <= lax.broadcasted_iota(jnp.int32, (T, T), 1)).astype(F32).astype(BF16)
    nblk = T // TOKEN_BLOCK
    cl = lax.broadcasted_iota(jnp.int32, c_ref.shape, 1)
    cnt = jnp.zeros(c_ref.shape, F32)
    for r in range(nreq):
        aff = a_ref[:, r * T:(r + 1) * T]
        lo, hi = bounds[2 * r], bounds[2 * r + 1]
        gt = aff >= hi
        eq = (aff >= lo) & jnp.logical_not(gt)
        need = capf - jnp.sum(gt.astype(F32), axis=1, keepdims=True)
        pe = _dot(eq.astype(F32).astype(BF16), u)
        sel = gt | (eq & (pe <= need))
        ps = _dot(sel.astype(F32).astype(BF16), u)
        slot = jnp.where(sel, ps - 1.0, -1.0)
        s_ref[:, r * T:(r + 1) * T] = slot
        st_ref[r * T:(r + 1) * T, :] = slot.T
        for k in range(nblk):
            end = (k + 1) * TOKEN_BLOCK
            cnt = jnp.where(cl == r * nblk + k, ps[:, end - 1:end], cnt)
    c_ref[...] = cnt


def _topk(aff_t, T, name):
    n = aff_t.shape[1]
    nreq = n // T
    ncnt = nreq * (T // TOKEN_BLOCK)
    kern = functools.partial(_topk_kernel, cap=T // 8, T=T, nreq=nreq)
    return pl.pallas_call(
        kern, grid=(1,),
        in_specs=[pl.BlockSpec((N_EXPERTS, n), lambda i: (0, 0))],
        out_specs=[pl.BlockSpec((N_EXPERTS, n), lambda i: (0, 0)),
                   pl.BlockSpec((n, N_EXPERTS), lambda i: (0, 0)),
                   pl.BlockSpec((N_EXPERTS, ncnt), lambda i: (0, 0))],
        out_shape=[jax.ShapeDtypeStruct((N_EXPERTS, n), F32),
                   jax.ShapeDtypeStruct((n, N_EXPERTS), F32),
                   jax.ShapeDtypeStruct((N_EXPERTS, ncnt), F32)],
        compiler_params=_cparams(48), name=name)(aff_t)


def _window_starts(start_ref, nblk):
    base = (pl.program_id(0) * nblk + pl.program_id(1)) * N_EXPERTS
    return [pl.multiple_of(start_ref[base + e], 16) for e in range(N_EXPERTS)]


def _gather_kernel(start_ref, ok_ref, h_ref, s_ref, a_ref, xg_ref, gt_ref, *, cap, W, nblk):
    @pl.when(pl.program_id(1) == 0)
    def _():
        xg_ref[...] = jnp.zeros_like(xg_ref)
        gt_ref[...] = jnp.zeros_like(gt_ref)

    h = h_ref[...]
    nch = 512

    def step(w, starts):
        j = lax.broadcasted_iota(jnp.int32, (w, TOKEN_BLOCK), 0).astype(F32)
        ohs = []
        for e in range(N_EXPERTS):
            rel = s_ref[e:e + 1, :]
            if starts is not None:
                rel = rel - starts[e].astype(F32)
            oh = rel == j
            rows = pl.ds(starts[e], w) if starts is not None else slice(0, w)
            gt_ref[e, rows, :] += jnp.sum(jnp.where(oh, a_ref[e:e + 1, :], 0.0), axis=1, keepdims=True)
            ohs.append(oh.astype(F32).astype(BF16))
        ohm = jnp.concatenate(ohs, axis=0)
        for c in range(0, D_MODEL, nch):
            res = _dot(ohm, h[:, c:c + nch])
            for e in range(N_EXPERTS):
                rows = pl.ds(starts[e], w) if starts is not None else slice(0, w)
                xg_ref[e, rows, c:c + nch] += res[e * w:(e + 1) * w].astype(BF16)

    if W == cap:
        step(cap, None)
    else:
        ok = ok_ref[pl.program_id(0)] > 0

        @pl.when(ok)
        def _():
            step(W, _window_starts(start_ref, nblk))

        @pl.when(jnp.logical_not(ok))
        def _():
            step(cap, None)


def _gather(h2, slot, aff_t, starts, ok, T, name):
    n = h2.shape[0]
    cap = T // 8
    nreq = n // T
    nblk = T // TOKEN_BLOCK
    kern = functools.partial(_gather_kernel, cap=cap, W=min(SLOT_WINDOW, cap), nblk=nblk)
    tok = lambda b, k, *_: (0, b * nblk + k)
    req = lambda b, k, *_: (0, b, 0)
    return pl.pallas_call(
        kern,
        grid_spec=pltpu.PrefetchScalarGridSpec(
            num_scalar_prefetch=2, grid=(nreq, nblk),
            in_specs=[pl.BlockSpec((TOKEN_BLOCK, D_MODEL), lambda b, k, *_: (b * nblk + k, 0)),
                      pl.BlockSpec((N_EXPERTS, TOKEN_BLOCK), tok),
                      pl.BlockSpec((N_EXPERTS, TOKEN_BLOCK), tok)],
            out_specs=[pl.BlockSpec((N_EXPERTS, cap, D_MODEL), req),
                       pl.BlockSpec((N_EXPERTS, cap, 1), req)]),
        out_shape=[jax.ShapeDtypeStruct((N_EXPERTS, nreq * cap, D_MODEL), BF16),
                   jax.ShapeDtypeStruct((N_EXPERTS, nreq * cap, 1), F32)],
        compiler_params=_cparams(56), name=name)(starts, ok, h2, slot, aff_t)


def _ffn_kernel(xc_ref, xl_ref, gc_ref, gl_ref, w1_ref, w3_ref, w2_ref, yc_ref, yl_ref, xbuf, acc, *, rc, nf, rsub):
    f = pl.program_id(2)

    @pl.when(f == 0)
    def _():
        xbuf[0:rc, :] = xc_ref[0]
        xbuf[rc:, :] = xl_ref[0]

    @pl.when(f == 0)
    def _():
        acc[...] = jnp.zeros_like(acc)

    w1 = w1_ref[0].astype(BF16)
    w3 = w3_ref[0].astype(BF16)
    w2 = w2_ref[0].astype(BF16)
    rows = xbuf.shape[0]
    for r0 in range(0, rows, rsub):
        x = xbuf[r0:r0 + rsub, :]
        hid = jax.nn.silu(_dot(x, w1)) * _dot(x, w3)
        acc[r0:r0 + rsub, :] += _dot(hid.astype(BF16), w2)

    @pl.when(f == nf - 1)
    def _():
        yc_ref[0] = (acc[0:rc, :] * gc_ref[0]).astype(BF16)
        yl_ref[0] = (acc[rc:, :] * gl_ref[0]).astype(BF16)


def _ffn(xg_c, xg_l, gt_c, gt_l, w1, w3, w2):
    rc = xg_c.shape[1] // 2
    rl = xg_l.shape[1] // 2
    bf = 256
    nf = D_MODEL // bf
    kern = functools.partial(_ffn_kernel, rc=rc, nf=nf, rsub=(rc + rl) // 2)
    rowsp = lambda r, w: pl.BlockSpec((1, r, w), lambda e, h, f: (e, h, 0))
    return pl.pallas_call(
        kern, grid=(N_EXPERTS, 2, nf),
        in_specs=[rowsp(rc, D_MODEL), rowsp(rl, D_MODEL), rowsp(rc, 1), rowsp(rl, 1),
                  pl.BlockSpec((1, D_MODEL, bf), lambda e, h, f: (e, 0, f)),
                  pl.BlockSpec((1, D_MODEL, bf), lambda e, h, f: (e, 0, f)),
                  pl.BlockSpec((1, bf, D_MODEL), lambda e, h, f: (e, f, 0))],
        out_specs=[rowsp(rc, D_MODEL), rowsp(rl, D_MODEL)],
        out_shape=[jax.ShapeDtypeStruct(xg_c.shape, BF16), jax.ShapeDtypeStruct(xg_l.shape, BF16)],
        scratch_shapes=[pltpu.VMEM((rc + rl, D_MODEL), BF16), pltpu.VMEM((rc + rl, D_MODEL), F32)],
        compiler_params=_cparams(56), name="ffn")(xg_c, xg_l, gt_c, gt_l, w1, w3, w2)


def _expand_consts(w):
    lane = jnp.arange(N_EXPERTS * w)
    expand = (jnp.arange(N_EXPERTS)[:, None] == (lane // w)[None, :]).astype(BF16)
    return expand, (lane % w).astype(F32)[None, :]


def _combine_kernel(start_ref, ok_ref, x_ref, y_ref, st_ref, sv_ref, mod_ref, fg_ref, exw_ref, jlw_ref,
                    exd_ref, jld_ref, o_ref, stage, *, cap, W, nblk):
    st = st_ref[...]

    def finish(moe):
        x2 = x_ref[...] + mod_ref[0, 5:6, :] * moe
        o_ref[...] = _rms(x2) * fg_ref[...]

    def dense():
        rep = _dot(st.astype(BF16), exd_ref[...])
        oh = (rep == jld_ref[...]).astype(F32).astype(BF16)
        finish(_dot(oh, y_ref[...].reshape(N_EXPERTS * cap, D_MODEL)))

    if W == cap:
        dense()
        return
    ok = ok_ref[pl.program_id(0)] > 0

    @pl.when(ok)
    def _():
        starts = _window_starts(start_ref, nblk)
        for e in range(N_EXPERTS):
            stage[e * W:(e + 1) * W, :] = y_ref[e, pl.ds(starts[e], W), :]
        rep = _dot((st - sv_ref[0]).astype(BF16), exw_ref[...])
        oh = (rep == jlw_ref[...]).astype(F32).astype(BF16)
        finish(_dot(oh, stage[...]))

    @pl.when(jnp.logical_not(ok))
    def _():
        dense()


def _combine(x1, ybuf, slot_t, starts, ok, starts_f, mod, final_g, T, name):
    n = x1.shape[0]
    cap = T // 8
    nblk = T // TOKEN_BLOCK
    W = min(SLOT_WINDOW, cap)
    multi = mod.shape[0] > 1
    exw, jlw = _expand_consts(W)
    exd, jld = _expand_consts(cap)
    kern = functools.partial(_combine_kernel, cap=cap, W=W, nblk=nblk)
    tok = lambda b, k, *_: (b * nblk + k, 0)
    const = lambda b, k, *_: (0, 0)
    return pl.pallas_call(
        kern,
        grid_spec=pltpu.PrefetchScalarGridSpec(
            num_scalar_prefetch=2, grid=(n // T, nblk),
            in_specs=[pl.BlockSpec((TOKEN_BLOCK, D_MODEL), tok),
                      pl.BlockSpec((N_EXPERTS, cap, D_MODEL), lambda b, k, *_: (0, b, 0)),
                      pl.BlockSpec((TOKEN_BLOCK, N_EXPERTS), tok),
                      pl.BlockSpec((1, 1, N_EXPERTS), lambda b, k, *_: (b * nblk + k, 0, 0)),
                      pl.BlockSpec((1, N_MOD, D_MODEL), lambda b, k, *_: (b if multi else 0, 0, 0)),
                      pl.BlockSpec((1, D_MODEL), const),
                      pl.BlockSpec(exw.shape, const), pl.BlockSpec(jlw.shape, const),
                      pl.BlockSpec(exd.shape, const), pl.BlockSpec(jld.shape, const)],
            out_specs=pl.BlockSpec((TOKEN_BLOCK, D_MODEL), tok),
            scratch_shapes=[pltpu.VMEM((N_EXPERTS * W, D_MODEL), BF16)]),
        out_shape=jax.ShapeDtypeStruct((n, D_MODEL), F32),
        compiler_params=_cparams(56), name=name)(starts, ok, x1, ybuf, slot_t, starts_f, mod, final_g,
                                                 exw, jlw, exd, jld)


def _window_tables(cnt, T):
    cap = T // 8
    nblk = T // TOKEN_BLOCK
    nreq = cnt.shape[1] // nblk
    W = min(SLOT_WINDOW, cap)
    end = cnt.astype(jnp.int32).reshape(N_EXPERTS, nreq, nblk)
    first = jnp.concatenate([jnp.zeros_like(end[..., :1]), end[..., :-1]], axis=-1)
    start = jnp.minimum((first // 16) * 16, cap - W)
    ok = jnp.all(end - start <= W, axis=(0, 2)).astype(jnp.int32)
    start = start.transpose(1, 2, 0)
    return start.reshape(-1), ok, start.astype(F32).reshape(nreq * nblk, 1, N_EXPERTS)


def _blockdiag_pairs(w):
    w = w.reshape(2, RG_SLABS, 2, 64, 64)
    bd = jnp.einsum('dshij,hk->dshikj', w, jnp.eye(2, dtype=w.dtype))
    return bd.reshape(2, RG_SLABS, LANES, LANES).transpose(1, 0, 2, 3)


def kernel(x_prompt, x_sample, state_rglru, state_s5_re, state_s5_im, c, c_ctx, norm1_g, norm2_g, w_ada, b_ada, w_in, rg_conv_w, rg_conv_b, rg_wa, rg_ba, rg_wx, rg_bx, rg_lam, s5_a_re, s5_a_im, s5_log_dt, s5_b_re, s5_b_im, s5_c_re, s5_c_im, s5_d, s5_w_glu, s5_b_glu, gnorm_rg, gnorm_s5, w_out, w_router, w1, w3, w2, final_g):
    l = 0
    Bp, Tp, _ = x_prompt.shape
    Bs, Ts, _ = x_sample.shape

    c16 = jnp.concatenate([c, c_ctx[None], jnp.zeros((16 - Bs - 1, D_MODEL), F32)], axis=0)
    mod = _adaln(c16, w_ada[l], b_ada[l]).reshape(16, N_MOD, D_MODEL)
    mod_lat = mod[:Bs]
    mod_ctx1 = mod[Bs:Bs + 1]
    mod_ctx = jnp.broadcast_to(mod_ctx1, (Bp, N_MOD, D_MODEL))

    w_in_b = w_in[l].astype(BF16)
    w_out_b = w_out[l].astype(BF16)
    w_glu_b = s5_w_glu[l].astype(BF16)
    row = lambda v: v.reshape(1, -1)

    wg = (0.5 * jnp.concatenate([_blockdiag_pairs(rg_wa[l]), _blockdiag_pairs(rg_wx[l])], axis=-1)).astype(BF16)
    bg = 0.5 * jnp.concatenate([rg_ba[l].reshape(2, RG_SLABS, LANES), rg_bx[l].reshape(2, RG_SLABS, LANES)],
                               axis=-1).transpose(1, 0, 2).reshape(RG_SLABS, 2, 1, 2 * LANES)
    kmat, pmat, qmat, lam16 = _s5prep(s5_a_re[l], s5_a_im[l], s5_log_dt[l], s5_b_re[l], s5_b_im[l],
                                       s5_c_re[l], s5_c_im[l])

    def mixer(x, modp, rg_h0, s5_h0, lat, tag):
        B, T, _ = x.shape
        R = T * B
        proj = _inproj(x, modp, row(norm1_g[l]), w_in_b, "inproj_" + tag)
        y_rg, fin_rg = _rglru(proj, rg_conv_w[l], row(rg_conv_b[l]), wg, bg, rg_lam[l], rg_h0, B, T,
                              "rglru_" + tag)
        if lat:
            A, Bq = T // GRID_W // CHUNK, GRID_W * B
        else:
            A, Bq = T // CHUNK, B
        y_ssm, fin_s5 = _s5(proj.reshape(N_SLAB, A, CHUNK, Bq, LANES), kmat, pmat, qmat, lam16, s5_h0,
                            A, Bq, B, lat, "s5_" + tag)
        x1 = _post(y_rg, y_ssm.reshape(S5_SLABS, R, LANES), proj, x, modp, w_out_b, w_glu_b,
                   row(s5_b_glu[l]), row(s5_d[l]), row(gnorm_rg[l]), row(gnorm_s5[l]), "post_" + tag)
        return x1.reshape(R, D_MODEL), fin_rg, fin_s5

    z_rg = jnp.zeros((2, Bp, D_RG), F32)
    z_s5 = jnp.zeros((S5_GROUPS, Bp, 256), F32)
    x1_c, fin_rg, fin_s5 = mixer(x_prompt, mod_ctx, z_rg, z_s5, False, "ctx")
    rg_h0 = state_rglru[:, l].transpose(1, 0, 2)
    s5_h0 = jnp.stack([state_s5_re[:, l, 0], state_s5_re[:, l, 1], state_s5_im[:, l, 0], state_s5_im[:, l, 1]],
                      axis=0)
    s5_h0 = s5_h0.transpose(2, 1, 0, 3).reshape(S5_GROUPS, Bs, 256)
    x1_s, _, _ = mixer(x_sample, mod_lat, rg_h0, s5_h0, True, "lat")

    w_router_t = w_router[l].T
    g2 = row(norm2_g[l])

    def route(x1, modp, T, tag):
        h2, aff_t = _premoe(x1, modp, g2, w_router_t, T, "premoe_" + tag)
        slot, slot_t, cnt = _topk(aff_t, T, "topk_" + tag)
        win = _window_tables(cnt, T)
        xg, gates = _gather(h2, slot, aff_t, win[0], win[1], T, "gather_" + tag)
        return xg, gates, slot_t, win

    xg_c, gt_c, st_c, win_c = route(x1_c, mod_ctx1, Tp, "ctx")
    xg_s, gt_s, st_s, win_s = route(x1_s, mod_lat, Ts, "lat")
    y_c, y_s = _ffn(xg_c, xg_s, gt_c, gt_s, w1[l], w3[l], w2[l])
    fg = row(final_g)
    y_prompt = _combine(x1_c, y_c, st_c, *win_c, mod_ctx1, fg, Tp, "combine_ctx").reshape(Bp, Tp, D_MODEL)
    y_sample = _combine(x1_s, y_s, st_s, *win_s, mod_lat, fg, Ts, "combine_lat").reshape(Bs, Ts, D_MODEL)

    new_rg = fin_rg.transpose(1, 0, 2)[:, None]
    f4 = fin_s5.reshape(S5_GROUPS, Bp, 4, 64).transpose(1, 2, 0, 3)
    new_re = f4[:, None, 0:2]
    new_im = f4[:, None, 2:4]
    return (y_prompt, y_sample, new_rg, new_re, new_im)
```

```python
import functools

import jax
import jax.numpy as jnp
from jax import lax
from jax.experimental import pallas as pl
from jax.experimental.pallas import tpu as pltpu

F32 = jnp.float32
BF16 = jnp.bfloat16
HIGHEST = lax.Precision.HIGHEST

LANES = 128
D_MODEL = 2048
D_RG = 1536
D_S5 = 512
D_IN = 2 * D_RG + D_S5
N_SLAB = D_IN // LANES
RG_SLABS = D_RG // LANES
S5_SLABS = D_S5 // LANES
S5_GROUPS = 32
GROUPS_PER_SLAB = 8
CHUNK = 16
N_EXPERTS = 16
N_MOD = 6
EPS = 1e-6
RG_C = 8.0
LOG2_E = 1.4426950408889634
GRID_W = 64
TOKEN_BLOCK = 256
SLOT_WINDOW = 64
MIB = 1024 * 1024


def _cparams(vmem_mib):
    return pltpu.CompilerParams(vmem_limit_bytes=vmem_mib * MIB)


def _dot(a, b):
    return jnp.dot(a, b, preferred_element_type=F32)


def _rms(x):
    return x * lax.rsqrt(jnp.mean(x * x, axis=-1, keepdims=True) + EPS)


def _adaln_kernel(c_ref, w_ref, b_ref, o_ref):
    s = jax.nn.silu(c_ref[...]).astype(BF16)
    o_ref[...] = _dot(s, w_ref[...].astype(BF16)) + b_ref[...]


def _adaln(c16, w_ada, b_ada):
    n = w_ada.shape[1]
    bn = 1024
    return pl.pallas_call(
        _adaln_kernel, grid=(n // bn,),
        in_specs=[pl.BlockSpec((16, D_MODEL), lambda j: (0, 0)),
                  pl.BlockSpec((D_MODEL, bn), lambda j: (0, j)),
                  pl.BlockSpec((1, bn), lambda j: (0, j))],
        out_specs=pl.BlockSpec((16, bn), lambda j: (0, j)),
        out_shape=jax.ShapeDtypeStruct((16, n), F32),
        compiler_params=_cparams(40), name="adaln")(c16, w_ada, b_ada.reshape(1, n))


def _sublane_transpose8(vs, sub):
    vs = list(vs)
    for d in (4, 2, 1):
        hi = (sub & d) != 0
        new = list(vs)
        for i in range(8):
            if i & d:
                continue
            new[i] = jnp.where(hi, pltpu.roll(vs[i + d], d, axis=1), vs[i])
            new[i + d] = jnp.where(hi, vs[i + d], pltpu.roll(vs[i], 8 - d, axis=1))
        vs = new
    return vs


def _rows_bt_to_tb(p, B, tT):
    sub = lax.broadcasted_iota(jnp.int32, (tT // 8, 8, LANES), 1)
    groups = []
    for g in range(B // 8):
        vs = [p[(8 * g + i) * tT:(8 * g + i + 1) * tT].reshape(tT // 8, 8, LANES) for i in range(8)]
        groups.append(_sublane_transpose8(vs, sub))
    out = jnp.stack([jnp.stack([grp[a] for grp in groups], axis=1) for a in range(8)], axis=1)
    return out.reshape(tT * B, LANES)


def _rows_tb_to_bt(y, B, tT):
    ng = B // 8
    y4 = y.reshape(tT // 8, 8 * ng, 8, LANES)
    sub = lax.broadcasted_iota(jnp.int32, (tT // 8, 8, LANES), 1)
    out = []
    for g in range(ng):
        out += _sublane_transpose8([y4[:, a * ng + g] for a in range(8)], sub)
    return jnp.concatenate([v.reshape(tT, LANES) for v in out], axis=0)


def _inproj_kernel(x_ref, mod_ref, g_ref, w_ref, o_ref, *, B, tT, nch):
    x = x_ref[...]
    shift = mod_ref[:, 0:1, :]
    scale = mod_ref[:, 1:2, :]
    h = _rms(x) * g_ref[...] * (1.0 + scale) + shift
    hb = h.reshape(B * tT, D_MODEL).astype(BF16)
    for c0 in range(0, D_IN, nch):
        p = _dot(hb, w_ref[:, c0:c0 + nch])
        if D_RG <= c0 < 2 * D_RG:
            p = jax.nn.gelu(p)
        for j in range(nch // LANES):
            o_ref[c0 // LANES + j] = _rows_bt_to_tb(p[:, j * LANES:(j + 1) * LANES], B, tT)


def _inproj(x, mod, g1, w_in_bf16, name):
    B, T, _ = x.shape
    tT = 512 // B
    kern = functools.partial(_inproj_kernel, B=B, tT=tT, nch=512)
    return pl.pallas_call(
        kern, grid=(T // tT,),
        in_specs=[pl.BlockSpec((B, tT, D_MODEL), lambda i: (0, i, 0)),
                  pl.BlockSpec((B, N_MOD, D_MODEL), lambda i: (0, 0, 0)),
                  pl.BlockSpec((1, D_MODEL), lambda i: (0, 0)),
                  pl.BlockSpec((D_MODEL, D_IN), lambda i: (0, 0), pipeline_mode=pl.Buffered(1))],
        out_specs=pl.BlockSpec((N_SLAB, tT * B, LANES), lambda i: (0, i, 0)),
        out_shape=jax.ShapeDtypeStruct((N_SLAB, T * B, LANES), F32),
        compiler_params=_cparams(56), name=name)(x, mod, g1, w_in_bf16)


def _rglru_kernel(xb_ref, gt_ref, cw_ref, cb_ref, wg_ref, bg_ref, lam_ref, h0_ref, y_ref, fin_ref,
                  *, B, T, CT):
    RC = CT * B
    nc = T // CT
    cw = [cw_ref[k:k + 1, :] for k in range(4)]
    cb = cb_ref[...]
    hl = [(-0.5 * RG_C * LOG2_E) * jax.nn.softplus(-lam_ref[d:d + 1, :]) for d in range(2)]

    def chunk(n, d):
        t0 = n * CT
        row0 = pl.multiple_of(t0 * B, RC)
        cur = xb_ref[0, pl.ds(row0, RC), :]
        p0 = pl.multiple_of(jnp.maximum(t0 - 1, 0) * B, B)
        prev = xb_ref[0, pl.ds(p0, B), :] * jnp.where(t0 > 0, 1.0, 0.0)
        n0 = pl.multiple_of(jnp.minimum(t0 + CT, T - 2) * B, B)
        nxt = xb_ref[0, pl.ds(n0, 2 * B), :] * jnp.where(t0 + CT < T, 1.0, 0.0)
        ext = jnp.concatenate([prev, cur, nxt], axis=0)
        xc = (cw[0] * ext[0:RC] + cw[1] * ext[B:B + RC] + cw[2] * ext[2 * B:2 * B + RC]
              + cw[3] * ext[3 * B:3 * B + RC] + cb)
        th = jnp.tanh(_dot(xc.astype(BF16), wg_ref[0, d]) + bg_ref[0, d])
        i = 0.5 + 0.5 * th[:, LANES:]
        a = jnp.exp2(hl[d] + hl[d] * th[:, :LANES])
        om = 1.0 - a * a
        bb = jnp.where(om > 0.0, om * lax.rsqrt(om), 0.0) * (i * xc)
        return a, bb, row0

    def make_body(finish):
        def body(n, carry):
            hf, hb = carry
            af, bf, rf = chunk(n, 0)
            ab, bk, rb = chunk(nc - 1 - n, 1)
            if finish:
                glf = gt_ref[0, pl.ds(rf, RC), :]
                glb = gt_ref[0, pl.ds(rb, RC), :]
                of = y_ref[0, pl.ds(rf, RC), :]
                ob = y_ref[0, pl.ds(rb, RC), :]
            for t in range(CT):
                u = CT - 1 - t
                hf = af[t * B:(t + 1) * B] * hf + bf[t * B:(t + 1) * B]
                hb = ab[u * B:(u + 1) * B] * hb + bk[u * B:(u + 1) * B]
                rows_f = pl.ds(rf + t * B, B)
                rows_b = pl.ds(rb + u * B, B)
                if finish:
                    y_ref[0, rows_f, :] = (of[t * B:(t + 1) * B] + hf) * glf[t * B:(t + 1) * B]
                    y_ref[0, rows_b, :] = (ob[u * B:(u + 1) * B] + hb) * glb[u * B:(u + 1) * B]
                else:
                    y_ref[0, rows_f, :] = hf
                    y_ref[0, rows_b, :] = hb
            return hf, hb
        return body

    carry = lax.fori_loop(0, nc // 2, make_body(False), (h0_ref[0], h0_ref[1]))
    hf, hb = lax.fori_loop(nc // 2, nc, make_body(True), carry)
    fin_ref[0] = hf
    fin_ref[1] = hb


def _rglru(proj, cw, cb, wg, bg, lam, h0, B, T, name):
    R = T * B
    kern = functools.partial(_rglru_kernel, B=B, T=T, CT=512 // B)
    return pl.pallas_call(
        kern, grid=(RG_SLABS,),
        in_specs=[pl.BlockSpec((1, R, LANES), lambda s: (s, 0, 0)),
                  pl.BlockSpec((1, R, LANES), lambda s: (s + RG_SLABS, 0, 0)),
                  pl.BlockSpec((4, LANES), lambda s: (0, s)),
                  pl.BlockSpec((1, LANES), lambda s: (0, s)),
                  pl.BlockSpec((1, 2, LANES, 2 * LANES), lambda s: (s, 0, 0, 0)),
                  pl.BlockSpec((1, 2, 1, 2 * LANES), lambda s: (s, 0, 0, 0)),
                  pl.BlockSpec((2, LANES), lambda s: (0, s)),
                  pl.BlockSpec((2, B, LANES), lambda s: (0, 0, s))],
        out_specs=[pl.BlockSpec((1, R, LANES), lambda s: (s, 0, 0)),
                   pl.BlockSpec((2, B, LANES), lambda s: (0, 0, s))],
        out_shape=[jax.ShapeDtypeStruct((RG_SLABS, R, LANES), F32),
                   jax.ShapeDtypeStruct((2, B, D_RG), F32)],
        compiler_params=_cparams(58), name=name)(proj, proj, cw, cb, wg, bg, lam, h0)


def _cpow(pr, pi, n):
    wr = jnp.ones(n.shape, F32)
    wi = jnp.zeros(n.shape, F32)
    for k in range(4):
        bit = (n & (1 << k)) != 0
        wr, wi = jnp.where(bit, wr * pr - wi * pi, wr), jnp.where(bit, wr * pi + wi * pr, wi)
        pr, pi = pr * pr - pi * pi, 2.0 * pr * pi
    return wr, wi


def _s5prep_kernel(ar_ref, ai_ref, ldt_ref, arc_ref, aic_ref, ldtc_ref, btr_ref, bti_ref, ctr_ref, cti_ref,
                   k_ref, p_ref, q_ref, l_ref):
    ar = ar_ref[0]
    ai = ai_ref[0]
    dt = jnp.exp(ldt_ref[0])
    mag = jnp.exp(dt * ar)
    lr = mag * jnp.cos(dt * ai)
    li = mag * jnp.sin(dt * ai)
    den = ar * ar + ai * ai
    nre = lr - 1.0
    qre = (nre * ar + li * ai) / den
    qim = (li * ar - nre * ai) / den
    btr = btr_ref[0]
    bti = bti_ref[0]
    bbr = qre * btr - qim * bti
    bbi = qre * bti + qim * btr
    rowblk = lax.broadcasted_iota(jnp.int32, (256, LANES), 0) // CHUNK
    isf = lax.broadcasted_iota(jnp.int32, (256, LANES), 1) < 64

    def powers(n):
        return _cpow(lr, li, n)

    ctr = ctr_ref[0]
    cti = cti_ref[0]

    wr, wi = powers(rowblk)
    zr = wr * bbr - wi * bbi
    zi = wr * bbi + wi * bbr
    zero = jnp.zeros_like(zr)

    def mm(a, b):
        return jnp.dot(a, b, precision=HIGHEST, preferred_element_type=F32)

    mall_f = mm(jnp.where(isf, zr, zero), ctr) - mm(jnp.where(isf, zi, zero), cti)
    mall_b = mm(jnp.where(isf, zero, zr), ctr) - mm(jnp.where(isf, zero, zi), cti)
    blocks = [mall_f[(15 - q) * CHUNK:(16 - q) * CHUNK] for q in range(15)]
    blocks.append(mall_f[0:CHUNK] + mall_b[0:CHUNK])
    blocks += [mall_b[l * CHUNK:(l + 1) * CHUNK] for l in range(1, 16)]
    blocks.append(jnp.zeros((CHUNK, 256), F32))
    gw = jnp.concatenate(blocks, axis=0)
    colblk = lax.broadcasted_iota(jnp.int32, (256, 256), 1) // CHUNK
    kmat = jnp.zeros((256, 256), F32)
    for tp in range(CHUNK):
        kmat = jnp.where(colblk == tp, gw[(15 - tp) * CHUNK:(15 - tp) * CHUNK + 256], kmat)
    k_ref[0] = kmat.astype(BF16)

    wr, wi = powers(jnp.where(isf, 15 - rowblk, rowblk))
    p_ref[0] = jnp.concatenate([wr * bbr - wi * bbi, wr * bbi + wi * bbr], axis=1).astype(BF16)

    arc = arc_ref[0]
    aic = aic_ref[0]
    dtc = jnp.exp(ldtc_ref[0])
    laneblk = lax.broadcasted_iota(jnp.int32, (LANES, 256), 1) // CHUNK
    rowf = lax.broadcasted_iota(jnp.int32, (LANES, 256), 0) < 64
    magc = jnp.exp(dtc * arc)
    lrc = magc * jnp.cos(dtc * aic)
    lic = magc * jnp.sin(dtc * aic)
    vr, vi = _cpow(lrc, lic, jnp.where(rowf, laneblk, 15 - laneblk))
    wr = vr * lrc - vi * lic
    wi = vr * lic + vi * lrc
    q_ref[0] = jnp.concatenate([ctr * wr - cti * wi, -(ctr * wi + cti * wr)], axis=0).astype(BF16)

    pr, pi = lr, li
    for _ in range(4):
        pr, pi = pr * pr - pi * pi, 2.0 * pr * pi
    l_ref[0] = jnp.concatenate([pr, pi], axis=1)


def _s5prep(a_re, a_im, log_dt, b_re, b_im, c_re, c_im):
    G = S5_GROUPS

    def rows(x):
        return jnp.concatenate([x[0], x[1]], axis=-1).reshape(G, 1, LANES)

    ldt = jnp.broadcast_to(log_dt[:, :, None], (2, G, 64))
    ar, ai, ld = rows(a_re), rows(a_im), rows(ldt)
    arc, aic, ldc = (v.reshape(G, LANES, 1) for v in (ar, ai, ld))

    def btile(b):
        bt = jnp.tile(jnp.swapaxes(b, 2, 3), (1, 1, CHUNK, 1))
        return jnp.concatenate([bt[0], bt[1]], axis=-1)

    def ctile(c):
        ct = jnp.tile(jnp.swapaxes(c, 2, 3), (1, 1, 1, CHUNK))
        return jnp.concatenate([ct[0], ct[1]], axis=1)

    vec = pl.BlockSpec((1, 1, LANES), lambda g: (g, 0, 0))
    col = pl.BlockSpec((1, LANES, 1), lambda g: (g, 0, 0))
    bsp = pl.BlockSpec((1, 256, LANES), lambda g: (g, 0, 0))
    csp = pl.BlockSpec((1, LANES, 256), lambda g: (g, 0, 0))
    msp = pl.BlockSpec((1, 256, 256), lambda g: (g, 0, 0))
    return pl.pallas_call(
        _s5prep_kernel, grid=(G,),
        in_specs=[vec, vec, vec, col, col, col, bsp, bsp, csp, csp],
        out_specs=[msp, msp, msp, pl.BlockSpec((1, 1, 256), lambda g: (g, 0, 0))],
        out_shape=[jax.ShapeDtypeStruct((G, 256, 256), BF16)] * 3 + [jax.ShapeDtypeStruct((G, 1, 256), F32)],
        compiler_params=_cparams(32), name="s5prep")(
            ar, ai, ld, arc, aic, ldc, btile(b_re), btile(b_im), ctile(c_re), ctile(c_im))


def _seg_transpose8(vs, seg):
    vs = list(vs)
    for d in (4, 2, 1):
        hi = (seg & d) != 0
        new = list(vs)
        for i in range(8):
            if i & d:
                continue
            j = i + d
            new[i] = jnp.where(hi, pltpu.roll(vs[j], d * CHUNK, axis=1), vs[i])
            new[j] = jnp.where(hi, vs[j], pltpu.roll(vs[i], LANES - d * CHUNK, axis=1))
        vs = new
    return vs


def _s5_kernel(u_ref, k_ref, p_ref, q_ref, l_ref, h0_ref, y_ref, fin_ref, ebuf, ybuf, *, A, Bq, B, lat):
    RP = A * Bq
    nchunk = RP // B
    G = GROUPS_PER_SLAB
    nr = 256
    qn = min(Bq, nr)
    tiles = [(r0, [((r0 + i * qn) // Bq, (r0 + i * qn) % Bq) for i in range(nr // qn)])
             for r0 in range(0, RP, nr)]
    lane_blk = lax.broadcasted_iota(jnp.int32, (nr, LANES), 1) // CHUNK

    for r0, pieces in tiles:
        xs = [jnp.concatenate([u_ref[0, a, tau, q0:q0 + qn, :] for a, q0 in pieces], axis=0)
              for tau in range(CHUNK)]
        halves = [_seg_transpose8(xs[8 * h:8 * h + 8], lane_blk) for h in range(2)]
        for g in range(G):
            ug = jnp.concatenate([halves[0][g], halves[1][g]], axis=1).astype(BF16)
            ybuf[g, r0:r0 + nr, :] = _dot(ug, k_ref[g])
            ebuf[g, r0:r0 + nr, :] = _dot(ug, p_ref[g])

    def off(m):
        if lat:
            return ((m & 1) * GRID_W + (m >> 1)) * B
        return m * B

    fmask = (lax.broadcasted_iota(jnp.int32, (B, 2 * LANES), 1) % LANES) < 64
    fmask1 = lax.broadcasted_iota(jnp.int32, (B, LANES), 1) < 64
    lam = [l_ref[g] for g in range(G)]

    def body(i, carry):
        of = pl.multiple_of(off(i), B)
        ob = pl.multiple_of(off(nchunk - 1 - i), B)
        out = []
        for g in range(G):
            re, im = carry[2 * g], carry[2 * g + 1]
            ef = ebuf[g, pl.ds(of, B), :]
            eb = ebuf[g, pl.ds(ob, B), :]
            s = jnp.concatenate([re, im], axis=1)
            ebuf[g, pl.ds(of, B), :] = jnp.where(fmask, s, ef)
            ebuf[g, pl.ds(ob, B), :] = jnp.where(fmask, eb, s)
            lr = lam[g][:, :LANES]
            li = lam[g][:, LANES:]
            e_re = jnp.where(fmask1, ef[:, :LANES], eb[:, :LANES])
            e_im = jnp.where(fmask1, ef[:, LANES:], eb[:, LANES:])
            out.append(lr * re - li * im + e_re)
            out.append(lr * im + li * re + e_im)
        return tuple(out)

    init = []
    for g in range(G):
        h0 = h0_ref[g]
        init += [h0[:, :LANES], h0[:, LANES:]]
    fin = lax.fori_loop(0, nchunk, body, tuple(init))
    for g in range(G):
        fin_ref[g] = jnp.concatenate([fin[2 * g], fin[2 * g + 1]], axis=1)

    for g in range(G):
        ybuf[g] = ybuf[g] + _dot(ebuf[g].astype(BF16), q_ref[g])
    for r0, pieces in tiles:
        for h in range(2):
            yt = _seg_transpose8([ybuf[g, r0:r0 + nr, h * LANES:(h + 1) * LANES] for g in range(G)], lane_blk)
            for tp in range(8):
                for k, (a, q0) in enumerate(pieces):
                    y_ref[0, a, 8 * h + tp, q0:q0 + qn, :] = yt[tp][k * qn:(k + 1) * qn]


def _s5(proj5, kmat, pmat, qmat, lam16, h0, A, Bq, B, lat, name):
    RP = A * Bq
    G = GROUPS_PER_SLAB
    kern = functools.partial(_s5_kernel, A=A, Bq=Bq, B=B, lat=lat)
    msp = pl.BlockSpec((G, 256, 256), lambda s: (s, 0, 0))
    return pl.pallas_call(
        kern, grid=(S5_SLABS,),
        in_specs=[pl.BlockSpec((1, A, CHUNK, Bq, LANES), lambda s: (s + 2 * RG_SLABS, 0, 0, 0, 0)),
                  msp, msp, msp,
                  pl.BlockSpec((G, 1, 256), lambda s: (s, 0, 0)),
                  pl.BlockSpec((G, B, 256), lambda s: (s, 0, 0))],
        out_specs=[pl.BlockSpec((1, A, CHUNK, Bq, LANES), lambda s: (s, 0, 0, 0, 0)),
                   pl.BlockSpec((G, B, 256), lambda s: (s, 0, 0))],
        out_shape=[jax.ShapeDtypeStruct((S5_SLABS, A, CHUNK, Bq, LANES), F32),
                   jax.ShapeDtypeStruct((S5_GROUPS, B, 256), F32)],
        scratch_shapes=[pltpu.VMEM((G, RP, 256), F32), pltpu.VMEM((G, RP, 256), F32)],
        compiler_params=_cparams(58), name=name)(proj5, kmat, pmat, qmat, lam16, h0)


def _post_kernel(yr_ref, ys_ref, u_ref, x_ref, mod_ref, wo_ref, wg_ref, bgl_ref, d_ref, grg_ref, gs5_ref,
                 o_ref, *, B, tT):
    yr = jnp.concatenate([yr_ref[s] for s in range(RG_SLABS)], axis=1)
    nr = (_rms(yr) * grg_ref[...]).astype(BF16)
    ys = jnp.concatenate([ys_ref[s] for s in range(S5_SLABS)], axis=1)
    u = jnp.concatenate([u_ref[s] for s in range(S5_SLABS)], axis=1)
    ys = jax.nn.gelu(ys + d_ref[...] * u)
    ys = ys * jax.nn.sigmoid(_dot(ys.astype(BF16), wg_ref[...]) + bgl_ref[...])
    ns = (_rms(ys) * gs5_ref[...]).astype(BF16)
    nch = 512
    for c0 in range(0, D_MODEL, nch):
        cs = slice(c0, c0 + nch)
        y = _dot(nr, wo_ref[0:D_RG, cs]) + _dot(ns, wo_ref[D_RG:D_MODEL, cs])
        cols = [_rows_tb_to_bt(y[:, j * LANES:(j + 1) * LANES], B, tT) for j in range(nch // LANES)]
        for b in range(B):
            yb = jnp.concatenate([c[b * tT:(b + 1) * tT] for c in cols], axis=1)
            o_ref[b, :, cs] = x_ref[b, :, cs] + mod_ref[b, 2:3, cs] * yb


def _post(y_rg, y_ssm, proj, x, mod, w_out_bf16, w_glu_bf16, b_glu, s5_d, g_rg, g_s5, name):
    B, T, _ = x.shape
    tT = 512 // B
    rows = tT * B
    kern = functools.partial(_post_kernel, B=B, tT=tT)
    const = lambda i: (0, 0)
    return pl.pallas_call(
        kern, grid=(T // tT,),
        in_specs=[pl.BlockSpec((RG_SLABS, rows, LANES), lambda i: (0, i, 0)),
                  pl.BlockSpec((S5_SLABS, rows, LANES), lambda i: (0, i, 0)),
                  pl.BlockSpec((S5_SLABS, rows, LANES), lambda i: (2 * RG_SLABS // S5_SLABS, i, 0)),
                  pl.BlockSpec((B, tT, D_MODEL), lambda i: (0, i, 0)),
                  pl.BlockSpec((B, N_MOD, D_MODEL), lambda i: (0, 0, 0)),
                  pl.BlockSpec((D_MODEL, D_MODEL), const, pipeline_mode=pl.Buffered(1)),
                  pl.BlockSpec((D_S5, D_S5), const),
                  pl.BlockSpec((1, D_S5), const),
                  pl.BlockSpec((1, D_S5), const),
                  pl.BlockSpec((1, D_RG), const),
                  pl.BlockSpec((1, D_S5), const)],
        out_specs=pl.BlockSpec((B, tT, D_MODEL), lambda i: (0, i, 0)),
        out_shape=jax.ShapeDtypeStruct((B, T, D_MODEL), F32),
        compiler_params=_cparams(56), name=name)(
            y_rg, y_ssm, proj, x, mod, w_out_bf16, w_glu_bf16, b_glu, s5_d, g_rg, g_s5)


def _premoe_kernel(x_ref, mod_ref, g_ref, wr_ref, h_ref, a_ref):
    h = _rms(x_ref[...]) * g_ref[...] * (1.0 + mod_ref[0, 4:5, :]) + mod_ref[0, 3:4, :]
    hhi = h.astype(BF16)
    h_ref[...] = hhi
    hlo = (h - hhi.astype(F32)).astype(BF16)
    w = wr_ref[...]
    whi = w.astype(BF16)
    wlo = (w - whi.astype(F32)).astype(BF16)
    nt = lambda a, b: lax.dot_general(a, b, (((1,), (1,)), ((), ())), preferred_element_type=F32)
    lg = nt(whi, hhi) + nt(wlo, hhi) + nt(whi, hlo)
    e = jnp.exp(lg - jnp.max(lg, axis=0, keepdims=True))
    a_ref[...] = e / jnp.sum(e, axis=0, keepdims=True)


def _premoe(x1, mod, g2, w_router_t, T, name):
    n = x1.shape[0]
    rows = 512
    per_req = T // rows if mod.shape[0] > 1 else n
    return pl.pallas_call(
        _premoe_kernel, grid=(n // rows,),
        in_specs=[pl.BlockSpec((rows, D_MODEL), lambda i: (i, 0)),
                  pl.BlockSpec((1, N_MOD, D_MODEL), lambda i: (i // per_req, 0, 0)),
                  pl.BlockSpec((1, D_MODEL), lambda i: (0, 0)),
                  pl.BlockSpec((N_EXPERTS, D_MODEL), lambda i: (0, 0))],
        out_specs=[pl.BlockSpec((rows, D_MODEL), lambda i: (i, 0)),
                   pl.BlockSpec((N_EXPERTS, rows), lambda i: (0, i))],
        out_shape=[jax.ShapeDtypeStruct((n, D_MODEL), BF16),
                   jax.ShapeDtypeStruct((N_EXPERTS, n), F32)],
        compiler_params=_cparams(40), name=name)(x1, mod, g2, w_router_t)


def _topk_kernel(a_ref, s_ref, st_ref, c_ref, *, cap, T, nreq):
    capf = float(cap)

    def count_ge(r, v):
        return jnp.sum((a_ref[:, r * T:(r + 1) * T] >= v).astype(F32), axis=1, keepdims=True)

    def bit_step(i, thrs):
        bit = jnp.int32(1) << (30 - i)
        out = []
        for r in range(nreq):
            cand = thrs[r] | bit
            ok = count_ge(r, pltpu.bitcast(cand, F32)) >= capf
            out.append(jnp.where(ok, cand, thrs[r]))
        return tuple(out)

    zero = jnp.zeros((N_EXPERTS, 1), jnp.int32)
    thrs = lax.fori_loop(0, 31, bit_step, (zero,) * nreq)

    def mid_step(i, c):
        out = []
        for r in range(nreq):
            lo, hi = c[2 * r], c[2 * r + 1]
            mid = 0.5 * (lo + hi)
            ok = count_ge(r, mid) >= capf
            out += [jnp.where(ok, mid, lo), jnp.where(ok, hi, mid)]
        return tuple(out)

    init = []
    for r in range(nreq):
        init += [pltpu.bitcast(thrs[r], F32), pltpu.bitcast(thrs[r] + 1, F32)]
    bounds = lax.fori_loop(0, 24, mid_step, tuple(init))

    u = (lax.broadcasted_iota(jnp.int32, (T, T), 0)
         <= lax.broadcasted_iota(jnp.int32, (T, T), 1)).astype(F32).astype(BF16)
    nblk = T // TOKEN_BLOCK
    cl = lax.broadcasted_iota(jnp.int32, c_ref.shape, 1)
    cnt = jnp.zeros(c_ref.shape, F32)
    for r in range(nreq):
        aff = a_ref[:, r * T:(r + 1) * T]
        lo, hi = bounds[2 * r], bounds[2 * r + 1]
        gt = aff >= hi
        eq = (aff >= lo) & jnp.logical_not(gt)
        need = capf - jnp.sum(gt.astype(F32), axis=1, keepdims=True)
        pe = _dot(eq.astype(F32).astype(BF16), u)
        sel = gt | (eq & (pe <= need))
        ps = _dot(sel.astype(F32).astype(BF16), u)
        slot = jnp.where(sel, ps - 1.0, -1.0)
        s_ref[:, r * T:(r + 1) * T] = slot
        st_ref[r * T:(r + 1) * T, :] = slot.T
        for k in range(nblk):
            end = (k + 1) * TOKEN_BLOCK
            cnt = jnp.where(cl == r * nblk + k, ps[:, end - 1:end], cnt)
    c_ref[...] = cnt


def _topk(aff_t, T, name):
    n = aff_t.shape[1]
    nreq = n // T
    ncnt = nreq * (T // TOKEN_BLOCK)
    kern = functools.partial(_topk_kernel, cap=T // 8, T=T, nreq=nreq)
    return pl.pallas_call(
        kern, grid=(1,),
        in_specs=[pl.BlockSpec((N_EXPERTS, n), lambda i: (0, 0))],
        out_specs=[pl.BlockSpec((N_EXPERTS, n), lambda i: (0, 0)),
                   pl.BlockSpec((n, N_EXPERTS), lambda i: (0, 0)),
                   pl.BlockSpec((N_EXPERTS, ncnt), lambda i: (0, 0))],
        out_shape=[jax.ShapeDtypeStruct((N_EXPERTS, n), F32),
                   jax.ShapeDtypeStruct((n, N_EXPERTS), F32),
                   jax.ShapeDtypeStruct((N_EXPERTS, ncnt), F32)],
        compiler_params=_cparams(48), name=name)(aff_t)


def _window_starts(start_ref, nblk):
    base = (pl.program_id(0) * nblk + pl.program_id(1)) * N_EXPERTS
    return [pl.multiple_of(start_ref[base + e], 16) for e in range(N_EXPERTS)]


def _gather_kernel(start_ref, ok_ref, h_ref, s_ref, a_ref, xg_ref, gt_ref, *, cap, W, nblk):
    @pl.when(pl.program_id(1) == 0)
    def _():
        xg_ref[...] = jnp.zeros_like(xg_ref)
        gt_ref[...] = jnp.zeros_like(gt_ref)

    h = h_ref[...]
    nch = 512

    def step(w, starts):
        j = lax.broadcasted_iota(jnp.int32, (w, TOKEN_BLOCK), 0).astype(F32)
        ohs = []
        for e in range(N_EXPERTS):
            rel = s_ref[e:e + 1, :]
            if starts is not None:
                rel = rel - starts[e].astype(F32)
            oh = rel == j
            rows = pl.ds(starts[e], w) if starts is not None else slice(0, w)
            gt_ref[e, rows, :] += jnp.sum(jnp.where(oh, a_ref[e:e + 1, :], 0.0), axis=1, keepdims=True)
            ohs.append(oh.astype(F32).astype(BF16))
        ohm = jnp.concatenate(ohs, axis=0)
        for c in range(0, D_MODEL, nch):
            res = _dot(ohm, h[:, c:c + nch])
            for e in range(N_EXPERTS):
                rows = pl.ds(starts[e], w) if starts is not None else slice(0, w)
                xg_ref[e, rows, c:c + nch] += res[e * w:(e + 1) * w].astype(BF16)

    if W == cap:
        step(cap, None)
    else:
        ok = ok_ref[pl.program_id(0) * nblk + pl.program_id(1)] > 0

        @pl.when(ok)
        def _():
            step(W, _window_starts(start_ref, nblk))

        @pl.when(jnp.logical_not(ok))
        def _():
            step(cap, None)


def _gather(h2, slot, aff_t, starts, ok, T, name):
    n = h2.shape[0]
    cap = T // 8
    nreq = n // T
    nblk = T // TOKEN_BLOCK
    kern = functools.partial(_gather_kernel, cap=cap, W=min(SLOT_WINDOW, cap), nblk=nblk)
    tok = lambda b, k, *_: (0, b * nblk + k)
    req = lambda b, k, *_: (0, b, 0)
    return pl.pallas_call(
        kern,
        grid_spec=pltpu.PrefetchScalarGridSpec(
            num_scalar_prefetch=2, grid=(nreq, nblk),
            in_specs=[pl.BlockSpec((TOKEN_BLOCK, D_MODEL), lambda b, k, *_: (b * nblk + k, 0)),
                      pl.BlockSpec((N_EXPERTS, TOKEN_BLOCK), tok),
                      pl.BlockSpec((N_EXPERTS, TOKEN_BLOCK), tok)],
            out_specs=[pl.BlockSpec((N_EXPERTS, cap, D_MODEL), req),
                       pl.BlockSpec((N_EXPERTS, cap, 1), req)]),
        out_shape=[jax.ShapeDtypeStruct((N_EXPERTS, nreq * cap, D_MODEL), BF16),
                   jax.ShapeDtypeStruct((N_EXPERTS, nreq * cap, 1), F32)],
        compiler_params=_cparams(56), name=name)(starts, ok, h2, slot, aff_t)


def _ffn_kernel(xc_ref, xl_ref, gc_ref, gl_ref, w1_ref, w3_ref, w2_ref, yc_ref, yl_ref, xbuf, acc, *, rc, nf, rsub):
    f = pl.program_id(2)

    @pl.when(f == 0)
    def _():
        xbuf[0:rc, :] = xc_ref[0]
        xbuf[rc:, :] = xl_ref[0]

    @pl.when(f == 0)
    def _():
        acc[...] = jnp.zeros_like(acc)

    w1 = w1_ref[0].astype(BF16)
    w3 = w3_ref[0].astype(BF16)
    w2 = w2_ref[0].astype(BF16)
    rows = xbuf.shape[0]
    for r0 in range(0, rows, rsub):
        x = xbuf[r0:r0 + rsub, :]
        hid = jax.nn.silu(_dot(x, w1)) * _dot(x, w3)
        acc[r0:r0 + rsub, :] += _dot(hid.astype(BF16), w2)

    @pl.when(f == nf - 1)
    def _():
        yc_ref[0] = (acc[0:rc, :] * gc_ref[0]).astype(BF16)
        yl_ref[0] = (acc[rc:, :] * gl_ref[0]).astype(BF16)


def _ffn(xg_c, xg_l, gt_c, gt_l, w1, w3, w2):
    rc = xg_c.shape[1] // 2
    rl = xg_l.shape[1] // 2
    bf = 256
    nf = D_MODEL // bf
    kern = functools.partial(_ffn_kernel, rc=rc, nf=nf, rsub=(rc + rl) // 2)
    rowsp = lambda r, w: pl.BlockSpec((1, r, w), lambda e, h, f: (e, h, 0))
    return pl.pallas_call(
        kern, grid=(N_EXPERTS, 2, nf),
        in_specs=[rowsp(rc, D_MODEL), rowsp(rl, D_MODEL), rowsp(rc, 1), rowsp(rl, 1),
                  pl.BlockSpec((1, D_MODEL, bf), lambda e, h, f: (e, 0, f)),
                  pl.BlockSpec((1, D_MODEL, bf), lambda e, h, f: (e, 0, f)),
                  pl.BlockSpec((1, bf, D_MODEL), lambda e, h, f: (e, f, 0))],
        out_specs=[rowsp(rc, D_MODEL), rowsp(rl, D_MODEL)],
        out_shape=[jax.ShapeDtypeStruct(xg_c.shape, BF16), jax.ShapeDtypeStruct(xg_l.shape, BF16)],
        scratch_shapes=[pltpu.VMEM((rc + rl, D_MODEL), BF16), pltpu.VMEM((rc + rl, D_MODEL), F32)],
        compiler_params=_cparams(56), name="ffn")(xg_c, xg_l, gt_c, gt_l, w1, w3, w2)


def _expand_consts(w):
    lane = jnp.arange(N_EXPERTS * w)
    expand = (jnp.arange(N_EXPERTS)[:, None] == (lane // w)[None, :]).astype(BF16)
    return expand, (lane % w).astype(F32)[None, :]


def _combine_kernel(start_ref, ok_ref, x_ref, y_ref, st_ref, sv_ref, mod_ref, fg_ref, exw_ref, jlw_ref,
                    exd_ref, jld_ref, o_ref, stage, *, cap, W, nblk):
    st = st_ref[...]

    def finish(moe):
        x2 = x_ref[...] + mod_ref[0, 5:6, :] * moe
        o_ref[...] = _rms(x2) * fg_ref[...]

    def dense():
        rep = _dot(st.astype(BF16), exd_ref[...])
        oh = (rep == jld_ref[...]).astype(F32).astype(BF16)
        finish(_dot(oh, y_ref[...].reshape(N_EXPERTS * cap, D_MODEL)))

    if W == cap:
        dense()
        return
    ok = ok_ref[pl.program_id(0) * nblk + pl.program_id(1)] > 0

    @pl.when(ok)
    def _():
        starts = _window_starts(start_ref, nblk)
        for e in range(N_EXPERTS):
            stage[e * W:(e + 1) * W, :] = y_ref[e, pl.ds(starts[e], W), :]
        rep = _dot((st - sv_ref[0]).astype(BF16), exw_ref[...])
        oh = (rep == jlw_ref[...]).astype(F32).astype(BF16)
        finish(_dot(oh, stage[...]))

    @pl.when(jnp.logical_not(ok))
    def _():
        dense()


def _combine(x1, ybuf, slot_t, starts, ok, starts_f, mod, final_g, T, name):
    n = x1.shape[0]
    cap = T // 8
    nblk = T // TOKEN_BLOCK
    W = min(SLOT_WINDOW, cap)
    multi = mod.shape[0] > 1
    exw, jlw = _expand_consts(W)
    exd, jld = _expand_consts(cap)
    kern = functools.partial(_combine_kernel, cap=cap, W=W, nblk=nblk)
    tok = lambda b, k, *_: (b * nblk + k, 0)
    const = lambda b, k, *_: (0, 0)
    return pl.pallas_call(
        kern,
        grid_spec=pltpu.PrefetchScalarGridSpec(
            num_scalar_prefetch=2, grid=(n // T, nblk),
            in_specs=[pl.BlockSpec((TOKEN_BLOCK, D_MODEL), tok),
                      pl.BlockSpec((N_EXPERTS, cap, D_MODEL), lambda b, k, *_: (0, b, 0)),
                      pl.BlockSpec((TOKEN_BLOCK, N_EXPERTS), tok),
                      pl.BlockSpec((1, 1, N_EXPERTS), lambda b, k, *_: (b * nblk + k, 0, 0)),
                      pl.BlockSpec((1, N_MOD, D_MODEL), lambda b, k, *_: (b if multi else 0, 0, 0)),
                      pl.BlockSpec((1, D_MODEL), const),
                      pl.BlockSpec(exw.shape, const), pl.BlockSpec(jlw.shape, const),
                      pl.BlockSpec(exd.shape, const), pl.BlockSpec(jld.shape, const)],
            out_specs=pl.BlockSpec((TOKEN_BLOCK, D_MODEL), tok),
            scratch_shapes=[pltpu.VMEM((N_EXPERTS * W, D_MODEL), BF16)]),
        out_shape=jax.ShapeDtypeStruct((n, D_MODEL), F32),
        compiler_params=_cparams(56), name=name)(starts, ok, x1, ybuf, slot_t, starts_f, mod, final_g,
                                                 exw, jlw, exd, jld)


def _window_tables(cnt, T):
    cap = T // 8
    nblk = T // TOKEN_BLOCK
    nreq = cnt.shape[1] // nblk
    W = min(SLOT_WINDOW, cap)
    end = cnt.astype(jnp.int32).reshape(N_EXPERTS, nreq, nblk)
    first = jnp.concatenate([jnp.zeros_like(end[..., :1]), end[..., :-1]], axis=-1)
    start = jnp.minimum((first // 16) * 16, cap - W)
    ok = jnp.all(end - start <= W, axis=0).astype(jnp.int32).reshape(-1)
    start = start.transpose(1, 2, 0)
    return start.reshape(-1), ok, start.astype(F32).reshape(nreq * nblk, 1, N_EXPERTS)


def _blockdiag_pairs(w):
    w = w.reshape(2, RG_SLABS, 2, 64, 64)
    bd = jnp.einsum('dshij,hk->dshikj', w, jnp.eye(2, dtype=w.dtype))
    return bd.reshape(2, RG_SLABS, LANES, LANES).transpose(1, 0, 2, 3)


def kernel(x_prompt, x_sample, state_rglru, state_s5_re, state_s5_im, c, c_ctx, norm1_g, norm2_g, w_ada, b_ada, w_in, rg_conv_w, rg_conv_b, rg_wa, rg_ba, rg_wx, rg_bx, rg_lam, s5_a_re, s5_a_im, s5_log_dt, s5_b_re, s5_b_im, s5_c_re, s5_c_im, s5_d, s5_w_glu, s5_b_glu, gnorm_rg, gnorm_s5, w_out, w_router, w1, w3, w2, final_g):
    l = 0
    Bp, Tp, _ = x_prompt.shape
    Bs, Ts, _ = x_sample.shape

    c16 = jnp.concatenate([c, c_ctx[None], jnp.zeros((16 - Bs - 1, D_MODEL), F32)], axis=0)
    mod = _adaln(c16, w_ada[l], b_ada[l]).reshape(16, N_MOD, D_MODEL)
    mod_lat = mod[:Bs]
    mod_ctx1 = mod[Bs:Bs + 1]
    mod_ctx = jnp.broadcast_to(mod_ctx1, (Bp, N_MOD, D_MODEL))

    w_in_b = w_in[l].astype(BF16)
    w_out_b = w_out[l].astype(BF16)
    w_glu_b = s5_w_glu[l].astype(BF16)
    row = lambda v: v.reshape(1, -1)

    wg = (0.5 * jnp.concatenate([_blockdiag_pairs(rg_wa[l]), _blockdiag_pairs(rg_wx[l])], axis=-1)).astype(BF16)
    bg = 0.5 * jnp.concatenate([rg_ba[l].reshape(2, RG_SLABS, LANES), rg_bx[l].reshape(2, RG_SLABS, LANES)],
                               axis=-1).transpose(1, 0, 2).reshape(RG_SLABS, 2, 1, 2 * LANES)
    kmat, pmat, qmat, lam16 = _s5prep(s5_a_re[l], s5_a_im[l], s5_log_dt[l], s5_b_re[l], s5_b_im[l],
                                       s5_c_re[l], s5_c_im[l])

    def mixer(x, modp, rg_h0, s5_h0, lat, tag):
        B, T, _ = x.shape
        R = T * B
        proj = _inproj(x, modp, row(norm1_g[l]), w_in_b, "inproj_" + tag)
        y_rg, fin_rg = _rglru(proj, rg_conv_w[l], row(rg_conv_b[l]), wg, bg, rg_lam[l], rg_h0, B, T,
                              "rglru_" + tag)
        if lat:
            A, Bq = T // GRID_W // CHUNK, GRID_W * B
        else:
            A, Bq = T // CHUNK, B
        y_ssm, fin_s5 = _s5(proj.reshape(N_SLAB, A, CHUNK, Bq, LANES), kmat, pmat, qmat, lam16, s5_h0,
                            A, Bq, B, lat, "s5_" + tag)
        x1 = _post(y_rg, y_ssm.reshape(S5_SLABS, R, LANES), proj, x, modp, w_out_b, w_glu_b,
                   row(s5_b_glu[l]), row(s5_d[l]), row(gnorm_rg[l]), row(gnorm_s5[l]), "post_" + tag)
        return x1.reshape(R, D_MODEL), fin_rg, fin_s5

    z_rg = jnp.zeros((2, Bp, D_RG), F32)
    z_s5 = jnp.zeros((S5_GROUPS, Bp, 256), F32)
    x1_c, fin_rg, fin_s5 = mixer(x_prompt, mod_ctx, z_rg, z_s5, False, "ctx")
    rg_h0 = state_rglru[:, l].transpose(1, 0, 2)
    s5_h0 = jnp.stack([state_s5_re[:, l, 0], state_s5_re[:, l, 1], state_s5_im[:, l, 0], state_s5_im[:, l, 1]],
                      axis=0)
    s5_h0 = s5_h0.transpose(2, 1, 0, 3).reshape(S5_GROUPS, Bs, 256)
    x1_s, _, _ = mixer(x_sample, mod_lat, rg_h0, s5_h0, True, "lat")

    w_router_t = w_router[l].T
    g2 = row(norm2_g[l])

    def route(x1, modp, T, tag):
        h2, aff_t = _premoe(x1, modp, g2, w_router_t, T, "premoe_" + tag)
        slot, slot_t, cnt = _topk(aff_t, T, "topk_" + tag)
        win = _window_tables(cnt, T)
        xg, gates = _gather(h2, slot, aff_t, win[0], win[1], T, "gather_" + tag)
        return xg, gates, slot_t, win

    xg_c, gt_c, st_c, win_c = route(x1_c, mod_ctx1, Tp, "ctx")
    xg_s, gt_s, st_s, win_s = route(x1_s, mod_lat, Ts, "lat")
    y_c, y_s = _ffn(xg_c, xg_s, gt_c, gt_s, w1[l], w3[l], w2[l])
    fg = row(final_g)
    y_prompt = _combine(x1_c, y_c, st_c, *win_c, mod_ctx1, fg, Tp, "combine_ctx").reshape(Bp, Tp, D_MODEL)
    y_sample = _combine(x1_s, y_s, st_s, *win_s, mod_lat, fg, Ts, "combine_lat").reshape(Bs, Ts, D_MODEL)

    new_rg = fin_rg.transpose(1, 0, 2)[:, None]
    f4 = fin_s5.reshape(S5_GROUPS, Bp, 4, 64).transpose(1, 2, 0, 3)
    new_re = f4[:, None, 0:2]
    new_im = f4[:, None, 2:4]
    return (y_prompt, y_sample, new_rg, new_re, new_im)
```

```python
import functools

import jax
import jax.numpy as jnp
from jax import lax
from jax.experimental import pallas as pl
from jax.experimental.pallas import tpu as pltpu

F32 = jnp.float32
BF16 = jnp.bfloat16
HIGHEST = lax.Precision.HIGHEST

LANES = 128
D_MODEL = 2048
D_RG = 1536
D_S5 = 512
D_IN = 2 * D_RG + D_S5
N_SLAB = D_IN // LANES
RG_SLABS = D_RG // LANES
S5_SLABS = D_S5 // LANES
S5_GROUPS = 32
GROUPS_PER_SLAB = 8
CHUNK = 16
N_EXPERTS = 16
N_MOD = 6
EPS = 1e-6
RG_C = 8.0
LOG2_E = 1.4426950408889634
GRID_W = 64
TOKEN_BLOCK = 256
SLOT_WINDOW = 64
MIB = 1024 * 1024


def _cparams(vmem_mib):
    return pltpu.CompilerParams(vmem_limit_bytes=vmem_mib * MIB)


def _dot(a, b):
    return jnp.dot(a, b, preferred_element_type=F32)


def _rms(x):
    return x * lax.rsqrt(jnp.mean(x * x, axis=-1, keepdims=True) + EPS)


def _adaln_kernel(c_ref, w_ref, b_ref, o_ref):
    s = jax.nn.silu(c_ref[...]).astype(BF16)
    o_ref[...] = _dot(s, w_ref[...].astype(BF16)) + b_ref[...]


def _adaln(c16, w_ada, b_ada):
    n = w_ada.shape[1]
    bn = 1024
    return pl.pallas_call(
        _adaln_kernel, grid=(n // bn,),
        in_specs=[pl.BlockSpec((16, D_MODEL), lambda j: (0, 0)),
                  pl.BlockSpec((D_MODEL, bn), lambda j: (0, j)),
                  pl.BlockSpec((1, bn), lambda j: (0, j))],
        out_specs=pl.BlockSpec((16, bn), lambda j: (0, j)),
        out_shape=jax.ShapeDtypeStruct((16, n), F32),
        compiler_params=_cparams(40), name="adaln")(c16, w_ada, b_ada.reshape(1, n))


def _sublane_transpose8(vs, sub):
    vs = list(vs)
    for d in (4, 2, 1):
        hi = (sub & d) != 0
        new = list(vs)
        for i in range(8):
            if i & d:
                continue
            new[i] = jnp.where(hi, pltpu.roll(vs[i + d], d, axis=1), vs[i])
            new[i + d] = jnp.where(hi, vs[i + d], pltpu.roll(vs[i], 8 - d, axis=1))
        vs = new
    return vs


def _rows_bt_to_tb(p, B, tT):
    sub = lax.broadcasted_iota(jnp.int32, (tT // 8, 8, LANES), 1)
    groups = []
    for g in range(B // 8):
        vs = [p[(8 * g + i) * tT:(8 * g + i + 1) * tT].reshape(tT // 8, 8, LANES) for i in range(8)]
        groups.append(_sublane_transpose8(vs, sub))
    out = jnp.stack([jnp.stack([grp[a] for grp in groups], axis=1) for a in range(8)], axis=1)
    return out.reshape(tT * B, LANES)


def _rows_tb_to_bt(y, B, tT):
    ng = B // 8
    y4 = y.reshape(tT // 8, 8 * ng, 8, LANES)
    sub = lax.broadcasted_iota(jnp.int32, (tT // 8, 8, LANES), 1)
    out = []
    for g in range(ng):
        out += _sublane_transpose8([y4[:, a * ng + g] for a in range(8)], sub)
    return jnp.concatenate([v.reshape(tT, LANES) for v in out], axis=0)


def _inproj_kernel(x_ref, mod_ref, g_ref, w_ref, o_ref, *, B, tT, nch):
    x = x_ref[...]
    shift = mod_ref[:, 0:1, :]
    scale = mod_ref[:, 1:2, :]
    h = _rms(x) * (g_ref[...] * (1.0 + scale)) + shift
    hb = h.reshape(B * tT, D_MODEL).astype(BF16)
    for c0 in range(0, D_IN, nch):
        p = _dot(hb, w_ref[:, c0:c0 + nch])
        if D_RG <= c0 < 2 * D_RG:
            p = jax.nn.gelu(p)
        for j in range(nch // LANES):
            o_ref[c0 // LANES + j] = _rows_bt_to_tb(p[:, j * LANES:(j + 1) * LANES], B, tT)


def _inproj(x, mod, g1, w_in_bf16, name):
    B, T, _ = x.shape
    tT = 512 // B
    kern = functools.partial(_inproj_kernel, B=B, tT=tT, nch=512)
    return pl.pallas_call(
        kern, grid=(T // tT,),
        in_specs=[pl.BlockSpec((B, tT, D_MODEL), lambda i: (0, i, 0)),
                  pl.BlockSpec((B, N_MOD, D_MODEL), lambda i: (0, 0, 0)),
                  pl.BlockSpec((1, D_MODEL), lambda i: (0, 0)),
                  pl.BlockSpec((D_MODEL, D_IN), lambda i: (0, 0), pipeline_mode=pl.Buffered(1))],
        out_specs=pl.BlockSpec((N_SLAB, tT * B, LANES), lambda i: (0, i, 0)),
        out_shape=jax.ShapeDtypeStruct((N_SLAB, T * B, LANES), F32),
        compiler_params=_cparams(56), name=name)(x, mod, g1, w_in_bf16)


def _rglru_kernel(xb_ref, gt_ref, cw_ref, cb_ref, wg_ref, bg_ref, lam_ref, h0_ref, y_ref, fin_ref,
                  *, B, T, CT):
    RC = CT * B
    nc = T // CT
    cw = [cw_ref[k:k + 1, :] for k in range(4)]
    cb = cb_ref[...]
    hl = [(-0.5 * RG_C * LOG2_E) * jax.nn.softplus(-lam_ref[d:d + 1, :]) for d in range(2)]

    def chunk(n, d):
        t0 = n * CT
        row0 = pl.multiple_of(t0 * B, RC)
        cur = xb_ref[0, pl.ds(row0, RC), :]
        p0 = pl.multiple_of(jnp.maximum(t0 - 1, 0) * B, B)
        prev = xb_ref[0, pl.ds(p0, B), :] * jnp.where(t0 > 0, 1.0, 0.0)
        n0 = pl.multiple_of(jnp.minimum(t0 + CT, T - 2) * B, B)
        nxt = xb_ref[0, pl.ds(n0, 2 * B), :] * jnp.where(t0 + CT < T, 1.0, 0.0)
        ext = jnp.concatenate([prev, cur, nxt], axis=0)
        xc = (cw[0] * ext[0:RC] + cw[1] * ext[B:B + RC] + cw[2] * ext[2 * B:2 * B + RC]
              + cw[3] * ext[3 * B:3 * B + RC] + cb)
        th = jnp.tanh(_dot(xc.astype(BF16), wg_ref[0, d]) + bg_ref[0, d])
        i = 0.5 + 0.5 * th[:, LANES:]
        a = jnp.exp2(hl[d] + hl[d] * th[:, :LANES])
        om = 1.0 - a * a
        bb = jnp.where(om > 0.0, om * lax.rsqrt(om), 0.0) * (i * xc)
        return a, bb, row0

    def make_body(finish):
        def body(n, carry):
            hf, hb = carry
            af, bf, rf = chunk(n, 0)
            ab, bk, rb = chunk(nc - 1 - n, 1)
            if finish:
                glf = gt_ref[0, pl.ds(rf, RC), :]
                glb = gt_ref[0, pl.ds(rb, RC), :]
                of = y_ref[0, pl.ds(rf, RC), :]
                ob = y_ref[0, pl.ds(rb, RC), :]
            for t in range(CT):
                u = CT - 1 - t
                hf = af[t * B:(t + 1) * B] * hf + bf[t * B:(t + 1) * B]
                hb = ab[u * B:(u + 1) * B] * hb + bk[u * B:(u + 1) * B]
                rows_f = pl.ds(rf + t * B, B)
                rows_b = pl.ds(rb + u * B, B)
                if finish:
                    y_ref[0, rows_f, :] = (of[t * B:(t + 1) * B] + hf) * glf[t * B:(t + 1) * B]
                    y_ref[0, rows_b, :] = (ob[u * B:(u + 1) * B] + hb) * glb[u * B:(u + 1) * B]
                else:
                    y_ref[0, rows_f, :] = hf
                    y_ref[0, rows_b, :] = hb
            return hf, hb
        return body

    carry = lax.fori_loop(0, nc // 2, make_body(False), (h0_ref[0], h0_ref[1]))
    hf, hb = lax.fori_loop(nc // 2, nc, make_body(True), carry)
    fin_ref[0] = hf
    fin_ref[1] = hb


def _rglru(proj, cw, cb, wg, bg, lam, h0, B, T, name):
    R = T * B
    kern = functools.partial(_rglru_kernel, B=B, T=T, CT=512 // B)
    return pl.pallas_call(
        kern, grid=(RG_SLABS,),
        in_specs=[pl.BlockSpec((1, R, LANES), lambda s: (s, 0, 0)),
                  pl.BlockSpec((1, R, LANES), lambda s: (s + RG_SLABS, 0, 0)),
                  pl.BlockSpec((4, LANES), lambda s: (0, s)),
                  pl.BlockSpec((1, LANES), lambda s: (0, s)),
                  pl.BlockSpec((1, 2, LANES, 2 * LANES), lambda s: (s, 0, 0, 0)),
                  pl.BlockSpec((1, 2, 1, 2 * LANES), lambda s: (s, 0, 0, 0)),
                  pl.BlockSpec((2, LANES), lambda s: (0, s)),
                  pl.BlockSpec((2, B, LANES), lambda s: (0, 0, s))],
        out_specs=[pl.BlockSpec((1, R, LANES), lambda s: (s, 0, 0)),
                   pl.BlockSpec((2, B, LANES), lambda s: (0, 0, s))],
        out_shape=[jax.ShapeDtypeStruct((RG_SLABS, R, LANES), F32),
                   jax.ShapeDtypeStruct((2, B, D_RG), F32)],
        compiler_params=_cparams(58), name=name)(proj, proj, cw, cb, wg, bg, lam, h0)


def _cpow(pr, pi, n):
    wr = jnp.ones(n.shape, F32)
    wi = jnp.zeros(n.shape, F32)
    for k in range(4):
        bit = (n & (1 << k)) != 0
        wr, wi = jnp.where(bit, wr * pr - wi * pi, wr), jnp.where(bit, wr * pi + wi * pr, wi)
        pr, pi = pr * pr - pi * pi, 2.0 * pr * pi
    return wr, wi


def _s5prep_kernel(ar_ref, ai_ref, ldt_ref, arc_ref, aic_ref, ldtc_ref, btr_ref, bti_ref, ctr_ref, cti_ref,
                   k_ref, p_ref, q_ref, l_ref):
    ar = ar_ref[0]
    ai = ai_ref[0]
    dt = jnp.exp(ldt_ref[0])
    mag = jnp.exp(dt * ar)
    lr = mag * jnp.cos(dt * ai)
    li = mag * jnp.sin(dt * ai)
    den = ar * ar + ai * ai
    nre = lr - 1.0
    qre = (nre * ar + li * ai) / den
    qim = (li * ar - nre * ai) / den
    btr = btr_ref[0]
    bti = bti_ref[0]
    bbr = qre * btr - qim * bti
    bbi = qre * bti + qim * btr
    rowblk = lax.broadcasted_iota(jnp.int32, (256, LANES), 0) // CHUNK
    isf = lax.broadcasted_iota(jnp.int32, (256, LANES), 1) < 64

    def powers(n):
        return _cpow(lr, li, n)

    ctr = ctr_ref[0]
    cti = cti_ref[0]

    wr, wi = powers(rowblk)
    zr = wr * bbr - wi * bbi
    zi = wr * bbi + wi * bbr
    zero = jnp.zeros_like(zr)

    def mm(a, b):
        return jnp.dot(a, b, precision=HIGHEST, preferred_element_type=F32)

    cri = jnp.concatenate([ctr, cti], axis=0)
    mall_f = mm(jnp.concatenate([jnp.where(isf, zr, zero), jnp.where(isf, -zi, zero)], axis=1), cri)
    mall_b = mm(jnp.concatenate([jnp.where(isf, zero, zr), jnp.where(isf, zero, -zi)], axis=1), cri)
    blocks = [mall_f[(15 - q) * CHUNK:(16 - q) * CHUNK] for q in range(15)]
    blocks.append(mall_f[0:CHUNK] + mall_b[0:CHUNK])
    blocks += [mall_b[l * CHUNK:(l + 1) * CHUNK] for l in range(1, 16)]
    blocks.append(jnp.zeros((CHUNK, 256), F32))
    gw = jnp.concatenate(blocks, axis=0)
    colblk = lax.broadcasted_iota(jnp.int32, (256, 256), 1) // CHUNK
    kmat = jnp.zeros((256, 256), F32)
    for tp in range(CHUNK):
        kmat = jnp.where(colblk == tp, gw[(15 - tp) * CHUNK:(15 - tp) * CHUNK + 256], kmat)
    k_ref[0] = kmat.astype(BF16)

    wr, wi = powers(jnp.where(isf, 15 - rowblk, rowblk))
    p_ref[0] = jnp.concatenate([wr * bbr - wi * bbi, wr * bbi + wi * bbr], axis=1).astype(BF16)

    arc = arc_ref[0]
    aic = aic_ref[0]
    dtc = jnp.exp(ldtc_ref[0])
    laneblk = lax.broadcasted_iota(jnp.int32, (LANES, 256), 1) // CHUNK
    rowf = lax.broadcasted_iota(jnp.int32, (LANES, 256), 0) < 64
    magc = jnp.exp(dtc * arc)
    lrc = magc * jnp.cos(dtc * aic)
    lic = magc * jnp.sin(dtc * aic)
    vr, vi = _cpow(lrc, lic, jnp.where(rowf, laneblk, 15 - laneblk))
    wr = vr * lrc - vi * lic
    wi = vr * lic + vi * lrc
    q_ref[0] = jnp.concatenate([ctr * wr - cti * wi, -(ctr * wi + cti * wr)], axis=0).astype(BF16)

    pr, pi = lr, li
    for _ in range(4):
        pr, pi = pr * pr - pi * pi, 2.0 * pr * pi
    l_ref[0] = jnp.concatenate([pr, pi], axis=1)


def _s5prep(a_re, a_im, log_dt, b_re, b_im, c_re, c_im):
    G = S5_GROUPS

    def rows(x):
        return jnp.concatenate([x[0], x[1]], axis=-1).reshape(G, 1, LANES)

    ldt = jnp.broadcast_to(log_dt[:, :, None], (2, G, 64))
    ar, ai, ld = rows(a_re), rows(a_im), rows(ldt)
    arc, aic, ldc = (v.reshape(G, LANES, 1) for v in (ar, ai, ld))

    def btile(b):
        bt = jnp.tile(jnp.swapaxes(b, 2, 3), (1, 1, CHUNK, 1))
        return jnp.concatenate([bt[0], bt[1]], axis=-1)

    def ctile(c):
        ct = jnp.tile(jnp.swapaxes(c, 2, 3), (1, 1, 1, CHUNK))
        return jnp.concatenate([ct[0], ct[1]], axis=1)

    vec = pl.BlockSpec((1, 1, LANES), lambda g: (g, 0, 0))
    col = pl.BlockSpec((1, LANES, 1), lambda g: (g, 0, 0))
    bsp = pl.BlockSpec((1, 256, LANES), lambda g: (g, 0, 0))
    csp = pl.BlockSpec((1, LANES, 256), lambda g: (g, 0, 0))
    msp = pl.BlockSpec((1, 256, 256), lambda g: (g, 0, 0))
    return pl.pallas_call(
        _s5prep_kernel, grid=(G,),
        in_specs=[vec, vec, vec, col, col, col, bsp, bsp, csp, csp],
        out_specs=[msp, msp, msp, pl.BlockSpec((1, 1, 256), lambda g: (g, 0, 0))],
        out_shape=[jax.ShapeDtypeStruct((G, 256, 256), BF16)] * 3 + [jax.ShapeDtypeStruct((G, 1, 256), F32)],
        compiler_params=_cparams(32), name="s5prep")(
            ar, ai, ld, arc, aic, ldc, btile(b_re), btile(b_im), ctile(c_re), ctile(c_im))


def _seg_transpose8(vs, seg):
    vs = list(vs)
    for d in (4, 2, 1):
        hi = (seg & d) != 0
        new = list(vs)
        for i in range(8):
            if i & d:
                continue
            j = i + d
            new[i] = jnp.where(hi, pltpu.roll(vs[j], d * CHUNK, axis=1), vs[i])
            new[j] = jnp.where(hi, vs[j], pltpu.roll(vs[i], LANES - d * CHUNK, axis=1))
        vs = new
    return vs


def _s5_kernel(u_ref, k_ref, p_ref, q_ref, l_ref, h0_ref, y_ref, fin_ref, ebuf, ybuf, *, A, Bq, B, lat):
    RP = A * Bq
    nchunk = RP // B
    G = GROUPS_PER_SLAB
    nr = 256
    qn = min(Bq, nr)
    tiles = [(r0, [((r0 + i * qn) // Bq, (r0 + i * qn) % Bq) for i in range(nr // qn)])
             for r0 in range(0, RP, nr)]
    lane_blk = lax.broadcasted_iota(jnp.int32, (nr, LANES), 1) // CHUNK

    for r0, pieces in tiles:
        xs = [jnp.concatenate([u_ref[0, a, tau, q0:q0 + qn, :] for a, q0 in pieces], axis=0)
              for tau in range(CHUNK)]
        halves = [_seg_transpose8(xs[8 * h:8 * h + 8], lane_blk) for h in range(2)]
        for g in range(G):
            ug = jnp.concatenate([halves[0][g], halves[1][g]], axis=1).astype(BF16)
            ybuf[g, r0:r0 + nr, :] = _dot(ug, k_ref[g])
            ebuf[g, r0:r0 + nr, :] = _dot(ug, p_ref[g])

    def off(m):
        if lat:
            return ((m & 1) * GRID_W + (m >> 1)) * B
        return m * B

    fmask = (lax.broadcasted_iota(jnp.int32, (B, 2 * LANES), 1) % LANES) < 64
    fmask1 = lax.broadcasted_iota(jnp.int32, (B, LANES), 1) < 64
    lam = [l_ref[g] for g in range(G)]

    def body(i, carry):
        of = pl.multiple_of(off(i), B)
        ob = pl.multiple_of(off(nchunk - 1 - i), B)
        out = []
        for g in range(G):
            re, im = carry[2 * g], carry[2 * g + 1]
            ef = ebuf[g, pl.ds(of, B), :]
            eb = ebuf[g, pl.ds(ob, B), :]
            s = jnp.concatenate([re, im], axis=1)
            ebuf[g, pl.ds(of, B), :] = jnp.where(fmask, s, ef)
            ebuf[g, pl.ds(ob, B), :] = jnp.where(fmask, eb, s)
            lr = lam[g][:, :LANES]
            li = lam[g][:, LANES:]
            e_re = jnp.where(fmask1, ef[:, :LANES], eb[:, :LANES])
            e_im = jnp.where(fmask1, ef[:, LANES:], eb[:, LANES:])
            out.append(lr * re - li * im + e_re)
            out.append(lr * im + li * re + e_im)
        return tuple(out)

    init = []
    for g in range(G):
        h0 = h0_ref[g]
        init += [h0[:, :LANES], h0[:, LANES:]]
    fin = lax.fori_loop(0, nchunk, body, tuple(init))
    for g in range(G):
        fin_ref[g] = jnp.concatenate([fin[2 * g], fin[2 * g + 1]], axis=1)

    for g in range(G):
        ybuf[g] = ybuf[g] + _dot(ebuf[g].astype(BF16), q_ref[g])
    for r0, pieces in tiles:
        for h in range(2):
            yt = _seg_transpose8([ybuf[g, r0:r0 + nr, h * LANES:(h + 1) * LANES] for g in range(G)], lane_blk)
            for tp in range(8):
                for k, (a, q0) in enumerate(pieces):
                    y_ref[0, a, 8 * h + tp, q0:q0 + qn, :] = yt[tp][k * qn:(k + 1) * qn]


def _s5(proj5, kmat, pmat, qmat, lam16, h0, A, Bq, B, lat, name):
    RP = A * Bq
    G = GROUPS_PER_SLAB
    kern = functools.partial(_s5_kernel, A=A, Bq=Bq, B=B, lat=lat)
    msp = pl.BlockSpec((G, 256, 256), lambda s: (s, 0, 0))
    return pl.pallas_call(
        kern, grid=(S5_SLABS,),
        in_specs=[pl.BlockSpec((1, A, CHUNK, Bq, LANES), lambda s: (s + 2 * RG_SLABS, 0, 0, 0, 0)),
                  msp, msp, msp,
                  pl.BlockSpec((G, 1, 256), lambda s: (s, 0, 0)),
                  pl.BlockSpec((G, B, 256), lambda s: (s, 0, 0))],
        out_specs=[pl.BlockSpec((1, A, CHUNK, Bq, LANES), lambda s: (s, 0, 0, 0, 0)),
                   pl.BlockSpec((G, B, 256), lambda s: (s, 0, 0))],
        out_shape=[jax.ShapeDtypeStruct((S5_SLABS, A, CHUNK, Bq, LANES), F32),
                   jax.ShapeDtypeStruct((S5_GROUPS, B, 256), F32)],
        scratch_shapes=[pltpu.VMEM((G, RP, 256), F32), pltpu.VMEM((G, RP, 256), F32)],
        compiler_params=_cparams(58), name=name)(proj5, kmat, pmat, qmat, lam16, h0)


def _post_kernel(yr_ref, ys_ref, u_ref, x_ref, mod_ref, wo_ref, wg_ref, bgl_ref, d_ref, grg_ref, gs5_ref,
                 o_ref, *, B, tT):
    yr = jnp.concatenate([yr_ref[s] for s in range(RG_SLABS)], axis=1)
    nr = (_rms(yr) * grg_ref[...]).astype(BF16)
    ys = jnp.concatenate([ys_ref[s] for s in range(S5_SLABS)], axis=1)
    u = jnp.concatenate([u_ref[s] for s in range(S5_SLABS)], axis=1)
    ys = jax.nn.gelu(ys + d_ref[...] * u)
    ys = ys * jax.nn.sigmoid(_dot(ys.astype(BF16), wg_ref[...]) + bgl_ref[...])
    ns = (_rms(ys) * gs5_ref[...]).astype(BF16)
    nch = 512
    for c0 in range(0, D_MODEL, nch):
        cs = slice(c0, c0 + nch)
        y = _dot(nr, wo_ref[0:D_RG, cs]) + _dot(ns, wo_ref[D_RG:D_MODEL, cs])
        cols = [_rows_tb_to_bt(y[:, j * LANES:(j + 1) * LANES], B, tT) for j in range(nch // LANES)]
        for b in range(B):
            yb = jnp.concatenate([c[b * tT:(b + 1) * tT] for c in cols], axis=1)
            o_ref[b, :, cs] = x_ref[b, :, cs] + mod_ref[b, 2:3, cs] * yb


def _post(y_rg, y_ssm, proj, x, mod, w_out_bf16, w_glu_bf16, b_glu, s5_d, g_rg, g_s5, name):
    B, T, _ = x.shape
    tT = 512 // B
    rows = tT * B
    kern = functools.partial(_post_kernel, B=B, tT=tT)
    const = lambda i: (0, 0)
    return pl.pallas_call(
        kern, grid=(T // tT,),
        in_specs=[pl.BlockSpec((RG_SLABS, rows, LANES), lambda i: (0, i, 0)),
                  pl.BlockSpec((S5_SLABS, rows, LANES), lambda i: (0, i, 0)),
                  pl.BlockSpec((S5_SLABS, rows, LANES), lambda i: (2 * RG_SLABS // S5_SLABS, i, 0)),
                  pl.BlockSpec((B, tT, D_MODEL), lambda i: (0, i, 0)),
                  pl.BlockSpec((B, N_MOD, D_MODEL), lambda i: (0, 0, 0)),
                  pl.BlockSpec((D_MODEL, D_MODEL), const, pipeline_mode=pl.Buffered(1)),
                  pl.BlockSpec((D_S5, D_S5), const),
                  pl.BlockSpec((1, D_S5), const),
                  pl.BlockSpec((1, D_S5), const),
                  pl.BlockSpec((1, D_RG), const),
                  pl.BlockSpec((1, D_S5), const)],
        out_specs=pl.BlockSpec((B, tT, D_MODEL), lambda i: (0, i, 0)),
        out_shape=jax.ShapeDtypeStruct((B, T, D_MODEL), F32),
        compiler_params=_cparams(56), name=name)(
            y_rg, y_ssm, proj, x, mod, w_out_bf16, w_glu_bf16, b_glu, s5_d, g_rg, g_s5)


def _premoe_kernel(x_ref, mod_ref, g_ref, wr_ref, h_ref, a_ref):
    h = _rms(x_ref[...]) * (g_ref[...] * (1.0 + mod_ref[0, 4:5, :])) + mod_ref[0, 3:4, :]
    hhi = h.astype(BF16)
    h_ref[...] = hhi
    hlo = (h - hhi.astype(F32)).astype(BF16)
    w = wr_ref[...]
    whi = w.astype(BF16)
    wlo = (w - whi.astype(F32)).astype(BF16)
    nt = lambda a, b: lax.dot_general(a, b, (((1,), (1,)), ((), ())), preferred_element_type=F32)
    lg = nt(whi, hhi) + nt(wlo, hhi) + nt(whi, hlo)
    e = jnp.exp(lg - jnp.max(lg, axis=0, keepdims=True))
    a_ref[...] = e / jnp.sum(e, axis=0, keepdims=True)


def _premoe(x1, mod, g2, w_router_t, T, name):
    n = x1.shape[0]
    rows = 512
    per_req = T // rows if mod.shape[0] > 1 else n
    return pl.pallas_call(
        _premoe_kernel, grid=(n // rows,),
        in_specs=[pl.BlockSpec((rows, D_MODEL), lambda i: (i, 0)),
                  pl.BlockSpec((1, N_MOD, D_MODEL), lambda i: (i // per_req, 0, 0)),
                  pl.BlockSpec((1, D_MODEL), lambda i: (0, 0)),
                  pl.BlockSpec((N_EXPERTS, D_MODEL), lambda i: (0, 0))],
        out_specs=[pl.BlockSpec((rows, D_MODEL), lambda i: (i, 0)),
                   pl.BlockSpec((N_EXPERTS, rows), lambda i: (0, i))],
        out_shape=[jax.ShapeDtypeStruct((n, D_MODEL), BF16),
                   jax.ShapeDtypeStruct((N_EXPERTS, n), F32)],
        compiler_params=_cparams(40), name=name)(x1, mod, g2, w_router_t)


def _topk_kernel(a_ref, s_ref, st_ref, c_ref, *, cap, T, nreq):
    capf = float(cap)

    def count_ge(r, v):
        return jnp.sum((a_ref[:, r * T:(r + 1) * T] >= v).astype(F32), axis=1, keepdims=True)

    def bit_step(i, thrs):
        bit = jnp.int32(1) << (30 - i)
        out = []
        for r in range(nreq):
            cand = thrs[r] | bit
            ok = count_ge(r, pltpu.bitcast(cand, F32)) >= capf
            out.append(jnp.where(ok, cand, thrs[r]))
        return tuple(out)

    zero = jnp.zeros((N_EXPERTS, 1), jnp.int32)
    thrs = lax.fori_loop(0, 31, bit_step, (zero,) * nreq)

    def mid_step(i, c):
        out = []
        for r in range(nreq):
            lo, hi = c[2 * r], c[2 * r + 1]
            mid = 0.5 * (lo + hi)
            ok = count_ge(r, mid) >= capf
            out += [jnp.where(ok, mid, lo), jnp.where(ok, hi, mid)]
        return tuple(out)

    init = []
    for r in range(nreq):
        init += [pltpu.bitcast(thrs[r], F32), pltpu.bitcast(thrs[r] + 1, F32)]
    bounds = lax.fori_loop(0, 24, mid_step, tuple(init))

    u = (lax.broadcasted_iota(jnp.int32, (T, T), 0)
         <= lax.broadcasted_iota(jnp.int32, (T, T), 1)).astype(F32).astype(BF16)
    nblk = T // TOKEN_BLOCK
    cl = lax.broadcasted_iota(jnp.int32, c_ref.shape, 1)
    cnt = jnp.zeros(c_ref.shape, F32)
    for r in range(nreq):
        aff = a_ref[:, r * T:(r + 1) * T]
        lo, hi = bounds[2 * r], bounds[2 * r + 1]
        gt = aff >= hi
        eq = (aff >= lo) & jnp.logical_not(gt)
        need = capf - jnp.sum(gt.astype(F32), axis=1, keepdims=True)
        pe = _dot(eq.astype(F32).astype(BF16), u)
        sel = gt | (eq & (pe <= need))
        ps = _dot(sel.astype(F32).astype(BF16), u)
        slot = jnp.where(sel, ps - 1.0, -1.0)
        s_ref[:, r * T:(r + 1) * T] = slot
        st_ref[r * T:(r + 1) * T, :] = slot.T
        for k in range(nblk):
            end = (k + 1) * TOKEN_BLOCK
            cnt = jnp.where(cl == r * nblk + k, ps[:, end - 1:end], cnt)
    c_ref[...] = cnt


def _topk(aff_t, T, name):
    n = aff_t.shape[1]
    nreq = n // T
    ncnt = nreq * (T // TOKEN_BLOCK)
    kern = functools.partial(_topk_kernel, cap=T // 8, T=T, nreq=nreq)
    return pl.pallas_call(
        kern, grid=(1,),
        in_specs=[pl.BlockSpec((N_EXPERTS, n), lambda i: (0, 0))],
        out_specs=[pl.BlockSpec((N_EXPERTS, n), lambda i: (0, 0)),
                   pl.BlockSpec((n, N_EXPERTS), lambda i: (0, 0)),
                   pl.BlockSpec((N_EXPERTS, ncnt), lambda i: (0, 0))],
        out_shape=[jax.ShapeDtypeStruct((N_EXPERTS, n), F32),
                   jax.ShapeDtypeStruct((n, N_EXPERTS), F32),
                   jax.ShapeDtypeStruct((N_EXPERTS, ncnt), F32)],
        compiler_params=_cparams(48), name=name)(aff_t)


def _window_starts(start_ref, nblk):
    base = (pl.program_id(0) * nblk + pl.program_id(1)) * N_EXPERTS
    return [pl.multiple_of(start_ref[base + e], 16) for e in range(N_EXPERTS)]


def _gather_kernel(start_ref, ok_ref, h_ref, s_ref, a_ref, xg_ref, gt_ref, *, cap, W, nblk):
    @pl.when(pl.program_id(1) == 0)
    def _():
        xg_ref[...] = jnp.zeros_like(xg_ref)
        gt_ref[...] = jnp.zeros_like(gt_ref)

    h = h_ref[...]
    nch = 512

    def step(w, starts):
        j = lax.broadcasted_iota(jnp.int32, (w, TOKEN_BLOCK), 0).astype(F32)
        ohs = []
        for e in range(N_EXPERTS):
            rel = s_ref[e:e + 1, :]
            if starts is not None:
                rel = rel - starts[e].astype(F32)
            oh = rel == j
            rows = pl.ds(starts[e], w) if starts is not None else slice(0, w)
            gt_ref[e, rows, :] += jnp.sum(jnp.where(oh, a_ref[e:e + 1, :], 0.0), axis=1, keepdims=True)
            ohs.append(oh.astype(F32).astype(BF16))
        ohm = jnp.concatenate(ohs, axis=0)
        for c in range(0, D_MODEL, nch):
            res = _dot(ohm, h[:, c:c + nch])
            for e in range(N_EXPERTS):
                rows = pl.ds(starts[e], w) if starts is not None else slice(0, w)
                xg_ref[e, rows, c:c + nch] += res[e * w:(e + 1) * w].astype(BF16)

    if W == cap:
        step(cap, None)
    else:
        ok = ok_ref[pl.program_id(0) * nblk + pl.program_id(1)] > 0

        @pl.when(ok)
        def _():
            step(W, _window_starts(start_ref, nblk))

        @pl.when(jnp.logical_not(ok))
        def _():
            step(cap, None)


def _gather(h2, slot, aff_t, starts, ok, T, name):
    n = h2.shape[0]
    cap = T // 8
    nreq = n // T
    nblk = T // TOKEN_BLOCK
    kern = functools.partial(_gather_kernel, cap=cap, W=min(SLOT_WINDOW, cap), nblk=nblk)
    tok = lambda b, k, *_: (0, b * nblk + k)
    req = lambda b, k, *_: (0, b, 0)
    return pl.pallas_call(
        kern,
        grid_spec=pltpu.PrefetchScalarGridSpec(
            num_scalar_prefetch=2, grid=(nreq, nblk),
            in_specs=[pl.BlockSpec((TOKEN_BLOCK, D_MODEL), lambda b, k, *_: (b * nblk + k, 0)),
                      pl.BlockSpec((N_EXPERTS, TOKEN_BLOCK), tok),
                      pl.BlockSpec((N_EXPERTS, TOKEN_BLOCK), tok)],
            out_specs=[pl.BlockSpec((N_EXPERTS, cap, D_MODEL), req),
                       pl.BlockSpec((N_EXPERTS, cap, 1), req)]),
        out_shape=[jax.ShapeDtypeStruct((N_EXPERTS, nreq * cap, D_MODEL), BF16),
                   jax.ShapeDtypeStruct((N_EXPERTS, nreq * cap, 1), F32)],
        compiler_params=_cparams(56), name=name)(starts, ok, h2, slot, aff_t)


def _ffn_kernel(xc_ref, xl_ref, gc_ref, gl_ref, w1_ref, w3_ref, w2_ref, yc_ref, yl_ref, xbuf, acc, *, rc, nf, rsub):
    f = pl.program_id(2)

    @pl.when(f == 0)
    def _():
        xbuf[0:rc, :] = xc_ref[0]
        xbuf[rc:, :] = xl_ref[0]

    @pl.when(f == 0)
    def _():
        acc[...] = jnp.zeros_like(acc)

    w1 = w1_ref[0].astype(BF16)
    w3 = w3_ref[0].astype(BF16)
    w2 = w2_ref[0].astype(BF16)
    rows = xbuf.shape[0]
    for r0 in range(0, rows, rsub):
        x = xbuf[r0:r0 + rsub, :]
        hid = jax.nn.silu(_dot(x, w1)) * _dot(x, w3)
        acc[r0:r0 + rsub, :] += _dot(hid.astype(BF16), w2)

    @pl.when(f == nf - 1)
    def _():
        yc_ref[0] = (acc[0:rc, :] * gc_ref[0]).astype(BF16)
        yl_ref[0] = (acc[rc:, :] * gl_ref[0]).astype(BF16)


def _ffn(xg_c, xg_l, gt_c, gt_l, w1, w3, w2):
    rc = xg_c.shape[1] // 2
    rl = xg_l.shape[1] // 2
    bf = 256
    nf = D_MODEL // bf
    kern = functools.partial(_ffn_kernel, rc=rc, nf=nf, rsub=rc + rl)
    rowsp = lambda r, w: pl.BlockSpec((1, r, w), lambda e, h, f: (e, h, 0))
    return pl.pallas_call(
        kern, grid=(N_EXPERTS, 2, nf),
        in_specs=[rowsp(rc, D_MODEL), rowsp(rl, D_MODEL), rowsp(rc, 1), rowsp(rl, 1),
                  pl.BlockSpec((1, D_MODEL, bf), lambda e, h, f: (e, 0, f)),
                  pl.BlockSpec((1, D_MODEL, bf), lambda e, h, f: (e, 0, f)),
                  pl.BlockSpec((1, bf, D_MODEL), lambda e, h, f: (e, f, 0))],
        out_specs=[rowsp(rc, D_MODEL), rowsp(rl, D_MODEL)],
        out_shape=[jax.ShapeDtypeStruct(xg_c.shape, BF16), jax.ShapeDtypeStruct(xg_l.shape, BF16)],
        scratch_shapes=[pltpu.VMEM((rc + rl, D_MODEL), BF16), pltpu.VMEM((rc + rl, D_MODEL), F32)],
        compiler_params=_cparams(56), name="ffn")(xg_c, xg_l, gt_c, gt_l, w1, w3, w2)


def _expand_consts(w):
    lane = jnp.arange(N_EXPERTS * w)
    expand = (jnp.arange(N_EXPERTS)[:, None] == (lane // w)[None, :]).astype(BF16)
    return expand, (lane % w).astype(F32)[None, :]


def _combine_kernel(start_ref, ok_ref, x_ref, y_ref, st_ref, sv_ref, mod_ref, fg_ref, exw_ref, jlw_ref,
                    exd_ref, jld_ref, o_ref, stage, *, cap, W, nblk):
    st = st_ref[...]

    def finish(moe):
        x2 = x_ref[...] + mod_ref[0, 5:6, :] * moe
        o_ref[...] = _rms(x2) * fg_ref[...]

    def dense():
        rep = _dot(st.astype(BF16), exd_ref[...])
        oh = (rep == jld_ref[...]).astype(F32).astype(BF16)
        finish(_dot(oh, y_ref[...].reshape(N_EXPERTS * cap, D_MODEL)))

    if W == cap:
        dense()
        return
    ok = ok_ref[pl.program_id(0) * nblk + pl.program_id(1)] > 0

    @pl.when(ok)
    def _():
        starts = _window_starts(start_ref, nblk)
        for e in range(N_EXPERTS):
            stage[e * W:(e + 1) * W, :] = y_ref[e, pl.ds(starts[e], W), :]
        rep = _dot((st - sv_ref[0]).astype(BF16), exw_ref[...])
        oh = (rep == jlw_ref[...]).astype(F32).astype(BF16)
        finish(_dot(oh, stage[...]))

    @pl.when(jnp.logical_not(ok))
    def _():
        dense()


def _combine(x1, ybuf, slot_t, starts, ok, starts_f, mod, final_g, T, name):
    n = x1.shape[0]
    cap = T // 8
    nblk = T // TOKEN_BLOCK
    W = min(SLOT_WINDOW, cap)
    multi = mod.shape[0] > 1
    exw, jlw = _expand_consts(W)
    exd, jld = _expand_consts(cap)
    kern = functools.partial(_combine_kernel, cap=cap, W=W, nblk=nblk)
    tok = lambda b, k, *_: (b * nblk + k, 0)
    const = lambda b, k, *_: (0, 0)
    return pl.pallas_call(
        kern,
        grid_spec=pltpu.PrefetchScalarGridSpec(
            num_scalar_prefetch=2, grid=(n // T, nblk),
            in_specs=[pl.BlockSpec((TOKEN_BLOCK, D_MODEL), tok),
                      pl.BlockSpec((N_EXPERTS, cap, D_MODEL), lambda b, k, *_: (0, b, 0)),
                      pl.BlockSpec((TOKEN_BLOCK, N_EXPERTS), tok),
                      pl.BlockSpec((1, 1, N_EXPERTS), lambda b, k, *_: (b * nblk + k, 0, 0)),
                      pl.BlockSpec((1, N_MOD, D_MODEL), lambda b, k, *_: (b if multi else 0, 0, 0)),
                      pl.BlockSpec((1, D_MODEL), const),
                      pl.BlockSpec(exw.shape, const), pl.BlockSpec(jlw.shape, const),
                      pl.BlockSpec(exd.shape, const), pl.BlockSpec(jld.shape, const)],
            out_specs=pl.BlockSpec((TOKEN_BLOCK, D_MODEL), tok),
            scratch_shapes=[pltpu.VMEM((N_EXPERTS * W, D_MODEL), BF16)]),
        out_shape=jax.ShapeDtypeStruct((n, D_MODEL), F32),
        compiler_params=_cparams(56), name=name)(starts, ok, x1, ybuf, slot_t, starts_f, mod, final_g,
                                                 exw, jlw, exd, jld)


def _window_tables(cnt, T):
    cap = T // 8
    nblk = T // TOKEN_BLOCK
    nreq = cnt.shape[1] // nblk
    W = min(SLOT_WINDOW, cap)
    end = cnt.astype(jnp.int32).reshape(N_EXPERTS, nreq, nblk)
    first = jnp.concatenate([jnp.zeros_like(end[..., :1]), end[..., :-1]], axis=-1)
    start = jnp.minimum((first // 16) * 16, cap - W)
    ok = jnp.all(end - start <= W, axis=0).astype(jnp.int32).reshape(-1)
    start = start.transpose(1, 2, 0)
    return start.reshape(-1), ok, start.astype(F32).reshape(nreq * nblk, 1, N_EXPERTS)


def _blockdiag_pairs(w):
    w = w.reshape(2, RG_SLABS, 2, 64, 64)
    bd = jnp.einsum('dshij,hk->dshikj', w, jnp.eye(2, dtype=w.dtype))
    return bd.reshape(2, RG_SLABS, LANES, LANES).transpose(1, 0, 2, 3)


def kernel(x_prompt, x_sample, state_rglru, state_s5_re, state_s5_im, c, c_ctx, norm1_g, norm2_g, w_ada, b_ada, w_in, rg_conv_w, rg_conv_b, rg_wa, rg_ba, rg_wx, rg_bx, rg_lam, s5_a_re, s5_a_im, s5_log_dt, s5_b_re, s5_b_im, s5_c_re, s5_c_im, s5_d, s5_w_glu, s5_b_glu, gnorm_rg, gnorm_s5, w_out, w_router, w1, w3, w2, final_g):
    l = 0
    Bp, Tp, _ = x_prompt.shape
    Bs, Ts, _ = x_sample.shape

    c16 = jnp.concatenate([c, c_ctx[None], jnp.zeros((16 - Bs - 1, D_MODEL), F32)], axis=0)
    mod = _adaln(c16, w_ada[l], b_ada[l]).reshape(16, N_MOD, D_MODEL)
    mod_lat = mod[:Bs]
    mod_ctx1 = mod[Bs:Bs + 1]
    mod_ctx = jnp.broadcast_to(mod_ctx1, (Bp, N_MOD, D_MODEL))

    w_in_b = w_in[l].astype(BF16)
    w_out_b = w_out[l].astype(BF16)
    w_glu_b = s5_w_glu[l].astype(BF16)
    row = lambda v: v.reshape(1, -1)

    wg = (0.5 * jnp.concatenate([_blockdiag_pairs(rg_wa[l]), _blockdiag_pairs(rg_wx[l])], axis=-1)).astype(BF16)
    bg = 0.5 * jnp.concatenate([rg_ba[l].reshape(2, RG_SLABS, LANES), rg_bx[l].reshape(2, RG_SLABS, LANES)],
                               axis=-1).transpose(1, 0, 2).reshape(RG_SLABS, 2, 1, 2 * LANES)
    kmat, pmat, qmat, lam16 = _s5prep(s5_a_re[l], s5_a_im[l], s5_log_dt[l], s5_b_re[l], s5_b_im[l],
                                       s5_c_re[l], s5_c_im[l])

    def mixer(x, modp, rg_h0, s5_h0, lat, tag):
        B, T, _ = x.shape
        R = T * B
        proj = _inproj(x, modp, row(norm1_g[l]), w_in_b, "inproj_" + tag)
        y_rg, fin_rg = _rglru(proj, rg_conv_w[l], row(rg_conv_b[l]), wg, bg, rg_lam[l], rg_h0, B, T,
                              "rglru_" + tag)
        if lat:
            A, Bq = T // GRID_W // CHUNK, GRID_W * B
        else:
            A, Bq = T // CHUNK, B
        y_ssm, fin_s5 = _s5(proj.reshape(N_SLAB, A, CHUNK, Bq, LANES), kmat, pmat, qmat, lam16, s5_h0,
                            A, Bq, B, lat, "s5_" + tag)
        x1 = _post(y_rg, y_ssm.reshape(S5_SLABS, R, LANES), proj, x, modp, w_out_b, w_glu_b,
                   row(s5_b_glu[l]), row(s5_d[l]), row(gnorm_rg[l]), row(gnorm_s5[l]), "post_" + tag)
        return x1.reshape(R, D_MODEL), fin_rg, fin_s5

    z_rg = jnp.zeros((2, Bp, D_RG), F32)
    z_s5 = jnp.zeros((S5_GROUPS, Bp, 256), F32)
    x1_c, fin_rg, fin_s5 = mixer(x_prompt, mod_ctx, z_rg, z_s5, False, "ctx")
    rg_h0 = state_rglru[:, l].transpose(1, 0, 2)
    s5_h0 = jnp.stack([state_s5_re[:, l, 0], state_s5_re[:, l, 1], state_s5_im[:, l, 0], state_s5_im[:, l, 1]],
                      axis=0)
    s5_h0 = s5_h0.transpose(2, 1, 0, 3).reshape(S5_GROUPS, Bs, 256)
    x1_s, _, _ = mixer(x_sample, mod_lat, rg_h0, s5_h0, True, "lat")

    w_router_t = w_router[l].T
    g2 = row(norm2_g[l])

    def route(x1, modp, T, tag):
        h2, aff_t = _premoe(x1, modp, g2, w_router_t, T, "premoe_" + tag)
        slot, slot_t, cnt = _topk(aff_t, T, "topk_" + tag)
        win = _window_tables(cnt, T)
        xg, gates = _gather(h2, slot, aff_t, win[0], win[1], T, "gather_" + tag)
        return xg, gates, slot_t, win

    xg_c, gt_c, st_c, win_c = route(x1_c, mod_ctx1, Tp, "ctx")
    xg_s, gt_s, st_s, win_s = route(x1_s, mod_lat, Ts, "lat")
    y_c, y_s = _ffn(xg_c, xg_s, gt_c, gt_s, w1[l], w3[l], w2[l])
    fg = row(final_g)
    y_prompt = _combine(x1_c, y_c, st_c, *win_c, mod_ctx1, fg, Tp, "combine_ctx").reshape(Bp, Tp, D_MODEL)
    y_sample = _combine(x1_s, y_s, st_s, *win_s, mod_lat, fg, Ts, "combine_lat").reshape(Bs, Ts, D_MODEL)

    new_rg = fin_rg.transpose(1, 0, 2)[:, None]
    f4 = fin_s5.reshape(S5_GROUPS, Bp, 4, 64).transpose(1, 2, 0, 3)
    new_re = f4[:, None, 0:2]
    new_im = f4[:, None, 2:4]
    return (y_prompt, y_sample, new_rg, new_re, new_im)
```

```python
import functools

import jax
import jax.numpy as jnp
from jax import lax
from jax.experimental import pallas as pl
from jax.experimental.pallas import tpu as pltpu

F32 = jnp.float32
BF16 = jnp.bfloat16
HIGHEST = lax.Precision.HIGHEST

LANES = 128
D_MODEL = 2048
D_RG = 1536
D_S5 = 512
D_IN = 2 * D_RG + D_S5
N_SLAB = D_IN // LANES
RG_SLABS = D_RG // LANES
S5_SLABS = D_S5 // LANES
S5_GROUPS = 32
GROUPS_PER_SLAB = 8
CHUNK = 16
N_EXPERTS = 16
N_MOD = 6
EPS = 1e-6
RG_C = 8.0
LOG2_E = 1.4426950408889634
GRID_W = 64
TOKEN_BLOCK = 256
SLOT_WINDOW = 64
MIB = 1024 * 1024


def _cparams(vmem_mib):
    return pltpu.CompilerParams(vmem_limit_bytes=vmem_mib * MIB)


def _dot(a, b):
    return jnp.dot(a, b, preferred_element_type=F32)


def _rms(x):
    return x * lax.rsqrt(jnp.mean(x * x, axis=-1, keepdims=True) + EPS)


def _adaln_kernel(c_ref, w_ref, b_ref, o_ref):
    s = jax.nn.silu(c_ref[...]).astype(BF16)
    o_ref[...] = _dot(s, w_ref[...].astype(BF16)) + b_ref[...]


def _adaln(c16, w_ada, b_ada):
    n = w_ada.shape[1]
    bn = 1024
    return pl.pallas_call(
        _adaln_kernel, grid=(n // bn,),
        in_specs=[pl.BlockSpec((16, D_MODEL), lambda j: (0, 0)),
                  pl.BlockSpec((D_MODEL, bn), lambda j: (0, j)),
                  pl.BlockSpec((1, bn), lambda j: (0, j))],
        out_specs=pl.BlockSpec((16, bn), lambda j: (0, j)),
        out_shape=jax.ShapeDtypeStruct((16, n), F32),
        compiler_params=_cparams(40), name="adaln")(c16, w_ada, b_ada.reshape(1, n))


def _sublane_transpose8(vs, sub):
    vs = list(vs)
    for d in (4, 2, 1):
        hi = (sub & d) != 0
        new = list(vs)
        for i in range(8):
            if i & d:
                continue
            new[i] = jnp.where(hi, pltpu.roll(vs[i + d], d, axis=1), vs[i])
            new[i + d] = jnp.where(hi, vs[i + d], pltpu.roll(vs[i], 8 - d, axis=1))
        vs = new
    return vs


def _rows_bt_to_tb(p, B, tT):
    sub = lax.broadcasted_iota(jnp.int32, (tT // 8, 8, LANES), 1)
    groups = []
    for g in range(B // 8):
        vs = [p[(8 * g + i) * tT:(8 * g + i + 1) * tT].reshape(tT // 8, 8, LANES) for i in range(8)]
        groups.append(_sublane_transpose8(vs, sub))
    out = jnp.stack([jnp.stack([grp[a] for grp in groups], axis=1) for a in range(8)], axis=1)
    return out.reshape(tT * B, LANES)


def _rows_tb_to_bt(y, B, tT):
    ng = B // 8
    y4 = y.reshape(tT // 8, 8 * ng, 8, LANES)
    sub = lax.broadcasted_iota(jnp.int32, (tT // 8, 8, LANES), 1)
    out = []
    for g in range(ng):
        out += _sublane_transpose8([y4[:, a * ng + g] for a in range(8)], sub)
    return jnp.concatenate([v.reshape(tT, LANES) for v in out], axis=0)


def _inproj_kernel(x_ref, mod_ref, g_ref, w_ref, o_ref, *, B, tT, nch):
    x = x_ref[...]
    shift = mod_ref[:, 0:1, :]
    scale = mod_ref[:, 1:2, :]
    h = _rms(x) * (g_ref[...] * (1.0 + scale)) + shift
    hb = h.reshape(B * tT, D_MODEL).astype(BF16)
    for c0 in range(0, D_IN, nch):
        p = _dot(hb, w_ref[:, c0:c0 + nch])
        if D_RG <= c0 < 2 * D_RG:
            p = jax.nn.gelu(p)
        for j in range(nch // LANES):
            o_ref[c0 // LANES + j] = _rows_bt_to_tb(p[:, j * LANES:(j + 1) * LANES], B, tT)


def _inproj(x, mod, g1, w_in_bf16, name):
    B, T, _ = x.shape
    tT = 512 // B
    kern = functools.partial(_inproj_kernel, B=B, tT=tT, nch=512)
    return pl.pallas_call(
        kern, grid=(T // tT,),
        in_specs=[pl.BlockSpec((B, tT, D_MODEL), lambda i: (0, i, 0)),
                  pl.BlockSpec((B, N_MOD, D_MODEL), lambda i: (0, 0, 0)),
                  pl.BlockSpec((1, D_MODEL), lambda i: (0, 0)),
                  pl.BlockSpec((D_MODEL, D_IN), lambda i: (0, 0), pipeline_mode=pl.Buffered(1))],
        out_specs=pl.BlockSpec((N_SLAB, tT * B, LANES), lambda i: (0, i, 0)),
        out_shape=jax.ShapeDtypeStruct((N_SLAB, T * B, LANES), F32),
        compiler_params=_cparams(56), name=name)(x, mod, g1, w_in_bf16)


def _rglru_kernel(xb_ref, gt_ref, cw_ref, cb_ref, wg_ref, bg_ref, lam_ref, h0_ref, y_ref, fin_ref,
                  *, B, T, CT):
    RC = CT * B
    nc = T // CT
    cw = [cw_ref[k:k + 1, :] for k in range(4)]
    cb = cb_ref[...]
    hl = [(-0.5 * RG_C * LOG2_E) * jax.nn.softplus(-lam_ref[d:d + 1, :]) for d in range(2)]

    def chunk(n, d):
        t0 = n * CT
        row0 = pl.multiple_of(t0 * B, RC)
        cur = xb_ref[0, pl.ds(row0, RC), :]
        p0 = pl.multiple_of(jnp.maximum(t0 - 1, 0) * B, B)
        prev = xb_ref[0, pl.ds(p0, B), :] * jnp.where(t0 > 0, 1.0, 0.0)
        n0 = pl.multiple_of(jnp.minimum(t0 + CT, T - 2) * B, B)
        nxt = xb_ref[0, pl.ds(n0, 2 * B), :] * jnp.where(t0 + CT < T, 1.0, 0.0)
        ext = jnp.concatenate([prev, cur, nxt], axis=0)
        xc = (cw[0] * ext[0:RC] + cw[1] * ext[B:B + RC] + cw[2] * ext[2 * B:2 * B + RC]
              + cw[3] * ext[3 * B:3 * B + RC] + cb)
        th = jnp.tanh(_dot(xc.astype(BF16), wg_ref[0, d]) + bg_ref[0, d])
        i = 0.5 + 0.5 * th[:, LANES:]
        a = jnp.exp2(hl[d] + hl[d] * th[:, :LANES])
        om = 1.0 - a * a
        bb = jnp.where(om > 0.0, om * lax.rsqrt(om), 0.0) * (i * xc)
        return a, bb, row0

    def make_body(finish):
        def body(n, carry):
            hf, hb = carry
            af, bf, rf = chunk(n, 0)
            ab, bk, rb = chunk(nc - 1 - n, 1)
            if finish:
                glf = gt_ref[0, pl.ds(rf, RC), :]
                glb = gt_ref[0, pl.ds(rb, RC), :]
                of = y_ref[0, pl.ds(rf, RC), :]
                ob = y_ref[0, pl.ds(rb, RC), :]
            for t in range(CT):
                u = CT - 1 - t
                hf = af[t * B:(t + 1) * B] * hf + bf[t * B:(t + 1) * B]
                hb = ab[u * B:(u + 1) * B] * hb + bk[u * B:(u + 1) * B]
                rows_f = pl.ds(rf + t * B, B)
                rows_b = pl.ds(rb + u * B, B)
                if finish:
                    y_ref[0, rows_f, :] = (of[t * B:(t + 1) * B] + hf) * glf[t * B:(t + 1) * B]
                    y_ref[0, rows_b, :] = (ob[u * B:(u + 1) * B] + hb) * glb[u * B:(u + 1) * B]
                else:
                    y_ref[0, rows_f, :] = hf
                    y_ref[0, rows_b, :] = hb
            return hf, hb
        return body

    carry = lax.fori_loop(0, nc // 2, make_body(False), (h0_ref[0], h0_ref[1]))
    hf, hb = lax.fori_loop(nc // 2, nc, make_body(True), carry)
    fin_ref[0] = hf
    fin_ref[1] = hb


def _rglru(proj, cw, cb, wg, bg, lam, h0, B, T, name):
    R = T * B
    kern = functools.partial(_rglru_kernel, B=B, T=T, CT=512 // B)
    return pl.pallas_call(
        kern, grid=(RG_SLABS,),
        in_specs=[pl.BlockSpec((1, R, LANES), lambda s: (s, 0, 0)),
                  pl.BlockSpec((1, R, LANES), lambda s: (s + RG_SLABS, 0, 0)),
                  pl.BlockSpec((4, LANES), lambda s: (0, s)),
                  pl.BlockSpec((1, LANES), lambda s: (0, s)),
                  pl.BlockSpec((1, 2, LANES, 2 * LANES), lambda s: (s, 0, 0, 0)),
                  pl.BlockSpec((1, 2, 1, 2 * LANES), lambda s: (s, 0, 0, 0)),
                  pl.BlockSpec((2, LANES), lambda s: (0, s)),
                  pl.BlockSpec((2, B, LANES), lambda s: (0, 0, s))],
        out_specs=[pl.BlockSpec((1, R, LANES), lambda s: (s, 0, 0)),
                   pl.BlockSpec((2, B, LANES), lambda s: (0, 0, s))],
        out_shape=[jax.ShapeDtypeStruct((RG_SLABS, R, LANES), F32),
                   jax.ShapeDtypeStruct((2, B, D_RG), F32)],
        compiler_params=_cparams(58), name=name)(proj, proj, cw, cb, wg, bg, lam, h0)


def _cpow(pr, pi, n):
    wr = jnp.ones(n.shape, F32)
    wi = jnp.zeros(n.shape, F32)
    for k in range(4):
        bit = (n & (1 << k)) != 0
        wr, wi = jnp.where(bit, wr * pr - wi * pi, wr), jnp.where(bit, wr * pi + wi * pr, wi)
        pr, pi = pr * pr - pi * pi, 2.0 * pr * pi
    return wr, wi


def _s5prep_kernel(ar_ref, ai_ref, ldt_ref, arc_ref, aic_ref, ldtc_ref, btr_ref, bti_ref, ctr_ref, cti_ref,
                   k_ref, p_ref, q_ref, l_ref):
    ar = ar_ref[0]
    ai = ai_ref[0]
    dt = jnp.exp(ldt_ref[0])
    mag = jnp.exp(dt * ar)
    lr = mag * jnp.cos(dt * ai)
    li = mag * jnp.sin(dt * ai)
    den = ar * ar + ai * ai
    nre = lr - 1.0
    qre = (nre * ar + li * ai) / den
    qim = (li * ar - nre * ai) / den
    btr = btr_ref[0]
    bti = bti_ref[0]
    bbr = qre * btr - qim * bti
    bbi = qre * bti + qim * btr
    rowblk = lax.broadcasted_iota(jnp.int32, (256, LANES), 0) // CHUNK
    isf = lax.broadcasted_iota(jnp.int32, (256, LANES), 1) < 64

    def powers(n):
        return _cpow(lr, li, n)

    ctr = ctr_ref[0]
    cti = cti_ref[0]

    wr, wi = powers(rowblk)
    zr = wr * bbr - wi * bbi
    zi = wr * bbi + wi * bbr
    zero = jnp.zeros_like(zr)

    def mm(a, b):
        return jnp.dot(a, b, precision=HIGHEST, preferred_element_type=F32)

    cri = jnp.concatenate([ctr, cti], axis=0)
    mall_f = mm(jnp.concatenate([jnp.where(isf, zr, zero), jnp.where(isf, -zi, zero)], axis=1), cri)
    mall_b = mm(jnp.concatenate([jnp.where(isf, zero, zr), jnp.where(isf, zero, -zi)], axis=1), cri)
    blocks = [mall_f[(15 - q) * CHUNK:(16 - q) * CHUNK] for q in range(15)]
    blocks.append(mall_f[0:CHUNK] + mall_b[0:CHUNK])
    blocks += [mall_b[l * CHUNK:(l + 1) * CHUNK] for l in range(1, 16)]
    blocks.append(jnp.zeros((CHUNK, 256), F32))
    gw = jnp.concatenate(blocks, axis=0)
    colblk = lax.broadcasted_iota(jnp.int32, (256, 256), 1) // CHUNK
    kmat = jnp.zeros((256, 256), F32)
    for tp in range(CHUNK):
        kmat = jnp.where(colblk == tp, gw[(15 - tp) * CHUNK:(15 - tp) * CHUNK + 256], kmat)
    k_ref[0] = kmat.astype(BF16)

    wr, wi = powers(jnp.where(isf, 15 - rowblk, rowblk))
    p_ref[0] = jnp.concatenate([wr * bbr - wi * bbi, wr * bbi + wi * bbr], axis=1).astype(BF16)

    arc = arc_ref[0]
    aic = aic_ref[0]
    dtc = jnp.exp(ldtc_ref[0])
    laneblk = lax.broadcasted_iota(jnp.int32, (LANES, 256), 1) // CHUNK
    rowf = lax.broadcasted_iota(jnp.int32, (LANES, 256), 0) < 64
    magc = jnp.exp(dtc * arc)
    lrc = magc * jnp.cos(dtc * aic)
    lic = magc * jnp.sin(dtc * aic)
    vr, vi = _cpow(lrc, lic, jnp.where(rowf, laneblk, 15 - laneblk))
    wr = vr * lrc - vi * lic
    wi = vr * lic + vi * lrc
    q_ref[0] = jnp.concatenate([ctr * wr - cti * wi, -(ctr * wi + cti * wr)], axis=0).astype(BF16)

    pr, pi = lr, li
    for _ in range(4):
        pr, pi = pr * pr - pi * pi, 2.0 * pr * pi
    l_ref[0] = jnp.concatenate([pr, pi], axis=1)


def _s5prep(a_re, a_im, log_dt, b_re, b_im, c_re, c_im):
    G = S5_GROUPS

    def rows(x):
        return jnp.concatenate([x[0], x[1]], axis=-1).reshape(G, 1, LANES)

    ldt = jnp.broadcast_to(log_dt[:, :, None], (2, G, 64))
    ar, ai, ld = rows(a_re), rows(a_im), rows(ldt)
    arc, aic, ldc = (v.reshape(G, LANES, 1) for v in (ar, ai, ld))

    def btile(b):
        bt = jnp.tile(jnp.swapaxes(b, 2, 3), (1, 1, CHUNK, 1))
        return jnp.concatenate([bt[0], bt[1]], axis=-1)

    def ctile(c):
        ct = jnp.tile(jnp.swapaxes(c, 2, 3), (1, 1, 1, CHUNK))
        return jnp.concatenate([ct[0], ct[1]], axis=1)

    vec = pl.BlockSpec((1, 1, LANES), lambda g: (g, 0, 0))
    col = pl.BlockSpec((1, LANES, 1), lambda g: (g, 0, 0))
    bsp = pl.BlockSpec((1, 256, LANES), lambda g: (g, 0, 0))
    csp = pl.BlockSpec((1, LANES, 256), lambda g: (g, 0, 0))
    msp = pl.BlockSpec((1, 256, 256), lambda g: (g, 0, 0))
    return pl.pallas_call(
        _s5prep_kernel, grid=(G,),
        in_specs=[vec, vec, vec, col, col, col, bsp, bsp, csp, csp],
        out_specs=[msp, msp, msp, pl.BlockSpec((1, 1, 256), lambda g: (g, 0, 0))],
        out_shape=[jax.ShapeDtypeStruct((G, 256, 256), BF16)] * 3 + [jax.ShapeDtypeStruct((G, 1, 256), F32)],
        compiler_params=_cparams(32), name="s5prep")(
            ar, ai, ld, arc, aic, ldc, btile(b_re), btile(b_im), ctile(c_re), ctile(c_im))


def _seg_transpose8(vs, seg):
    vs = list(vs)
    for d in (4, 2, 1):
        hi = (seg & d) != 0
        new = list(vs)
        for i in range(8):
            if i & d:
                continue
            j = i + d
            new[i] = jnp.where(hi, pltpu.roll(vs[j], d * CHUNK, axis=1), vs[i])
            new[j] = jnp.where(hi, vs[j], pltpu.roll(vs[i], LANES - d * CHUNK, axis=1))
        vs = new
    return vs


def _s5_kernel(u_ref, k_ref, p_ref, q_ref, l_ref, h0_ref, y_ref, fin_ref, ebuf, ybuf, *, A, Bq, B, lat):
    RP = A * Bq
    nchunk = RP // B
    G = GROUPS_PER_SLAB
    nr = 256
    qn = min(Bq, nr)
    tiles = [(r0, [((r0 + i * qn) // Bq, (r0 + i * qn) % Bq) for i in range(nr // qn)])
             for r0 in range(0, RP, nr)]
    lane_blk = lax.broadcasted_iota(jnp.int32, (nr, LANES), 1) // CHUNK

    for r0, pieces in tiles:
        xs = [jnp.concatenate([u_ref[0, a, tau, q0:q0 + qn, :] for a, q0 in pieces], axis=0)
              for tau in range(CHUNK)]
        halves = [_seg_transpose8(xs[8 * h:8 * h + 8], lane_blk) for h in range(2)]
        for g in range(G):
            ug = jnp.concatenate([halves[0][g], halves[1][g]], axis=1).astype(BF16)
            ybuf[g, r0:r0 + nr, :] = _dot(ug, k_ref[g])
            ebuf[g, r0:r0 + nr, :] = _dot(ug, p_ref[g])

    def off(m):
        if lat:
            return ((m & 1) * GRID_W + (m >> 1)) * B
        return m * B

    fmask = (lax.broadcasted_iota(jnp.int32, (B, 2 * LANES), 1) % LANES) < 64
    fmask1 = lax.broadcasted_iota(jnp.int32, (B, LANES), 1) < 64
    lam = [l_ref[g] for g in range(G)]

    def body(i, carry):
        of = pl.multiple_of(off(i), B)
        ob = pl.multiple_of(off(nchunk - 1 - i), B)
        out = []
        for g in range(G):
            re, im = carry[2 * g], carry[2 * g + 1]
            ef = ebuf[g, pl.ds(of, B), :]
            eb = ebuf[g, pl.ds(ob, B), :]
            s = jnp.concatenate([re, im], axis=1)
            ebuf[g, pl.ds(of, B), :] = jnp.where(fmask, s, ef)
            ebuf[g, pl.ds(ob, B), :] = jnp.where(fmask, eb, s)
            lr = lam[g][:, :LANES]
            li = lam[g][:, LANES:]
            e_re = jnp.where(fmask1, ef[:, :LANES], eb[:, :LANES])
            e_im = jnp.where(fmask1, ef[:, LANES:], eb[:, LANES:])
            out.append(lr * re - li * im + e_re)
            out.append(lr * im + li * re + e_im)
        return tuple(out)

    init = []
    for g in range(G):
        h0 = h0_ref[g]
        init += [h0[:, :LANES], h0[:, LANES:]]
    fin = lax.fori_loop(0, nchunk, body, tuple(init))
    for g in range(G):
        fin_ref[g] = jnp.concatenate([fin[2 * g], fin[2 * g + 1]], axis=1)

    for g in range(G):
        ybuf[g] = ybuf[g] + _dot(ebuf[g].astype(BF16), q_ref[g])
    for r0, pieces in tiles:
        for h in range(2):
            yt = _seg_transpose8([ybuf[g, r0:r0 + nr, h * LANES:(h + 1) * LANES] for g in range(G)], lane_blk)
            for tp in range(8):
                for k, (a, q0) in enumerate(pieces):
                    y_ref[0, a, 8 * h + tp, q0:q0 + qn, :] = yt[tp][k * qn:(k + 1) * qn]


def _s5(proj5, kmat, pmat, qmat, lam16, h0, A, Bq, B, lat, name):
    RP = A * Bq
    G = GROUPS_PER_SLAB
    kern = functools.partial(_s5_kernel, A=A, Bq=Bq, B=B, lat=lat)
    msp = pl.BlockSpec((G, 256, 256), lambda s: (s, 0, 0))
    return pl.pallas_call(
        kern, grid=(S5_SLABS,),
        in_specs=[pl.BlockSpec((1, A, CHUNK, Bq, LANES), lambda s: (s + 2 * RG_SLABS, 0, 0, 0, 0)),
                  msp, msp, msp,
                  pl.BlockSpec((G, 1, 256), lambda s: (s, 0, 0)),
                  pl.BlockSpec((G, B, 256), lambda s: (s, 0, 0))],
        out_specs=[pl.BlockSpec((1, A, CHUNK, Bq, LANES), lambda s: (s, 0, 0, 0, 0)),
                   pl.BlockSpec((G, B, 256), lambda s: (s, 0, 0))],
        out_shape=[jax.ShapeDtypeStruct((S5_SLABS, A, CHUNK, Bq, LANES), F32),
                   jax.ShapeDtypeStruct((S5_GROUPS, B, 256), F32)],
        scratch_shapes=[pltpu.VMEM((G, RP, 256), F32), pltpu.VMEM((G, RP, 256), F32)],
        compiler_params=_cparams(58), name=name)(proj5, kmat, pmat, qmat, lam16, h0)


def _post_kernel(yr_ref, ys_ref, u_ref, x_ref, mod_ref, wo_ref, wg_ref, bgl_ref, d_ref, grg_ref, gs5_ref,
                 o_ref, *, B, tT):
    yr = jnp.concatenate([yr_ref[s] for s in range(RG_SLABS)], axis=1)
    nr = (_rms(yr) * grg_ref[...]).astype(BF16)
    ys = jnp.concatenate([ys_ref[s] for s in range(S5_SLABS)], axis=1)
    u = jnp.concatenate([u_ref[s] for s in range(S5_SLABS)], axis=1)
    ys = jax.nn.gelu(ys + d_ref[...] * u)
    ys = ys * jax.nn.sigmoid(_dot(ys.astype(BF16), wg_ref[...]) + bgl_ref[...])
    ns = (_rms(ys) * gs5_ref[...]).astype(BF16)
    nch = 512
    for c0 in range(0, D_MODEL, nch):
        cs = slice(c0, c0 + nch)
        y = _dot(nr, wo_ref[0:D_RG, cs]) + _dot(ns, wo_ref[D_RG:D_MODEL, cs])
        cols = [_rows_tb_to_bt(y[:, j * LANES:(j + 1) * LANES], B, tT) for j in range(nch // LANES)]
        for b in range(B):
            yb = jnp.concatenate([c[b * tT:(b + 1) * tT] for c in cols], axis=1)
            o_ref[b, :, cs] = x_ref[b, :, cs] + mod_ref[b, 2:3, cs] * yb


def _post(y_rg, y_ssm, proj, x, mod, w_out_bf16, w_glu_bf16, b_glu, s5_d, g_rg, g_s5, name):
    B, T, _ = x.shape
    tT = 512 // B
    rows = tT * B
    kern = functools.partial(_post_kernel, B=B, tT=tT)
    const = lambda i: (0, 0)
    return pl.pallas_call(
        kern, grid=(T // tT,),
        in_specs=[pl.BlockSpec((RG_SLABS, rows, LANES), lambda i: (0, i, 0)),
                  pl.BlockSpec((S5_SLABS, rows, LANES), lambda i: (0, i, 0)),
                  pl.BlockSpec((S5_SLABS, rows, LANES), lambda i: (2 * RG_SLABS // S5_SLABS, i, 0)),
                  pl.BlockSpec((B, tT, D_MODEL), lambda i: (0, i, 0)),
                  pl.BlockSpec((B, N_MOD, D_MODEL), lambda i: (0, 0, 0)),
                  pl.BlockSpec((D_MODEL, D_MODEL), const, pipeline_mode=pl.Buffered(1)),
                  pl.BlockSpec((D_S5, D_S5), const),
                  pl.BlockSpec((1, D_S5), const),
                  pl.BlockSpec((1, D_S5), const),
                  pl.BlockSpec((1, D_RG), const),
                  pl.BlockSpec((1, D_S5), const)],
        out_specs=pl.BlockSpec((B, tT, D_MODEL), lambda i: (0, i, 0)),
        out_shape=jax.ShapeDtypeStruct((B, T, D_MODEL), F32),
        compiler_params=_cparams(56), name=name)(
            y_rg, y_ssm, proj, x, mod, w_out_bf16, w_glu_bf16, b_glu, s5_d, g_rg, g_s5)


def _premoe_kernel(x_ref, mod_ref, g_ref, wr_ref, h_ref, a_ref):
    h = _rms(x_ref[...]) * (g_ref[...] * (1.0 + mod_ref[0, 4:5, :])) + mod_ref[0, 3:4, :]
    hhi = h.astype(BF16)
    h_ref[...] = hhi
    w = wr_ref[...]
    whi = w.astype(BF16)
    wlo = (w - whi.astype(F32)).astype(BF16)
    nt = lambda a, b: lax.dot_general(a, b, (((1,), (1,)), ((), ())), preferred_element_type=F32)
    lg = nt(whi, hhi) + nt(wlo, hhi)
    e = jnp.exp(lg - jnp.max(lg, axis=0, keepdims=True))
    a_ref[...] = e / jnp.sum(e, axis=0, keepdims=True)


def _premoe(x1, mod, g2, w_router_t, T, name):
    n = x1.shape[0]
    rows = 512
    per_req = T // rows if mod.shape[0] > 1 else n
    return pl.pallas_call(
        _premoe_kernel, grid=(n // rows,),
        in_specs=[pl.BlockSpec((rows, D_MODEL), lambda i: (i, 0)),
                  pl.BlockSpec((1, N_MOD, D_MODEL), lambda i: (i // per_req, 0, 0)),
                  pl.BlockSpec((1, D_MODEL), lambda i: (0, 0)),
                  pl.BlockSpec((N_EXPERTS, D_MODEL), lambda i: (0, 0))],
        out_specs=[pl.BlockSpec((rows, D_MODEL), lambda i: (i, 0)),
                   pl.BlockSpec((N_EXPERTS, rows), lambda i: (0, i))],
        out_shape=[jax.ShapeDtypeStruct((n, D_MODEL), BF16),
                   jax.ShapeDtypeStruct((N_EXPERTS, n), F32)],
        compiler_params=_cparams(40), name=name)(x1, mod, g2, w_router_t)


def _topk_kernel(a_ref, s_ref, st_ref, c_ref, *, cap, T, nreq):
    capf = float(cap)

    def count_ge(r, v):
        return jnp.sum((a_ref[:, r * T:(r + 1) * T] >= v).astype(F32), axis=1, keepdims=True)

    def bit_step(i, thrs):
        bit = jnp.int32(1) << (30 - i)
        out = []
        for r in range(nreq):
            cand = thrs[r] | bit
            ok = count_ge(r, pltpu.bitcast(cand, F32)) >= capf
            out.append(jnp.where(ok, cand, thrs[r]))
        return tuple(out)

    zero = jnp.zeros((N_EXPERTS, 1), jnp.int32)
    thrs = lax.fori_loop(0, 31, bit_step, (zero,) * nreq)

    def mid_step(i, c):
        out = []
        for r in range(nreq):
            lo, hi = c[2 * r], c[2 * r + 1]
            mid = 0.5 * (lo + hi)
            ok = count_ge(r, mid) >= capf
            out += [jnp.where(ok, mid, lo), jnp.where(ok, hi, mid)]
        return tuple(out)

    init = []
    for r in range(nreq):
        init += [pltpu.bitcast(thrs[r], F32), pltpu.bitcast(thrs[r] + 1, F32)]
    bounds = lax.fori_loop(0, 24, mid_step, tuple(init))

    u = (lax.broadcasted_iota(jnp.int32, (T, T), 0)
         <= lax.broadcasted_iota(jnp.int32, (T, T), 1)).astype(F32).astype(BF16)
    nblk = T // TOKEN_BLOCK
    cl = lax.broadcasted_iota(jnp.int32, c_ref.shape, 1)
    cnt = jnp.zeros(c_ref.shape, F32)
    for r in range(nreq):
        aff = a_ref[:, r * T:(r + 1) * T]
        lo, hi = bounds[2 * r], bounds[2 * r + 1]
        gt = aff >= hi
        eq = (aff >= lo) & jnp.logical_not(gt)
        need = capf - jnp.sum(gt.astype(F32), axis=1, keepdims=True)
        pe = _dot(eq.astype(F32).astype(BF16), u)
        sel = gt | (eq & (pe <= need))
        ps = _dot(sel.astype(F32).astype(BF16), u)
        slot = jnp.where(sel, ps - 1.0, -1.0)
        s_ref[:, r * T:(r + 1) * T] = slot
        st_ref[r * T:(r + 1) * T, :] = slot.T
        for k in range(nblk):
            end = (k + 1) * TOKEN_BLOCK
            cnt = jnp.where(cl == r * nblk + k, ps[:, end - 1:end], cnt)
    c_ref[...] = cnt


def _topk(aff_t, T, name):
    n = aff_t.shape[1]
    nreq = n // T
    ncnt = nreq * (T // TOKEN_BLOCK)
    kern = functools.partial(_topk_kernel, cap=T // 8, T=T, nreq=nreq)
    return pl.pallas_call(
        kern, grid=(1,),
        in_specs=[pl.BlockSpec((N_EXPERTS, n), lambda i: (0, 0))],
        out_specs=[pl.BlockSpec((N_EXPERTS, n), lambda i: (0, 0)),
                   pl.BlockSpec((n, N_EXPERTS), lambda i: (0, 0)),
                   pl.BlockSpec((N_EXPERTS, ncnt), lambda i: (0, 0))],
        out_shape=[jax.ShapeDtypeStruct((N_EXPERTS, n), F32),
                   jax.ShapeDtypeStruct((n, N_EXPERTS), F32),
                   jax.ShapeDtypeStruct((N_EXPERTS, ncnt), F32)],
        compiler_params=_cparams(48), name=name)(aff_t)


def _window_starts(start_ref, nblk):
    base = (pl.program_id(0) * nblk + pl.program_id(1)) * N_EXPERTS
    return [pl.multiple_of(start_ref[base + e], 16) for e in range(N_EXPERTS)]


def _gather_kernel(start_ref, ok_ref, h_ref, s_ref, a_ref, xg_ref, gt_ref, *, cap, W, nblk):
    @pl.when(pl.program_id(1) == 0)
    def _():
        xg_ref[...] = jnp.zeros_like(xg_ref)
        gt_ref[...] = jnp.zeros_like(gt_ref)

    h = h_ref[...]
    nch = 512

    def step(w, starts):
        j = lax.broadcasted_iota(jnp.int32, (w, TOKEN_BLOCK), 0).astype(F32)
        ohs = []
        for e in range(N_EXPERTS):
            rel = s_ref[e:e + 1, :]
            if starts is not None:
                rel = rel - starts[e].astype(F32)
            oh = rel == j
            rows = pl.ds(starts[e], w) if starts is not None else slice(0, w)
            gt_ref[e, rows, :] += jnp.sum(jnp.where(oh, a_ref[e:e + 1, :], 0.0), axis=1, keepdims=True)
            ohs.append(oh.astype(F32).astype(BF16))
        ohm = jnp.concatenate(ohs, axis=0)
        for c in range(0, D_MODEL, nch):
            res = _dot(ohm, h[:, c:c + nch])
            for e in range(N_EXPERTS):
                rows = pl.ds(starts[e], w) if starts is not None else slice(0, w)
                xg_ref[e, rows, c:c + nch] += res[e * w:(e + 1) * w].astype(BF16)

    if W == cap:
        step(cap, None)
    else:
        ok = ok_ref[pl.program_id(0) * nblk + pl.program_id(1)] > 0

        @pl.when(ok)
        def _():
            step(W, _window_starts(start_ref, nblk))

        @pl.when(jnp.logical_not(ok))
        def _():
            step(cap, None)


def _gather(h2, slot, aff_t, starts, ok, T, name):
    n = h2.shape[0]
    cap = T // 8
    nreq = n // T
    nblk = T // TOKEN_BLOCK
    kern = functools.partial(_gather_kernel, cap=cap, W=min(SLOT_WINDOW, cap), nblk=nblk)
    tok = lambda b, k, *_: (0, b * nblk + k)
    req = lambda b, k, *_: (0, b, 0)
    return pl.pallas_call(
        kern,
        grid_spec=pltpu.PrefetchScalarGridSpec(
            num_scalar_prefetch=2, grid=(nreq, nblk),
            in_specs=[pl.BlockSpec((TOKEN_BLOCK, D_MODEL), lambda b, k, *_: (b * nblk + k, 0)),
                      pl.BlockSpec((N_EXPERTS, TOKEN_BLOCK), tok),
                      pl.BlockSpec((N_EXPERTS, TOKEN_BLOCK), tok)],
            out_specs=[pl.BlockSpec((N_EXPERTS, cap, D_MODEL), req),
                       pl.BlockSpec((N_EXPERTS, cap, 1), req)]),
        out_shape=[jax.ShapeDtypeStruct((N_EXPERTS, nreq * cap, D_MODEL), BF16),
                   jax.ShapeDtypeStruct((N_EXPERTS, nreq * cap, 1), F32)],
        compiler_params=_cparams(56), name=name)(starts, ok, h2, slot, aff_t)


def _ffn_kernel(xc_ref, xl_ref, gc_ref, gl_ref, w1_ref, w3_ref, w2_ref, yc_ref, yl_ref, xbuf, acc, *, rc, nf, rsub):
    f = pl.program_id(2)

    @pl.when(f == 0)
    def _():
        xbuf[0:rc, :] = xc_ref[0]
        xbuf[rc:, :] = xl_ref[0]

    @pl.when(f == 0)
    def _():
        acc[...] = jnp.zeros_like(acc)

    w1 = w1_ref[0].astype(BF16)
    w3 = w3_ref[0].astype(BF16)
    w2 = w2_ref[0].astype(BF16)
    rows = xbuf.shape[0]
    for r0 in range(0, rows, rsub):
        x = xbuf[r0:r0 + rsub, :]
        hid = jax.nn.silu(_dot(x, w1)) * _dot(x, w3)
        acc[r0:r0 + rsub, :] += _dot(hid.astype(BF16), w2)

    @pl.when(f == nf - 1)
    def _():
        yc_ref[0] = (acc[0:rc, :] * gc_ref[0]).astype(BF16)
        yl_ref[0] = (acc[rc:, :] * gl_ref[0]).astype(BF16)


def _ffn(xg_c, xg_l, gt_c, gt_l, w1, w3, w2):
    rc = xg_c.shape[1] // 2
    rl = xg_l.shape[1] // 2
    bf = 256
    nf = D_MODEL // bf
    kern = functools.partial(_ffn_kernel, rc=rc, nf=nf, rsub=rc + rl)
    rowsp = lambda r, w: pl.BlockSpec((1, r, w), lambda e, h, f: (e, h, 0))
    return pl.pallas_call(
        kern, grid=(N_EXPERTS, 2, nf),
        in_specs=[rowsp(rc, D_MODEL), rowsp(rl, D_MODEL), rowsp(rc, 1), rowsp(rl, 1),
                  pl.BlockSpec((1, D_MODEL, bf), lambda e, h, f: (e, 0, f)),
                  pl.BlockSpec((1, D_MODEL, bf), lambda e, h, f: (e, 0, f)),
                  pl.BlockSpec((1, bf, D_MODEL), lambda e, h, f: (e, f, 0))],
        out_specs=[rowsp(rc, D_MODEL), rowsp(rl, D_MODEL)],
        out_shape=[jax.ShapeDtypeStruct(xg_c.shape, BF16), jax.ShapeDtypeStruct(xg_l.shape, BF16)],
        scratch_shapes=[pltpu.VMEM((rc + rl, D_MODEL), BF16), pltpu.VMEM((rc + rl, D_MODEL), F32)],
        compiler_params=_cparams(56), name="ffn")(xg_c, xg_l, gt_c, gt_l, w1, w3, w2)


def _expand_consts(w):
    lane = jnp.arange(N_EXPERTS * w)
    expand = (jnp.arange(N_EXPERTS)[:, None] == (lane // w)[None, :]).astype(BF16)
    return expand, (lane % w).astype(F32)[None, :]


def _combine_kernel(start_ref, ok_ref, x_ref, y_ref, st_ref, sv_ref, mod_ref, fg_ref, exw_ref, jlw_ref,
                    exd_ref, jld_ref, o_ref, stage, *, cap, W, nblk):
    st = st_ref[...]

    def finish(moe):
        x2 = x_ref[...] + mod_ref[0, 5:6, :] * moe
        o_ref[...] = _rms(x2) * fg_ref[...]

    def dense():
        rep = _dot(st.astype(BF16), exd_ref[...])
        oh = (rep == jld_ref[...]).astype(F32).astype(BF16)
        finish(_dot(oh, y_ref[...].reshape(N_EXPERTS * cap, D_MODEL)))

    if W == cap:
        dense()
        return
    ok = ok_ref[pl.program_id(0) * nblk + pl.program_id(1)] > 0

    @pl.when(ok)
    def _():
        starts = _window_starts(start_ref, nblk)
        for e in range(N_EXPERTS):
            stage[e * W:(e + 1) * W, :] = y_ref[e, pl.ds(starts[e], W), :]
        rep = _dot((st - sv_ref[0]).astype(BF16), exw_ref[...])
        oh = (rep == jlw_ref[...]).astype(F32).astype(BF16)
        finish(_dot(oh, stage[...]))

    @pl.when(jnp.logical_not(ok))
    def _():
        dense()


def _combine(x1, ybuf, slot_t, starts, ok, starts_f, mod, final_g, T, name):
    n = x1.shape[0]
    cap = T // 8
    nblk = T // TOKEN_BLOCK
    W = min(SLOT_WINDOW, cap)
    multi = mod.shape[0] > 1
    exw, jlw = _expand_consts(W)
    exd, jld = _expand_consts(cap)
    kern = functools.partial(_combine_kernel, cap=cap, W=W, nblk=nblk)
    tok = lambda b, k, *_: (b * nblk + k, 0)
    const = lambda b, k, *_: (0, 0)
    return pl.pallas_call(
        kern,
        grid_spec=pltpu.PrefetchScalarGridSpec(
            num_scalar_prefetch=2, grid=(n // T, nblk),
            in_specs=[pl.BlockSpec((TOKEN_BLOCK, D_MODEL), tok),
                      pl.BlockSpec((N_EXPERTS, cap, D_MODEL), lambda b, k, *_: (0, b, 0)),
                      pl.BlockSpec((TOKEN_BLOCK, N_EXPERTS), tok),
                      pl.BlockSpec((1, 1, N_EXPERTS), lambda b, k, *_: (b * nblk + k, 0, 0)),
                      pl.BlockSpec((1, N_MOD, D_MODEL), lambda b, k, *_: (b if multi else 0, 0, 0)),
                      pl.BlockSpec((1, D_MODEL), const),
                      pl.BlockSpec(exw.shape, const), pl.BlockSpec(jlw.shape, const),
                      pl.BlockSpec(exd.shape, const), pl.BlockSpec(jld.shape, const)],
            out_specs=pl.BlockSpec((TOKEN_BLOCK, D_MODEL), tok),
            scratch_shapes=[pltpu.VMEM((N_EXPERTS * W, D_MODEL), BF16)]),
        out_shape=jax.ShapeDtypeStruct((n, D_MODEL), F32),
        compiler_params=_cparams(56), name=name)(starts, ok, x1, ybuf, slot_t, starts_f, mod, final_g,
                                                 exw, jlw, exd, jld)


def _window_tables(cnt, T):
    cap = T // 8
    nblk = T // TOKEN_BLOCK
    nreq = cnt.shape[1] // nblk
    W = min(SLOT_WINDOW, cap)
    end = cnt.astype(jnp.int32).reshape(N_EXPERTS, nreq, nblk)
    first = jnp.concatenate([jnp.zeros_like(end[..., :1]), end[..., :-1]], axis=-1)
    start = jnp.minimum((first // 16) * 16, cap - W)
    ok = jnp.all(end - start <= W, axis=0).astype(jnp.int32).reshape(-1)
    start = start.transpose(1, 2, 0)
    return start.reshape(-1), ok, start.astype(F32).reshape(nreq * nblk, 1, N_EXPERTS)


def _blockdiag_pairs(w):
    w = w.reshape(2, RG_SLABS, 2, 64, 64)
    bd = jnp.einsum('dshij,hk->dshikj', w, jnp.eye(2, dtype=w.dtype))
    return bd.reshape(2, RG_SLABS, LANES, LANES).transpose(1, 0, 2, 3)


def kernel(x_prompt, x_sample, state_rglru, state_s5_re, state_s5_im, c, c_ctx, norm1_g, norm2_g, w_ada, b_ada, w_in, rg_conv_w, rg_conv_b, rg_wa, rg_ba, rg_wx, rg_bx, rg_lam, s5_a_re, s5_a_im, s5_log_dt, s5_b_re, s5_b_im, s5_c_re, s5_c_im, s5_d, s5_w_glu, s5_b_glu, gnorm_rg, gnorm_s5, w_out, w_router, w1, w3, w2, final_g):
    l = 0
    Bp, Tp, _ = x_prompt.shape
    Bs, Ts, _ = x_sample.shape

    c16 = jnp.concatenate([c, c_ctx[None], jnp.zeros((16 - Bs - 1, D_MODEL), F32)], axis=0)
    mod = _adaln(c16, w_ada[l], b_ada[l]).reshape(16, N_MOD, D_MODEL)
    mod_lat = mod[:Bs]
    mod_ctx1 = mod[Bs:Bs + 1]
    mod_ctx = jnp.broadcast_to(mod_ctx1, (Bp, N_MOD, D_MODEL))

    w_in_b = w_in[l].astype(BF16)
    w_out_b = w_out[l].astype(BF16)
    w_glu_b = s5_w_glu[l].astype(BF16)
    row = lambda v: v.reshape(1, -1)

    wg = (0.5 * jnp.concatenate([_blockdiag_pairs(rg_wa[l]), _blockdiag_pairs(rg_wx[l])], axis=-1)).astype(BF16)
    bg = 0.5 * jnp.concatenate([rg_ba[l].reshape(2, RG_SLABS, LANES), rg_bx[l].reshape(2, RG_SLABS, LANES)],
                               axis=-1).transpose(1, 0, 2).reshape(RG_SLABS, 2, 1, 2 * LANES)
    kmat, pmat, qmat, lam16 = _s5prep(s5_a_re[l], s5_a_im[l], s5_log_dt[l], s5_b_re[l], s5_b_im[l],
                                       s5_c_re[l], s5_c_im[l])

    def mixer(x, modp, rg_h0, s5_h0, lat, tag):
        B, T, _ = x.shape
        R = T * B
        proj = _inproj(x, modp, row(norm1_g[l]), w_in_b, "inproj_" + tag)
        y_rg, fin_rg = _rglru(proj, rg_conv_w[l], row(rg_conv_b[l]), wg, bg, rg_lam[l], rg_h0, B, T,
                              "rglru_" + tag)
        if lat:
            A, Bq = T // GRID_W // CHUNK, GRID_W * B
        else:
            A, Bq = T // CHUNK, B
        y_ssm, fin_s5 = _s5(proj.reshape(N_SLAB, A, CHUNK, Bq, LANES), kmat, pmat, qmat, lam16, s5_h0,
                            A, Bq, B, lat, "s5_" + tag)
        x1 = _post(y_rg, y_ssm.reshape(S5_SLABS, R, LANES), proj, x, modp, w_out_b, w_glu_b,
                   row(s5_b_glu[l]), row(s5_d[l]), row(gnorm_rg[l]), row(gnorm_s5[l]), "post_" + tag)
        return x1.reshape(R, D_MODEL), fin_rg, fin_s5

    z_rg = jnp.zeros((2, Bp, D_RG), F32)
    z_s5 = jnp.zeros((S5_GROUPS, Bp, 256), F32)
    x1_c, fin_rg, fin_s5 = mixer(x_prompt, mod_ctx, z_rg, z_s5, False, "ctx")
    rg_h0 = state_rglru[:, l].transpose(1, 0, 2)
    s5_h0 = jnp.stack([state_s5_re[:, l, 0], state_s5_re[:, l, 1], state_s5_im[:, l, 0], state_s5_im[:, l, 1]],
                      axis=0)
    s5_h0 = s5_h0.transpose(2, 1, 0, 3).reshape(S5_GROUPS, Bs, 256)
    x1_s, _, _ = mixer(x_sample, mod_lat, rg_h0, s5_h0, True, "lat")

    w_router_t = w_router[l].T
    g2 = row(norm2_g[l])

    def route(x1, modp, T, tag):
        h2, aff_t = _premoe(x1, modp, g2, w_router_t, T, "premoe_" + tag)
        slot, slot_t, cnt = _topk(aff_t, T, "topk_" + tag)
        win = _window_tables(cnt, T)
        xg, gates = _gather(h2, slot, aff_t, win[0], win[1], T, "gather_" + tag)
        return xg, gates, slot_t, win

    xg_c, gt_c, st_c, win_c = route(x1_c, mod_ctx1, Tp, "ctx")
    xg_s, gt_s, st_s, win_s = route(x1_s, mod_lat, Ts, "lat")
    y_c, y_s = _ffn(xg_c, xg_s, gt_c, gt_s, w1[l], w3[l], w2[l])
    fg = row(final_g)
    y_prompt = _combine(x1_c, y_c, st_c, *win_c, mod_ctx1, fg, Tp, "combine_ctx").reshape(Bp, Tp, D_MODEL)
    y_sample = _combine(x1_s, y_s, st_s, *win_s, mod_lat, fg, Ts, "combine_lat").reshape(Bs, Ts, D_MODEL)

    new_rg = fin_rg.transpose(1, 0, 2)[:, None]
    f4 = fin_s5.reshape(S5_GROUPS, Bp, 4, 64).transpose(1, 2, 0, 3)
    new_re = f4[:, None, 0:2]
    new_im = f4[:, None, 2:4]
    return (y_prompt, y_sample, new_rg, new_re, new_im)
```
